```python
import jax
import jax.numpy as jnp
from jax import lax
import numpy as np

D_MODEL = 1024
BATCH = 8
SEQ = 4096
DEPTH = 2
DEC_BATCH = 1
DEC_SEQ = 16384
PAST_LEN = 128

GRID_W = 64
CHUNK = 128
A_WIDTH = 512
A_GROUPS = 4
A_GROUP_DIM = A_WIDTH // A_GROUPS
NA_HEADS = 8
NA_HEAD_DIM = 64
NA_WIN_H = 8
NA_WIN_W = 16
NA_WIDTH = NA_HEADS * NA_HEAD_DIM
NA_SCALE = NA_HEAD_DIM ** -0.5
MLA_HEADS = 4
Q_LORA = 384
KV_LORA = 256
QK_NOPE = 128
QK_ROPE = 64
V_DIM = 128
MLA_WIDTH = MLA_HEADS * V_DIM
MLA_SCALE = (QK_NOPE + QK_ROPE) ** -0.5
ROPE_BASE = 10000.0
Q_BLOCK = 128
N_BRANCH = 3
D_IN = 2 * A_WIDTH + 3 * NA_WIDTH + Q_LORA + KV_LORA + QK_ROPE + N_BRANCH * D_MODEL
N_EXPERTS = 16
N_GROUPS = 4
EXPERTS_PER_GROUP = N_EXPERTS // N_GROUPS
TOP_K = 2
D_EXPERT = 256
MOE_BLOCK = 256
DEEPNORM_ALPHA = (2 * DEPTH) ** 0.25
DEEPNORM_BETA = (8 * DEPTH) ** -0.25
LN_EPS = 1e-5
RMS_EPS = 1e-6
NEG_INF = -1e30

kernel_name = "hybrid_gmlp_natten_mla_moe_encoder"


def layer_norm(x, g, b):
    xf = x.astype(jnp.float32)
    mu = jnp.mean(xf, -1, keepdims=True)
    var = jnp.mean(jnp.square(xf - mu), -1, keepdims=True)
    y = (xf - mu) * lax.rsqrt(var + LN_EPS) * g.astype(jnp.float32) + b.astype(jnp.float32)
    return y.astype(x.dtype)


def rms_norm(x, g):
    xf = x.astype(jnp.float32)
    y = xf * lax.rsqrt(jnp.mean(jnp.square(xf), -1, keepdims=True) + RMS_EPS) * g.astype(jnp.float32)
    return y.astype(x.dtype)


def rope_tables(seq):
    inv = 1.0 / (ROPE_BASE ** (jnp.arange(0, QK_ROPE, 2, dtype=jnp.float32) / QK_ROPE))
    ang = jnp.arange(seq, dtype=jnp.float32)[:, None] * inv[None, :]
    return jnp.cos(ang), jnp.sin(ang)


def apply_rope(t, cos, sin):
    half = t.shape[-1] // 2
    t1 = t[..., :half].astype(jnp.float32)
    t2 = t[..., half:].astype(jnp.float32)
    return jnp.concatenate([t1 * cos - t2 * sin, t1 * sin + t2 * cos], -1).astype(t.dtype)


def chunked_spatial_gating(h_a, ln_g, ln_b, w_s, b_s):
    bsz, seq, _ = h_a.shape
    a = jax.nn.gelu(h_a)
    u, v = a[..., :A_WIDTH], a[..., A_WIDTH:]
    v = layer_norm(v, ln_g, ln_b)
    vc = v.reshape(bsz, seq // CHUNK, CHUNK, A_GROUPS, A_GROUP_DIM)
    sp = jnp.einsum('gpq,bnqgc->bnpgc', w_s, vc) + b_s.T[:, :, None]
    return u * sp.reshape(bsz, seq, A_WIDTH)


def neighbourhood_bias(rpb, rows):
    kh = min(NA_WIN_H, rows)
    r = jnp.arange(rows)
    row_idx = jnp.clip(r - kh // 2, 0, rows - kh)[:, None] + jnp.arange(kh)[None, :]
    c = jnp.arange(GRID_W)
    col_start = jnp.clip(c - NA_WIN_W // 2, 0, GRID_W - NA_WIN_W)
    in_win = (c[None, :] >= col_start[:, None]) & (c[None, :] < col_start[:, None] + NA_WIN_W)
    dr = row_idx - r[:, None] + (NA_WIN_H - 1)
    dc = jnp.clip(c[None, :] - c[:, None] + (NA_WIN_W - 1), 0, 2 * NA_WIN_W - 2)
    bias = rpb[:, dr[:, None, :, None], dc[None, :, None, :]].astype(jnp.float32)
    bias = jnp.where(in_win[None, None, :, None, :], bias, NEG_INF)
    return row_idx, bias


def neighbourhood_attention(q, k, v, rpb):
    bsz, seq, _ = q.shape
    rows = seq // GRID_W
    kh = min(NA_WIN_H, rows)
    row_idx, bias = neighbourhood_bias(rpb, rows)

    def one(args):
        qi, ki, vi = [t.reshape(rows, GRID_W, NA_HEADS, NA_HEAD_DIM) for t in args]
        kb = ki[row_idx]
        vb = vi[row_idx]
        s = jnp.einsum('rchd,rkwhd->hrckw', qi, kb).astype(jnp.float32) * NA_SCALE + bias
        p = jax.nn.softmax(s.reshape(NA_HEADS, rows, GRID_W, kh * GRID_W), -1)
        p = p.reshape(s.shape).astype(vi.dtype)
        o = jnp.einsum('hrckw,rkwhd->rchd', p, vb)
        return o.reshape(seq, NA_WIDTH)

    return lax.map(one, (q, k, v))


def latent_attention(c_q, c_kv, k_rope_in, q_norm, kv_norm, w_uq, w_ukv):
    bsz, seq, _ = c_q.shape
    cos, sin = rope_tables(seq)
    q = jnp.einsum('bsr,re->bse', rms_norm(c_q, q_norm), w_uq)
    q = q.reshape(bsz, seq, MLA_HEADS, QK_NOPE + QK_ROPE)
    q_nope = q[..., :QK_NOPE]
    q_rope = apply_rope(q[..., QK_NOPE:], cos[:, None], sin[:, None])
    kv = jnp.einsum('bsr,re->bse', rms_norm(c_kv, kv_norm), w_ukv)
    kv = kv.reshape(bsz, seq, MLA_HEADS, QK_NOPE + V_DIM)
    k_nope, val = kv[..., :QK_NOPE], kv[..., QK_NOPE:]
    k_rope = apply_rope(k_rope_in, cos, sin)
    nb = seq // Q_BLOCK

    def blocks(t):
        return jnp.moveaxis(t.reshape(bsz, nb, Q_BLOCK, MLA_HEADS, t.shape[-1]), 1, 0)

    def one(args):
        qn, qr = args
        s = jnp.einsum('bqhd,bkhd->bhqk', qn, k_nope) + jnp.einsum('bqhr,bkr->bhqk', qr, k_rope)
        p = jax.nn.softmax(s.astype(jnp.float32) * MLA_SCALE, -1).astype(val.dtype)
        return jnp.einsum('bhqk,bkhd->bqhd', p, val)

    o = lax.map(one, (blocks(q_nope), blocks(q_rope)))
    return jnp.moveaxis(o, 0, 1).reshape(bsz, seq, MLA_WIDTH)


def mixer_block(x, w_in, a_ln_g, a_ln_b, a_ws, a_bs, na_rpb, mla_q_norm, mla_kv_norm,
                mla_w_uq, mla_w_ukv, w_br_a, w_br_b, w_br_c, w_o):
    bsz, seq, _ = x.shape
    h = jnp.einsum('bsd,de->bse', x, w_in)
    sizes = (2 * A_WIDTH, NA_WIDTH, NA_WIDTH, NA_WIDTH, Q_LORA, KV_LORA, QK_ROPE, N_BRANCH * D_MODEL)
    cuts = np.cumsum(sizes)[:-1].tolist()
    h_a, h_q, h_k, h_v, h_cq, h_ckv, h_kr, h_g = jnp.split(h, cuts, axis=-1)
    o_a = chunked_spatial_gating(h_a, a_ln_g, a_ln_b, a_ws, a_bs)
    o_b = neighbourhood_attention(h_q, h_k, h_v, na_rpb)
    o_c = latent_attention(h_cq, h_ckv, h_kr, mla_q_norm, mla_kv_norm, mla_w_uq, mla_w_ukv)
    g = jax.nn.sigmoid(h_g.reshape(bsz, seq, N_BRANCH, D_MODEL))
    merged = (g[:, :, 0] * (o_a @ w_br_a) + g[:, :, 1] * (o_b @ w_br_b)
              + g[:, :, 2] * (o_c @ w_br_c))
    return merged @ w_o


def routed_moe(x, w_router, router_bias, w_gate, w_up, w_down):
    shape = x.shape
    xt = x.reshape(-1, D_MODEL)
    n_tok = xt.shape[0]
    scores = jax.nn.sigmoid(jnp.dot(xt, w_router).astype(jnp.float32))
    biased = scores + router_bias.astype(jnp.float32)
    grp_score = lax.top_k(biased.reshape(n_tok, N_GROUPS, EXPERTS_PER_GROUP), 2)[0].sum(-1)
    grp = jnp.argmax(grp_score, -1)
    in_grp = (jnp.arange(N_EXPERTS) // EXPERTS_PER_GROUP)[None, :] == grp[:, None]
    _, eid = lax.top_k(jnp.where(in_grp, biased, NEG_INF), TOP_K)
    gate = jnp.take_along_axis(scores, eid, -1)
    gate = gate / jnp.sum(gate, -1, keepdims=True)
    n_assign = n_tok * TOP_K
    flat_e = eid.reshape(-1)
    order = jnp.argsort(flat_e)
    se = flat_e[order]
    stok = order // TOP_K
    sgate = gate.reshape(-1)[order]
    counts = jnp.bincount(flat_e, length=N_EXPERTS)
    padded = (counts + MOE_BLOCK - 1) // MOE_BLOCK * MOE_BLOCK
    pad_end = jnp.cumsum(padded)
    pad_start = pad_end - padded
    first = jnp.cumsum(counts) - counts
    dest = pad_start[se] + jnp.arange(n_assign) - first[se]
    n_blocks = -(-n_assign // MOE_BLOCK) + N_EXPERTS
    n_slots = n_blocks * MOE_BLOCK
    slot_tok = jnp.zeros((n_slots,), jnp.int32).at[dest].set(stok.astype(jnp.int32))
    slot_gate = jnp.zeros((n_slots,), jnp.float32).at[dest].set(sgate)
    block_e = jnp.minimum(
        jnp.searchsorted(pad_end, jnp.arange(n_blocks) * MOE_BLOCK, side='right'), N_EXPERTS - 1)
    xb = xt[slot_tok].reshape(n_blocks, MOE_BLOCK, D_MODEL)

    def expert(args):
        xi, e = args
        hid = jax.nn.silu(xi @ w_gate[e]) * (xi @ w_up[e])
        return hid @ w_down[e]

    yb = lax.map(expert, (xb, block_e)).reshape(n_slots, D_MODEL)
    y = jnp.zeros_like(xt).at[slot_tok].add(yb * slot_gate[:, None].astype(yb.dtype))
    return y.reshape(shape)


def encoder_trunk(x, w_in, a_ln_g, a_ln_b, a_ws, a_bs, na_rpb, mla_q_norm, mla_kv_norm,
                  mla_w_uq, mla_w_ukv, w_br_a, w_br_b, w_br_c, w_o, ln1_g, ln1_b, ln2_g, ln2_b,
                  w_router, router_bias, w_gate, w_up, w_down):
    for l in range(DEPTH):
        m = mixer_block(x, w_in[l], a_ln_g[l], a_ln_b[l], a_ws[l], a_bs[l], na_rpb[l],
                        mla_q_norm[l], mla_kv_norm[l], mla_w_uq[l], mla_w_ukv[l],
                        w_br_a[l], w_br_b[l], w_br_c[l], w_o[l])
        x = layer_norm(DEEPNORM_ALPHA * x + m, ln1_g[l], ln1_b[l])
        f = routed_moe(x, w_router, router_bias, w_gate[l], w_up[l], w_down[l])
        x = layer_norm(DEEPNORM_ALPHA * x + f, ln2_g[l], ln2_b[l])
    return x


def setup_inputs(seed: int = 0) -> dict:
    key = jax.random.key(seed)
    ks = jax.random.split(key, 26)

    def nrm(k, shape, scale):
        return jax.random.normal(k, shape, jnp.float32) * scale

    L = DEPTH
    return {
        'x_prompt': nrm(ks[0], (BATCH, SEQ, D_MODEL), 1.0),
        'x_sample': nrm(ks[1], (DEC_BATCH, DEC_SEQ, D_MODEL), 1.0),
        'w_in': nrm(ks[2], (L, D_MODEL, D_IN), D_MODEL ** -0.5),
        'a_ln_g': 1.0 + nrm(ks[3], (L, A_WIDTH), 0.01),
        'a_ln_b': nrm(ks[4], (L, A_WIDTH), 0.01),
        'a_ws': nrm(ks[5], (L, A_GROUPS, CHUNK, CHUNK), CHUNK ** -0.5),
        'a_bs': 1.0 + nrm(ks[6], (L, A_GROUPS, CHUNK), 0.01),
        'na_rpb': nrm(ks[7], (L, NA_HEADS, 2 * NA_WIN_H - 1, 2 * NA_WIN_W - 1), 0.1),
        'mla_q_norm': 1.0 + nrm(ks[8], (L, Q_LORA), 0.01),
        'mla_kv_norm': 1.0 + nrm(ks[9], (L, KV_LORA), 0.01),
        'mla_w_uq': nrm(ks[10], (L, Q_LORA, MLA_HEADS * (QK_NOPE + QK_ROPE)), Q_LORA ** -0.5),
        'mla_w_ukv': nrm(ks[11], (L, KV_LORA, MLA_HEADS * (QK_NOPE + V_DIM)), KV_LORA ** -0.5),
        'w_br_a': nrm(ks[12], (L, A_WIDTH, D_MODEL), DEEPNORM_BETA * A_WIDTH ** -0.5),
        'w_br_b': nrm(ks[13], (L, NA_WIDTH, D_MODEL), DEEPNORM_BETA * NA_WIDTH ** -0.5),
        'w_br_c': nrm(ks[14], (L, MLA_WIDTH, D_MODEL), DEEPNORM_BETA * MLA_WIDTH ** -0.5),
        'w_o': nrm(ks[15], (L, D_MODEL, D_MODEL), DEEPNORM_BETA * D_MODEL ** -0.5),
        'ln1_g': 1.0 + nrm(ks[16], (L, D_MODEL), 0.01),
        'ln1_b': nrm(ks[17], (L, D_MODEL), 0.01),
        'ln2_g': 1.0 + nrm(ks[18], (L, D_MODEL), 0.01),
        'ln2_b': nrm(ks[19], (L, D_MODEL), 0.01),
        'w_router': nrm(ks[20], (D_MODEL, N_EXPERTS), D_MODEL ** -0.5),
        'router_bias': nrm(ks[21], (N_EXPERTS,), 0.01),
        'w_gate': nrm(ks[22], (L, N_EXPERTS, D_MODEL, D_EXPERT), D_MODEL ** -0.5),
        'w_up': nrm(ks[23], (L, N_EXPERTS, D_MODEL, D_EXPERT), D_MODEL ** -0.5),
        'w_down': nrm(ks[24], (L, N_EXPERTS, D_EXPERT, D_MODEL), DEEPNORM_BETA * D_EXPERT ** -0.5),
    }


def reference(x_prompt, x_sample, w_in, a_ln_g, a_ln_b, a_ws, a_bs, na_rpb, mla_q_norm,
              mla_kv_norm, mla_w_uq, mla_w_ukv, w_br_a, w_br_b, w_br_c, w_o, ln1_g, ln1_b,
              ln2_g, ln2_b, w_router, router_bias, w_gate, w_up, w_down):
    y_prompt = encoder_trunk(x_prompt, w_in, a_ln_g, a_ln_b, a_ws, a_bs, na_rpb, mla_q_norm,
                             mla_kv_norm, mla_w_uq, mla_w_ukv, w_br_a, w_br_b, w_br_c, w_o,
                             ln1_g, ln1_b, ln2_g, ln2_b, w_router, router_bias, w_gate, w_up, w_down)
    y_sample = encoder_trunk(x_sample, w_in, a_ln_g, a_ln_b, a_ws, a_bs, na_rpb, mla_q_norm,
                             mla_kv_norm, mla_w_uq, mla_w_ukv, w_br_a, w_br_b, w_br_c, w_o,
                             ln1_g, ln1_b, ln2_g, ln2_b, w_router, router_bias, w_gate, w_up, w_down)
    return (y_prompt, y_sample)
```

```python
import functools
import math

import jax
import jax.numpy as jnp
import numpy as np
from jax import lax
from jax.experimental import pallas as pl
from jax.experimental.pallas import tpu as pltpu

F32 = jnp.float32
BF16 = jnp.bfloat16

D_MODEL = 1024
DEPTH = 2
GRID_W = 64
CHUNK = 128
A_WIDTH = 512
A_GROUPS = 4
NA_HEADS = 8
NA_HEAD_DIM = 64
NA_WIN_H = 8
NA_WIN_W = 16
NA_WIDTH = NA_HEADS * NA_HEAD_DIM
NA_SCALE = NA_HEAD_DIM ** -0.5
MLA_HEADS = 4
Q_LORA = 384
KV_LORA = 256
QK_NOPE = 128
QK_ROPE = 64
V_DIM = 128
MLA_WIDTH = MLA_HEADS * V_DIM
MLA_SCALE = (QK_NOPE + QK_ROPE) ** -0.5
ROPE_BASE = 10000.0
N_BRANCH = 3
N_EXPERTS = 16
N_GROUPS = 4
EXPERTS_PER_GROUP = N_EXPERTS // N_GROUPS
TOP_K = 2
D_EXPERT = 256
MOE_BLOCK = 256
DEEPNORM_ALPHA = (2 * DEPTH) ** 0.25
LN_EPS = 1e-5
RMS_EPS = 1e-6
NEG_INF = -1e30
LOG2E = math.log2(math.e)

VMEM_LIMIT_BYTES = 56 * 1024 * 1024
LANES = 128
MLA_HEAD_PAD = 2 * LANES
NA_ROW_BLOCK = 8
NA_TOK_BLOCK = NA_ROW_BLOCK * GRID_W

PRE_TM = 512
MERGE_TM = 512
COMBINE_TM = 256
MLA_TQ = 512
MLA_TK = 1024


def _params(*sem):
    return pltpu.CompilerParams(dimension_semantics=sem, vmem_limit_bytes=VMEM_LIMIT_BYTES)


def _const_spec(shape):
    nd = len(shape)
    return pl.BlockSpec(shape, lambda *_: (0,) * nd)


def _layer_norm(y, g, b):
    mu = jnp.mean(y, -1, keepdims=True)
    yc = y - mu
    var = jnp.mean(yc * yc, -1, keepdims=True)
    return yc * lax.rsqrt(var + LN_EPS) * g + b


def _rms_norm(y, g):
    return y * lax.rsqrt(jnp.mean(y * y, -1, keepdims=True) + RMS_EPS) * g


def _gelu_tanh(x):
    return 0.5 * x * (1.0 + jnp.tanh(math.sqrt(2.0 / math.pi) * (x + 0.044715 * (x * x * x))))


def _sigmoid(x):
    return 1.0 / (1.0 + jnp.exp(-x))


def _dot(a, b):
    return jnp.dot(a, b, preferred_element_type=F32)


def _dot_nt(a, b):
    return lax.dot_general(a, b, (((1,), (1,)), ((), ())), preferred_element_type=F32)


def _pre_kernel(x_ref, cos_ref, sin_ref, wa_ref, lng_ref, lnb_ref, ws_ref, bs_ref, wqkv_ref,
                wcq_ref, qnorm_ref, wqn_ref, wqr_ref, wqs_ref, wckv_ref, kvnorm_ref, wukv_ref,
                wkr_ref, wkrs_ref,
                oa_ref, q_ref, k_ref, v_ref, qm_ref, km_ref, vm_ref):
    tm = x_ref.shape[0]
    xb = x_ref[...].astype(BF16)
    cos2 = cos_ref[...]
    sin2 = sin_ref[...]

    a = _gelu_tanh(_dot(xb, wa_ref[...]))
    u = a[:, :A_WIDTH]
    vn = _layer_norm(a[:, A_WIDTH:], lng_ref[...], lnb_ref[...]).astype(BF16)
    for c in range(tm // CHUNK):
        rs = slice(c * CHUNK, (c + 1) * CHUNK)
        for g in range(A_GROUPS):
            cs = slice(g * CHUNK, (g + 1) * CHUNK)
            sp = _dot(ws_ref[g], vn[rs, cs]) + bs_ref[g]
            oa_ref[rs, cs] = (u[rs, cs] * sp).astype(BF16)

    qkv = _dot(xb, wqkv_ref[...])
    q_ref[...] = qkv[:, :NA_WIDTH].astype(BF16)
    k_ref[...] = qkv[:, NA_WIDTH:2 * NA_WIDTH].astype(BF16)
    v_ref[...] = qkv[:, 2 * NA_WIDTH:].astype(BF16)

    cqn = _rms_norm(_dot(xb, wcq_ref[...]), qnorm_ref[...]).astype(BF16)
    qn = _dot(cqn, wqn_ref[...])
    qr = _dot(cqn, wqr_ref[...])
    qs = _dot(cqn, wqs_ref[...])
    qscale = MLA_SCALE * LOG2E
    for h in range(MLA_HEADS):
        hs = slice(h * LANES, (h + 1) * LANES)
        qm_ref[:, h * MLA_HEAD_PAD:h * MLA_HEAD_PAD + LANES] = (qn[:, hs] * qscale).astype(BF16)
        rot = qr[:, hs] * cos2 + qs[:, hs] * sin2
        qm_ref[:, h * MLA_HEAD_PAD + LANES:(h + 1) * MLA_HEAD_PAD] = (rot * qscale).astype(BF16)

    ckvn = _rms_norm(_dot(xb, wckv_ref[...]), kvnorm_ref[...]).astype(BF16)
    kv = _dot(ckvn, wukv_ref[...])
    krope = (_dot(xb, wkr_ref[...]) * cos2 + _dot(xb, wkrs_ref[...]) * sin2).astype(BF16)
    for h in range(MLA_HEADS):
        km_ref[:, h * MLA_HEAD_PAD:h * MLA_HEAD_PAD + LANES] = kv[:, h * LANES:(h + 1) * LANES].astype(BF16)
        km_ref[:, h * MLA_HEAD_PAD + LANES:(h + 1) * MLA_HEAD_PAD] = krope
    vm_ref[...] = kv[:, MLA_HEADS * QK_NOPE:].astype(BF16)


def _pre_call(x2d, seq, cos2, sin2, w):
    n = x2d.shape[0]
    tm = PRE_TM
    per_seq = seq // tm
    weights = [w['wa'], w['a_ln_g'], w['a_ln_b'], w['ws'], w['bs'], w['wqkv'], w['wcq'], w['q_norm'],
               w['wqn'], w['wqr'], w['wqs'], w['wckv'], w['kv_norm'], w['wukv'], w['wkr'], w['wkrs']]
    tok = lambda width: pl.BlockSpec((tm, width), lambda i: (i, 0))
    pos = pl.BlockSpec((tm, LANES), lambda i: (i % per_seq, 0))
    out_widths = [A_WIDTH, NA_WIDTH, NA_WIDTH, NA_WIDTH, MLA_HEADS * MLA_HEAD_PAD,
                  MLA_HEADS * MLA_HEAD_PAD, MLA_WIDTH]
    return pl.pallas_call(
        _pre_kernel,
        grid=(n // tm,),
        in_specs=[tok(D_MODEL), pos, pos] + [_const_spec(t.shape) for t in weights],
        out_specs=[tok(wd) for wd in out_widths],
        out_shape=[jax.ShapeDtypeStruct((n, wd), BF16) for wd in out_widths],
        compiler_params=_params("parallel"),
        name="pre",
    )(x2d, cos2, sin2, *weights)


def _natten_kernel(q_ref, kp_ref, kc_ref, kn_ref, vp_ref, vc_ref, vn_ref, bias_ref, o_ref,
                   kcat, vcat, *, rows):
    j = pl.program_id(1)
    nt = NA_TOK_BLOCK
    kcat[0:nt] = kp_ref[...]
    kcat[nt:2 * nt] = kc_ref[...]
    kcat[2 * nt:3 * nt] = kn_ref[...]
    vcat[0:nt] = vp_ref[...]
    vcat[nt:2 * nt] = vc_ref[...]
    vcat[2 * nt:3 * nt] = vn_ref[...]
    r0 = j * NA_ROW_BLOCK
    lane_lo = lax.broadcasted_iota(jnp.int32, (GRID_W, LANES), 1) < NA_HEAD_DIM
    for i in range(NA_ROW_BLOCK):
        r = r0 + i
        start = jnp.clip(r - NA_WIN_H // 2, 0, rows - NA_WIN_H)
        pat = start - r + (NA_WIN_H - 1)
        koff = pl.multiple_of((start - r0 + NA_ROW_BLOCK) * GRID_W, GRID_W)
        qs = slice(i * GRID_W, (i + 1) * GRID_W)
        for hp in range(NA_HEADS // 2):
            cs = slice(hp * LANES, (hp + 1) * LANES)
            qp = q_ref[qs, cs]
            zero = jnp.zeros_like(qp)
            q2 = jnp.concatenate([jnp.where(lane_lo, qp, zero), jnp.where(lane_lo, zero, qp)], axis=0)
            kk = kcat[pl.ds(koff, NA_WIN_H * GRID_W), cs]
            vv = vcat[pl.ds(koff, NA_WIN_H * GRID_W), cs]
            s = _dot_nt(q2, kk) + bias_ref[pat, hp]
            m = jnp.max(s, -1, keepdims=True)
            e = jnp.exp(s - m)
            l = jnp.sum(e, -1, keepdims=True)
            o2 = _dot(e.astype(BF16), vv) / l
            o_ref[qs, cs] = jnp.where(lane_lo, o2[:GRID_W], o2[GRID_W:]).astype(BF16)


def _natten_call(q, k, v, bias, bsz, seq):
    rows = seq // GRID_W
    nrb = rows // NA_ROW_BLOCK
    nt = NA_TOK_BLOCK
    cur = pl.BlockSpec((nt, NA_WIDTH), lambda b, j: (b * nrb + j, 0))
    prev = pl.BlockSpec((nt, NA_WIDTH), lambda b, j: (b * nrb + jnp.maximum(j - 1, 0), 0))
    nxt = pl.BlockSpec((nt, NA_WIDTH), lambda b, j: (b * nrb + jnp.minimum(j + 1, nrb - 1), 0))
    return pl.pallas_call(
        functools.partial(_natten_kernel, rows=rows),
        grid=(bsz, nrb),
        in_specs=[cur, prev, cur, nxt, prev, cur, nxt, _const_spec(bias.shape)],
        out_specs=cur,
        out_shape=jax.ShapeDtypeStruct((bsz * seq, NA_WIDTH), BF16),
        scratch_shapes=[pltpu.VMEM((3 * nt, NA_WIDTH), BF16), pltpu.VMEM((3 * nt, NA_WIDTH), BF16)],
        compiler_params=_params("parallel", "parallel"),
        name="natten",
    )(q, k, k, k, v, v, v, bias)


def _natten_bias(rpb):
    c = jnp.arange(GRID_W)
    col_start = jnp.clip(c - NA_WIN_W // 2, 0, GRID_W - NA_WIN_W)
    in_win = (c[None, :] >= col_start[:, None]) & (c[None, :] < col_start[:, None] + NA_WIN_W)
    dc = jnp.clip(c[None, :] - c[:, None] + (NA_WIN_W - 1), 0, 2 * NA_WIN_W - 2)
    dr = jnp.arange(NA_WIN_H)[:, None] + jnp.arange(NA_WIN_H)[None, :]
    b = rpb[:, dr[:, None, :, None], dc[None, :, None, :]].astype(F32)
    b = jnp.where(in_win[None, None, :, None, :], b, NEG_INF)
    b = b.reshape(NA_HEADS // 2, 2, NA_WIN_H, GRID_W, NA_WIN_H * GRID_W)
    return b.transpose(2, 0, 1, 3, 4).reshape(NA_WIN_H, NA_HEADS // 2, 2 * GRID_W, NA_WIN_H * GRID_W)


def _mla_kernel(q_ref, k_ref, v_ref, o_ref, *, tk):
    q = q_ref[...]
    tq = q.shape[0]
    nk = k_ref.shape[0] // tk

    def body(j, carry):
        m, l, acc = carry
        ks = pl.ds(pl.multiple_of(j * tk, tk), tk)
        s = _dot_nt(q, k_ref[ks, :])
        m_new = jnp.maximum(m, jnp.max(s, -1, keepdims=True))
        alpha = jnp.exp2(m - m_new)
        p = jnp.exp2(s - m_new)
        l = alpha * l + jnp.sum(p, -1, keepdims=True)
        acc = alpha * acc + _dot(p.astype(BF16), v_ref[ks, :])
        return m_new, l, acc

    init = (jnp.full((tq, 1), NEG_INF, F32), jnp.zeros((tq, 1), F32), jnp.zeros((tq, V_DIM), F32))
    _, l, acc = lax.fori_loop(0, nk, body, init)
    o_ref[...] = (acc / l).astype(BF16)


def _mla_call(qm, km, vm, bsz, seq):
    tq = MLA_TQ
    nq = seq // tq
    tk = min(MLA_TK, seq)
    return pl.pallas_call(
        functools.partial(_mla_kernel, tk=tk),
        grid=(bsz, MLA_HEADS, nq),
        in_specs=[pl.BlockSpec((tq, MLA_HEAD_PAD), lambda b, h, i: (b * nq + i, h)),
                  pl.BlockSpec((seq, MLA_HEAD_PAD), lambda b, h, i: (b, h)),
                  pl.BlockSpec((seq, V_DIM), lambda b, h, i: (b, h))],
        out_specs=pl.BlockSpec((tq, V_DIM), lambda b, h, i: (b * nq + i, h)),
        out_shape=jax.ShapeDtypeStruct((bsz * seq, MLA_WIDTH), BF16),
        compiler_params=_params("parallel", "parallel", "arbitrary"),
        name="mla",
    )(qm, km, vm)


def _first_argmax(vals):
    best, idx = vals[0], jnp.zeros(vals[0].shape, jnp.int32)
    for i in range(1, len(vals)):
        better = vals[i] > best
        idx = jnp.where(better, i, idx)
        best = jnp.where(better, vals[i], best)
    return best, idx


def _select(idx, vals):
    out = vals[0]
    for i in range(1, len(vals)):
        out = jnp.where(idx == i, vals[i], out)
    return out


def _merge_kernel(x_ref, oa_ref, ob_ref, oc_ref, wg_ref, wbr_ref, wo_ref, lng_ref, lnb_ref,
                  wrh_ref, wrl_ref, rb_ref, x1_ref, eid_ref, gate_ref):
    x = x_ref[...]
    xb = x.astype(BF16)
    merged = None
    for i, o_ref in enumerate((oa_ref, ob_ref, oc_ref)):
        g = _sigmoid(_dot(xb, wg_ref[:, i * D_MODEL:(i + 1) * D_MODEL]))
        term = g * _dot(o_ref[...], wbr_ref[i])
        merged = term if merged is None else merged + term
    m = _dot(merged.astype(BF16), wo_ref[...])
    x1 = _layer_norm(DEEPNORM_ALPHA * x + m, lng_ref[...], lnb_ref[...])
    x1_ref[...] = x1

    x1h = x1.astype(BF16)
    x1l = (x1 - x1h.astype(F32)).astype(BF16)
    logits = _dot_nt(wrh_ref[...], x1h) + (_dot_nt(wrl_ref[...], x1h) + _dot_nt(wrh_ref[...], x1l))
    scores = _sigmoid(logits)
    biased = scores + rb_ref[...]
    sc = [scores[e:e + 1, :] for e in range(N_EXPERTS)]
    bi = [biased[e:e + 1, :] for e in range(N_EXPERTS)]
    grp_scores = []
    for g in range(N_GROUPS):
        v = bi[g * EXPERTS_PER_GROUP:(g + 1) * EXPERTS_PER_GROUP]
        top2 = None
        for i in range(EXPERTS_PER_GROUP):
            for k in range(i + 1, EXPERTS_PER_GROUP):
                top2 = v[i] + v[k] if top2 is None else jnp.maximum(top2, v[i] + v[k])
        grp_scores.append(top2)
    _, grp = _first_argmax(grp_scores)
    cand = [_select(grp, [bi[g * EXPERTS_PER_GROUP + i] for g in range(N_GROUPS)])
            for i in range(EXPERTS_PER_GROUP)]
    cand_sc = [_select(grp, [sc[g * EXPERTS_PER_GROUP + i] for g in range(N_GROUPS)])
               for i in range(EXPERTS_PER_GROUP)]
    _, i1 = _first_argmax(cand)
    _, i2 = _first_argmax([jnp.where(i1 == i, -jnp.inf, cand[i]) for i in range(EXPERTS_PER_GROUP)])
    s1 = _select(i1, cand_sc)
    s2 = _select(i2, cand_sc)
    tot = s1 + s2
    eid_ref[0:1, :] = grp * EXPERTS_PER_GROUP + i1
    eid_ref[1:2, :] = grp * EXPERTS_PER_GROUP + i2
    gate_ref[0:1, :] = s1 / tot
    gate_ref[1:2, :] = s2 / tot


def _merge_call(x2d, oa, ob, oc, w, shared):
    n = x2d.shape[0]
    tm = MERGE_TM
    weights = [w['wg'], w['wbr'], w['wo'], w['ln1_g'], w['ln1_b'], shared['wrh'], shared['wrl'], shared['rb']]
    tok = lambda width: pl.BlockSpec((tm, width), lambda i: (i, 0))
    route = pl.BlockSpec((TOP_K, tm), lambda i: (0, i))
    return pl.pallas_call(
        _merge_kernel,
        grid=(n // tm,),
        in_specs=[tok(D_MODEL), tok(A_WIDTH), tok(NA_WIDTH), tok(MLA_WIDTH)]
                 + [_const_spec(t.shape) for t in weights],
        out_specs=[tok(D_MODEL), route, route],
        out_shape=[jax.ShapeDtypeStruct((n, D_MODEL), F32),
                   jax.ShapeDtypeStruct((TOP_K, n), jnp.int32),
                   jax.ShapeDtypeStruct((TOP_K, n), F32)],
        compiler_params=_params("parallel"),
        name="merge",
    )(x2d, oa, ob, oc, *weights)


def _row_copy(src_hbm, row, dst, slot, sem):
    return pltpu.make_async_copy(src_hbm.at[pl.ds(row, 1)], dst.at[pl.ds(slot, 1)], sem)


def _experts_kernel(be_ref, nused_ref, tok_ref, gate_ref, x_hbm, wgu_ref, wd_ref, y_ref, xbuf, sem):
    b = pl.program_id(0)

    @pl.when(b < nused_ref[0])
    def _():
        def issue(j, c):
            _row_copy(x_hbm, tok_ref[0, 0, j], xbuf, j, sem).start()
            return c

        def wait(j, c):
            _row_copy(x_hbm, 0, xbuf, j, sem).wait()
            return c

        lax.fori_loop(0, MOE_BLOCK, issue, 0, unroll=8)
        lax.fori_loop(0, MOE_BLOCK, wait, 0, unroll=8)
        xi = xbuf[...].astype(BF16)
        gu = _dot(xi, wgu_ref[0])
        gt = gu[:, :D_EXPERT]
        hid = gt * _sigmoid(gt) * gu[:, D_EXPERT:]
        gcol = jnp.broadcast_to(gate_ref[0], (LANES, MOE_BLOCK)).T
        hid = hid * jnp.concatenate([gcol] * (D_EXPERT // LANES), axis=1)
        y_ref[...] = _dot(hid.astype(BF16), wd_ref[0])

    @pl.when(b >= nused_ref[0])
    def _():
        y_ref[...] = jnp.zeros_like(y_ref)


def _experts_call(x1, block_e, n_used, slot_tok, slot_gate, w):
    n_blocks = block_e.shape[0]
    grid_spec = pltpu.PrefetchScalarGridSpec(
        num_scalar_prefetch=2,
        grid=(n_blocks,),
        in_specs=[pl.BlockSpec((1, 1, MOE_BLOCK), lambda b, be, nu: (b, 0, 0), memory_space=pltpu.SMEM),
                  pl.BlockSpec((1, 1, MOE_BLOCK), lambda b, be, nu: (b, 0, 0)),
                  pl.BlockSpec(memory_space=pl.ANY),
                  pl.BlockSpec((1, D_MODEL, 2 * D_EXPERT), lambda b, be, nu: (be[b], 0, 0)),
                  pl.BlockSpec((1, D_EXPERT, D_MODEL), lambda b, be, nu: (be[b], 0, 0))],
        out_specs=pl.BlockSpec((MOE_BLOCK, D_MODEL), lambda b, be, nu: (b, 0)),
        scratch_shapes=[pltpu.VMEM((MOE_BLOCK, D_MODEL), F32), pltpu.SemaphoreType.DMA(())],
    )
    return pl.pallas_call(
        _experts_kernel,
        grid_spec=grid_spec,
        out_shape=jax.ShapeDtypeStruct((n_blocks * MOE_BLOCK, D_MODEL), F32),
        compiler_params=_params("arbitrary"),
        name="experts",
    )(block_e, n_used, slot_tok.reshape(n_blocks, 1, MOE_BLOCK), slot_gate.reshape(n_blocks, 1, MOE_BLOCK),
      x1, w['wgu'], w['wd'])


def _combine_kernel(dest_ref, x1_ref, y_hbm, lng_ref, lnb_ref, o_ref, buf, sem):
    tm = x1_ref.shape[0]

    def issue(j, c):
        for k in range(TOP_K):
            _row_copy(y_hbm, dest_ref[0, k, j], buf.at[k], j, sem).start()
        return c

    def wait(j, c):
        for k in range(TOP_K):
            _row_copy(y_hbm, 0, buf.at[k], j, sem).wait()
        return c

    lax.fori_loop(0, tm, issue, 0, unroll=8)
    lax.fori_loop(0, tm, wait, 0, unroll=8)
    y = DEEPNORM_ALPHA * x1_ref[...] + (buf[0] + buf[1])
    o_ref[...] = _layer_norm(y, lng_ref[...], lnb_ref[...])


def _combine_call(x1, yb, dest, w):
    n = x1.shape[0]
    tm = COMBINE_TM
    return pl.pallas_call(
        _combine_kernel,
        grid=(n // tm,),
        in_specs=[pl.BlockSpec((1, TOP_K, tm), lambda i: (i, 0, 0), memory_space=pltpu.SMEM),
                  pl.BlockSpec((tm, D_MODEL), lambda i: (i, 0)),
                  pl.BlockSpec(memory_space=pl.ANY),
                  _const_spec(w['ln2_g'].shape), _const_spec(w['ln2_b'].shape)],
        out_specs=pl.BlockSpec((tm, D_MODEL), lambda i: (i, 0)),
        out_shape=jax.ShapeDtypeStruct((n, D_MODEL), F32),
        scratch_shapes=[pltpu.VMEM((TOP_K, tm, D_MODEL), F32), pltpu.SemaphoreType.DMA(())],
        compiler_params=_params("arbitrary"),
        name="combine",
    )(dest.reshape(TOP_K, n // tm, tm).transpose(1, 0, 2), x1, yb, w['ln2_g'], w['ln2_b'])


def _dispatch_plan(eid, gate):
    n = eid.shape[1]
    n_blocks = -(-(n * TOP_K) // MOE_BLOCK) + N_EXPERTS
    n_slots = n_blocks * MOE_BLOCK
    experts = jnp.arange(N_EXPERTS, dtype=jnp.int32)
    onehot = ((eid[0][:, None] == experts) | (eid[1][:, None] == experts)).astype(jnp.int32)
    incl = jnp.cumsum(onehot, axis=0)
    counts = incl[-1]
    rank = incl - onehot
    padded = (counts + MOE_BLOCK - 1) // MOE_BLOCK * MOE_BLOCK
    pad_end = jnp.cumsum(padded)
    pad_start = pad_end - padded
    dest = pad_start[eid] + jnp.take_along_axis(rank, eid.T, axis=1).T
    tok = jnp.broadcast_to(jnp.arange(n, dtype=jnp.int32), (TOP_K, n))
    slot_tok = jnp.zeros((n_slots,), jnp.int32).at[dest.reshape(-1)].set(tok.reshape(-1))
    slot_gate = jnp.zeros((n_slots,), F32).at[dest.reshape(-1)].set(gate.reshape(-1))
    block_e = jnp.minimum(
        jnp.searchsorted(pad_end, jnp.arange(n_blocks, dtype=jnp.int32) * MOE_BLOCK, side='right'),
        N_EXPERTS - 1).astype(jnp.int32)
    n_used = (pad_end[-1:] // MOE_BLOCK).astype(jnp.int32)
    return dest.astype(jnp.int32), slot_tok, slot_gate, block_e, n_used


def _rope_tables(seq):
    inv = 1.0 / (ROPE_BASE ** (jnp.arange(0, QK_ROPE, 2, dtype=F32) / QK_ROPE))
    ang = jnp.arange(seq, dtype=F32)[:, None] * inv[None, :]
    pad = jnp.zeros((seq, LANES - QK_ROPE), F32)
    cos, sin = jnp.cos(ang), jnp.sin(ang)
    return jnp.concatenate([cos, cos, pad], -1), jnp.concatenate([sin, sin, pad], -1)


def _pad_cols(t, width):
    return jnp.pad(t, ((0, 0), (0, width - t.shape[1])))


def _rope_weight_pair(w_rope):
    half = QK_ROPE // 2
    swapped = jnp.concatenate([-w_rope[:, half:], w_rope[:, :half]], axis=1)
    return _pad_cols(w_rope, LANES), _pad_cols(swapped, LANES)


def _layer_weights(l, w_in, a_ln_g, a_ln_b, a_ws, a_bs, na_rpb, mla_q_norm, mla_kv_norm, mla_w_uq,
                   mla_w_ukv, w_br_a, w_br_b, w_br_c, w_o, ln1_g, ln1_b, ln2_g, ln2_b, w_gate, w_up,
                   w_down):
    sizes = (2 * A_WIDTH, NA_WIDTH, NA_WIDTH, NA_WIDTH, Q_LORA, KV_LORA, QK_ROPE, N_BRANCH * D_MODEL)
    cuts = np.cumsum(sizes)[:-1].tolist()
    wa, wq, wk, wv, wcq, wckv, wkr, wg = jnp.split(w_in[l], cuts, axis=-1)
    row = lambda t: t.reshape(1, -1).astype(F32)
    uq = mla_w_uq[l].reshape(Q_LORA, MLA_HEADS, QK_NOPE + QK_ROPE)
    wqn = uq[:, :, :QK_NOPE].reshape(Q_LORA, MLA_HEADS * QK_NOPE)
    rope_pairs = [_rope_weight_pair(uq[:, h, QK_NOPE:]) for h in range(MLA_HEADS)]
    wqr = jnp.concatenate([p[0] for p in rope_pairs], axis=1)
    wqs = jnp.concatenate([p[1] for p in rope_pairs], axis=1)
    ukv = mla_w_ukv[l].reshape(KV_LORA, MLA_HEADS, QK_NOPE + V_DIM)
    wukv = jnp.concatenate([ukv[:, :, :QK_NOPE].reshape(KV_LORA, -1), ukv[:, :, QK_NOPE:].reshape(KV_LORA, -1)], 1)
    wkr_p, wkrs_p = _rope_weight_pair(wkr)
    return {
        'wa': wa.astype(BF16), 'a_ln_g': row(a_ln_g[l]), 'a_ln_b': row(a_ln_b[l]),
        'ws': a_ws[l].astype(BF16),
        'bs': jnp.broadcast_to(a_bs[l][:, :, None], (A_GROUPS, CHUNK, CHUNK)).astype(F32),
        'wqkv': jnp.concatenate([wq * NA_SCALE, wk, wv], axis=1).astype(BF16),
        'wcq': wcq.astype(BF16), 'q_norm': row(mla_q_norm[l]),
        'wqn': wqn.astype(BF16), 'wqr': wqr.astype(BF16), 'wqs': wqs.astype(BF16),
        'wckv': wckv.astype(BF16), 'kv_norm': row(mla_kv_norm[l]), 'wukv': wukv.astype(BF16),
        'wkr': wkr_p.astype(BF16), 'wkrs': wkrs_p.astype(BF16),
        'na_bias': _natten_bias(na_rpb[l]),
        'wg': wg.astype(BF16),
        'wbr': jnp.stack([w_br_a[l], w_br_b[l], w_br_c[l]]).astype(BF16),
        'wo': w_o[l].astype(BF16), 'ln1_g': row(ln1_g[l]), 'ln1_b': row(ln1_b[l]),
        'ln2_g': row(ln2_g[l]), 'ln2_b': row(ln2_b[l]),
        'wgu': jnp.concatenate([w_gate[l], w_up[l]], axis=-1).astype(BF16),
        'wd': w_down[l].astype(BF16),
    }


def _shared_weights(w_router, router_bias):
    wr_t = w_router.T.astype(F32)
    wrh = wr_t.astype(BF16)
    wrl = (wr_t - wrh.astype(F32)).astype(BF16)
    rb = jnp.broadcast_to(router_bias.astype(F32)[:, None], (N_EXPERTS, MERGE_TM))
    return {'wrh': wrh, 'wrl': wrl, 'rb': rb}


def _trunk(x, layers, shared):
    bsz, seq, _ = x.shape
    cos2, sin2 = _rope_tables(seq)
    x2d = x.reshape(bsz * seq, D_MODEL)
    for w in layers:
        oa, q, k, v, qm, km, vm = _pre_call(x2d, seq, cos2, sin2, w)
        ob = _natten_call(q, k, v, w['na_bias'], bsz, seq)
        oc = _mla_call(qm, km, vm, bsz, seq)
        x1, eid, gate = _merge_call(x2d, oa, ob, oc, w, shared)
        dest, slot_tok, slot_gate, block_e, n_used = _dispatch_plan(eid, gate)
        yb = _experts_call(x1, block_e, n_used, slot_tok, slot_gate, w)
        x2d = _combine_call(x1, yb, dest, w)
    return x2d.reshape(bsz, seq, D_MODEL)


def kernel(x_prompt, x_sample, w_in, a_ln_g, a_ln_b, a_ws, a_bs, na_rpb, mla_q_norm, mla_kv_norm, mla_w_uq, mla_w_ukv, w_br_a, w_br_b, w_br_c, w_o, ln1_g, ln1_b, ln2_g, ln2_b, w_router, router_bias, w_gate, w_up, w_down):
    layers = [_layer_weights(l, w_in, a_ln_g, a_ln_b, a_ws, a_bs, na_rpb, mla_q_norm, mla_kv_norm,
                             mla_w_uq, mla_w_ukv, w_br_a, w_br_b, w_br_c, w_o, ln1_g, ln1_b, ln2_g,
                             ln2_b, w_gate, w_up, w_down) for l in range(DEPTH)]
    shared = _shared_weights(w_router, router_bias)
    return (_trunk(x_prompt, layers, shared), _trunk(x_sample, layers, shared))
```

```python
import functools
import math

import jax
import jax.numpy as jnp
import numpy as np
from jax import lax
from jax.experimental import pallas as pl
from jax.experimental.pallas import tpu as pltpu

F32 = jnp.float32
BF16 = jnp.bfloat16

D_MODEL = 1024
DEPTH = 2
GRID_W = 64
CHUNK = 128
A_WIDTH = 512
A_GROUPS = 4
NA_HEADS = 8
NA_HEAD_DIM = 64
NA_WIN_H = 8
NA_WIN_W = 16
NA_WIDTH = NA_HEADS * NA_HEAD_DIM
NA_SCALE = NA_HEAD_DIM ** -0.5
MLA_HEADS = 4
Q_LORA = 384
KV_LORA = 256
QK_NOPE = 128
QK_ROPE = 64
V_DIM = 128
MLA_WIDTH = MLA_HEADS * V_DIM
MLA_SCALE = (QK_NOPE + QK_ROPE) ** -0.5
ROPE_BASE = 10000.0
N_BRANCH = 3
N_EXPERTS = 16
N_GROUPS = 4
EXPERTS_PER_GROUP = N_EXPERTS // N_GROUPS
TOP_K = 2
D_EXPERT = 256
MOE_BLOCK = 256
DEEPNORM_ALPHA = (2 * DEPTH) ** 0.25
LN_EPS = 1e-5
RMS_EPS = 1e-6
NEG_INF = -1e30
LOG2E = math.log2(math.e)

VMEM_LIMIT_BYTES = 56 * 1024 * 1024
LANES = 128
MLA_HEAD_PAD = 2 * LANES
NA_ROW_BLOCK = 8
NA_TOK_BLOCK = NA_ROW_BLOCK * GRID_W

PRE_TM = 512
MERGE_TM = 512
COMBINE_TM = 256
MLA_TQ = 512
MLA_TK = 1024


def _params(*sem):
    return pltpu.CompilerParams(dimension_semantics=sem, vmem_limit_bytes=VMEM_LIMIT_BYTES)


def _const_spec(shape):
    nd = len(shape)
    return pl.BlockSpec(shape, lambda *_: (0,) * nd)


def _layer_norm(y, g, b):
    mu = jnp.mean(y, -1, keepdims=True)
    yc = y - mu
    var = jnp.mean(yc * yc, -1, keepdims=True)
    return yc * lax.rsqrt(var + LN_EPS) * g + b


def _rms_norm(y, g):
    return y * lax.rsqrt(jnp.mean(y * y, -1, keepdims=True) + RMS_EPS) * g


def _gelu_tanh(x):
    return 0.5 * x * (1.0 + jnp.tanh(math.sqrt(2.0 / math.pi) * (x + 0.044715 * (x * x * x))))


def _sigmoid(x):
    return 1.0 / (1.0 + jnp.exp(-x))


def _dot(a, b):
    return jnp.dot(a, b, preferred_element_type=F32)


def _dot_nt(a, b):
    return lax.dot_general(a, b, (((1,), (1,)), ((), ())), preferred_element_type=F32)


def _pre_kernel(x_ref, cos_ref, sin_ref, wa_ref, lng_ref, lnb_ref, ws_ref, bs_ref, wqkv_ref,
                wcq_ref, qnorm_ref, wqn_ref, wqr_ref, wqs_ref, wckv_ref, kvnorm_ref, wukv_ref,
                wkr_ref, wkrs_ref,
                oa_ref, q_ref, k_ref, v_ref, qm_ref, km_ref, vm_ref):
    tm = x_ref.shape[0]
    xb = x_ref[...].astype(BF16)
    cos2 = cos_ref[...]
    sin2 = sin_ref[...]

    a = _gelu_tanh(_dot(xb, wa_ref[...]))
    u = a[:, :A_WIDTH]
    vn = _layer_norm(a[:, A_WIDTH:], lng_ref[...], lnb_ref[...]).astype(BF16)
    for c in range(tm // CHUNK):
        rs = slice(c * CHUNK, (c + 1) * CHUNK)
        for g in range(A_GROUPS):
            cs = slice(g * CHUNK, (g + 1) * CHUNK)
            sp = _dot(ws_ref[g], vn[rs, cs]) + bs_ref[g]
            oa_ref[rs, cs] = (u[rs, cs] * sp).astype(BF16)

    qkv = _dot(xb, wqkv_ref[...])
    q_ref[...] = qkv[:, :NA_WIDTH].astype(BF16)
    k_ref[...] = qkv[:, NA_WIDTH:2 * NA_WIDTH].astype(BF16)
    v_ref[...] = qkv[:, 2 * NA_WIDTH:].astype(BF16)

    cqn = _rms_norm(_dot(xb, wcq_ref[...]), qnorm_ref[...]).astype(BF16)
    qn = _dot(cqn, wqn_ref[...])
    qr = _dot(cqn, wqr_ref[...])
    qs = _dot(cqn, wqs_ref[...])
    qscale = MLA_SCALE * LOG2E
    for h in range(MLA_HEADS):
        hs = slice(h * LANES, (h + 1) * LANES)
        qm_ref[:, h * MLA_HEAD_PAD:h * MLA_HEAD_PAD + LANES] = (qn[:, hs] * qscale).astype(BF16)
        rot = qr[:, hs] * cos2 + qs[:, hs] * sin2
        qm_ref[:, h * MLA_HEAD_PAD + LANES:(h + 1) * MLA_HEAD_PAD] = (rot * qscale).astype(BF16)

    ckvn = _rms_norm(_dot(xb, wckv_ref[...]), kvnorm_ref[...]).astype(BF16)
    kv = _dot(ckvn, wukv_ref[...])
    krope = (_dot(xb, wkr_ref[...]) * cos2 + _dot(xb, wkrs_ref[...]) * sin2).astype(BF16)
    for h in range(MLA_HEADS):
        km_ref[:, h * MLA_HEAD_PAD:h * MLA_HEAD_PAD + LANES] = kv[:, h * LANES:(h + 1) * LANES].astype(BF16)
        km_ref[:, h * MLA_HEAD_PAD + LANES:(h + 1) * MLA_HEAD_PAD] = krope
    vm_ref[...] = kv[:, MLA_HEADS * QK_NOPE:].astype(BF16)


def _pre_call(x2d, seq, cos2, sin2, w):
    n = x2d.shape[0]
    tm = PRE_TM
    per_seq = seq // tm
    weights = [w['wa'], w['a_ln_g'], w['a_ln_b'], w['ws'], w['bs'], w['wqkv'], w['wcq'], w['q_norm'],
               w['wqn'], w['wqr'], w['wqs'], w['wckv'], w['kv_norm'], w['wukv'], w['wkr'], w['wkrs']]
    tok = lambda width: pl.BlockSpec((tm, width), lambda i: (i, 0))
    pos = pl.BlockSpec((tm, LANES), lambda i: (i % per_seq, 0))
    out_widths = [A_WIDTH, NA_WIDTH, NA_WIDTH, NA_WIDTH, MLA_HEADS * MLA_HEAD_PAD,
                  MLA_HEADS * MLA_HEAD_PAD, MLA_WIDTH]
    return pl.pallas_call(
        _pre_kernel,
        grid=(n // tm,),
        in_specs=[tok(D_MODEL), pos, pos] + [_const_spec(t.shape) for t in weights],
        out_specs=[tok(wd) for wd in out_widths],
        out_shape=[jax.ShapeDtypeStruct((n, wd), BF16) for wd in out_widths],
        compiler_params=_params("parallel"),
        name="pre",
    )(x2d, cos2, sin2, *weights)


def _natten_kernel(q_ref, kp_ref, kc_ref, kn_ref, vp_ref, vc_ref, vn_ref, bias_ref, o_ref,
                   kcat, vcat, *, rows):
    j = pl.program_id(1)
    nt = NA_TOK_BLOCK
    kcat[0:nt] = kp_ref[...]
    kcat[nt:2 * nt] = kc_ref[...]
    kcat[2 * nt:3 * nt] = kn_ref[...]
    vcat[0:nt] = vp_ref[...]
    vcat[nt:2 * nt] = vc_ref[...]
    vcat[2 * nt:3 * nt] = vn_ref[...]
    r0 = j * NA_ROW_BLOCK
    lane_lo = lax.broadcasted_iota(jnp.int32, (GRID_W, LANES), 1) < NA_HEAD_DIM
    for i in range(NA_ROW_BLOCK):
        r = r0 + i
        start = jnp.clip(r - NA_WIN_H // 2, 0, rows - NA_WIN_H)
        pat = start - r + (NA_WIN_H - 1)
        koff = pl.multiple_of((start - r0 + NA_ROW_BLOCK) * GRID_W, GRID_W)
        qs = slice(i * GRID_W, (i + 1) * GRID_W)
        for hp in range(NA_HEADS // 2):
            cs = slice(hp * LANES, (hp + 1) * LANES)
            qp = q_ref[qs, cs]
            zero = jnp.zeros_like(qp)
            q2 = jnp.concatenate([jnp.where(lane_lo, qp, zero), jnp.where(lane_lo, zero, qp)], axis=0)
            kk = kcat[pl.ds(koff, NA_WIN_H * GRID_W), cs]
            vv = vcat[pl.ds(koff, NA_WIN_H * GRID_W), cs]
            s = _dot_nt(q2, kk) + bias_ref[pat, hp]
            m = jnp.max(s, -1, keepdims=True)
            e = jnp.exp(s - m)
            l = jnp.sum(e, -1, keepdims=True)
            o2 = _dot(e.astype(BF16), vv) / l
            o_ref[qs, cs] = jnp.where(lane_lo, o2[:GRID_W], o2[GRID_W:]).astype(BF16)


def _natten_call(q, k, v, bias, bsz, seq):
    rows = seq // GRID_W
    nrb = rows // NA_ROW_BLOCK
    nt = NA_TOK_BLOCK
    cur = pl.BlockSpec((nt, NA_WIDTH), lambda b, j: (b * nrb + j, 0))
    prev = pl.BlockSpec((nt, NA_WIDTH), lambda b, j: (b * nrb + jnp.maximum(j - 1, 0), 0))
    nxt = pl.BlockSpec((nt, NA_WIDTH), lambda b, j: (b * nrb + jnp.minimum(j + 1, nrb - 1), 0))
    return pl.pallas_call(
        functools.partial(_natten_kernel, rows=rows),
        grid=(bsz, nrb),
        in_specs=[cur, prev, cur, nxt, prev, cur, nxt, _const_spec(bias.shape)],
        out_specs=cur,
        out_shape=jax.ShapeDtypeStruct((bsz * seq, NA_WIDTH), BF16),
        scratch_shapes=[pltpu.VMEM((3 * nt, NA_WIDTH), BF16), pltpu.VMEM((3 * nt, NA_WIDTH), BF16)],
        compiler_params=_params("parallel", "parallel"),
        name="natten",
    )(q, k, k, k, v, v, v, bias)


def _natten_bias(rpb):
    c = jnp.arange(GRID_W)
    col_start = jnp.clip(c - NA_WIN_W // 2, 0, GRID_W - NA_WIN_W)
    in_win = (c[None, :] >= col_start[:, None]) & (c[None, :] < col_start[:, None] + NA_WIN_W)
    dc = jnp.clip(c[None, :] - c[:, None] + (NA_WIN_W - 1), 0, 2 * NA_WIN_W - 2)
    onehot = (dc[:, :, None] == jnp.arange(2 * NA_WIN_W - 1)).astype(F32)
    t = jnp.einsum('hrd,qkd->hrqk', rpb.astype(F32), onehot, precision=lax.Precision.HIGHEST)
    t = jnp.where(in_win[None, None], t, NEG_INF)
    b = jnp.stack([t[:, p:p + NA_WIN_H] for p in range(NA_WIN_H)])
    b = b.transpose(0, 1, 3, 2, 4)
    return b.reshape(NA_WIN_H, NA_HEADS // 2, 2 * GRID_W, NA_WIN_H * GRID_W)


def _mla_kernel(q_ref, k_ref, v_ref, o_ref, s0_ref, s1_ref, *, tk):
    q = q_ref[...]
    tq = q.shape[0]
    nk = k_ref.shape[0] // tk

    def scores(j, s_ref):
        s_ref[...] = _dot_nt(q, k_ref[pl.ds(pl.multiple_of(j * tk, tk), tk), :])

    def update(j, s_ref, carry):
        m, l, acc = carry
        s = s_ref[...]
        m_new = jnp.maximum(m, jnp.max(s, -1, keepdims=True))
        alpha = jnp.exp2(m - m_new)
        p = jnp.exp2(s - m_new)
        l = alpha * l + jnp.sum(p, -1, keepdims=True)
        v = v_ref[pl.ds(pl.multiple_of(j * tk, tk), tk), :]
        acc = alpha * acc + _dot(p.astype(BF16), v)
        return m_new, l, acc

    def pair(jj, carry):
        j = 2 * jj
        scores(j + 1, s1_ref)
        carry = update(j, s0_ref, carry)
        scores(j + 2, s0_ref)
        return update(j + 1, s1_ref, carry)

    carry = (jnp.full((tq, 1), NEG_INF, F32), jnp.zeros((tq, 1), F32), jnp.zeros((tq, V_DIM), F32))
    scores(0, s0_ref)
    carry = lax.fori_loop(0, nk // 2 - 1, pair, carry)
    scores(nk - 1, s1_ref)
    carry = update(nk - 2, s0_ref, carry)
    _, l, acc = update(nk - 1, s1_ref, carry)
    o_ref[...] = (acc / l).astype(BF16)


def _mla_call(qm, km, vm, bsz, seq):
    tq = MLA_TQ
    nq = seq // tq
    tk = min(MLA_TK, seq // 2)
    assert seq % (2 * tk) == 0
    return pl.pallas_call(
        functools.partial(_mla_kernel, tk=tk),
        grid=(bsz, MLA_HEADS, nq),
        in_specs=[pl.BlockSpec((tq, MLA_HEAD_PAD), lambda b, h, i: (b * nq + i, h)),
                  pl.BlockSpec((seq, MLA_HEAD_PAD), lambda b, h, i: (b, h)),
                  pl.BlockSpec((seq, V_DIM), lambda b, h, i: (b, h))],
        out_specs=pl.BlockSpec((tq, V_DIM), lambda b, h, i: (b * nq + i, h)),
        out_shape=jax.ShapeDtypeStruct((bsz * seq, MLA_WIDTH), BF16),
        scratch_shapes=[pltpu.VMEM((tq, tk), F32), pltpu.VMEM((tq, tk), F32)],
        compiler_params=_params("parallel", "parallel", "arbitrary"),
        name="mla",
    )(qm, km, vm)


def _first_argmax(vals):
    best, idx = vals[0], jnp.zeros(vals[0].shape, jnp.int32)
    for i in range(1, len(vals)):
        better = vals[i] > best
        idx = jnp.where(better, i, idx)
        best = jnp.where(better, vals[i], best)
    return best, idx


def _select(idx, vals):
    out = vals[0]
    for i in range(1, len(vals)):
        out = jnp.where(idx == i, vals[i], out)
    return out


def _merge_kernel(x_ref, oa_ref, ob_ref, oc_ref, wg_ref, wbr_ref, wo_ref, lng_ref, lnb_ref,
                  wrh_ref, wrl_ref, rb_ref, x1_ref, eid_ref, gate_ref):
    x = x_ref[...]
    xb = x.astype(BF16)
    merged = None
    for i, o_ref in enumerate((oa_ref, ob_ref, oc_ref)):
        g = _sigmoid(_dot(xb, wg_ref[:, i * D_MODEL:(i + 1) * D_MODEL]))
        term = g * _dot(o_ref[...], wbr_ref[i])
        merged = term if merged is None else merged + term
    m = _dot(merged.astype(BF16), wo_ref[...])
    x1 = _layer_norm(DEEPNORM_ALPHA * x + m, lng_ref[...], lnb_ref[...])
    x1_ref[...] = x1

    x1h = x1.astype(BF16)
    x1l = (x1 - x1h.astype(F32)).astype(BF16)
    logits = _dot_nt(wrh_ref[...], x1h) + (_dot_nt(wrl_ref[...], x1h) + _dot_nt(wrh_ref[...], x1l))
    scores = _sigmoid(logits)
    biased = scores + rb_ref[...]
    sc = [scores[e:e + 1, :] for e in range(N_EXPERTS)]
    bi = [biased[e:e + 1, :] for e in range(N_EXPERTS)]
    grp_scores = []
    for g in range(N_GROUPS):
        v = bi[g * EXPERTS_PER_GROUP:(g + 1) * EXPERTS_PER_GROUP]
        top2 = None
        for i in range(EXPERTS_PER_GROUP):
            for k in range(i + 1, EXPERTS_PER_GROUP):
                top2 = v[i] + v[k] if top2 is None else jnp.maximum(top2, v[i] + v[k])
        grp_scores.append(top2)
    _, grp = _first_argmax(grp_scores)
    cand = [_select(grp, [bi[g * EXPERTS_PER_GROUP + i] for g in range(N_GROUPS)])
            for i in range(EXPERTS_PER_GROUP)]
    cand_sc = [_select(grp, [sc[g * EXPERTS_PER_GROUP + i] for g in range(N_GROUPS)])
               for i in range(EXPERTS_PER_GROUP)]
    _, i1 = _first_argmax(cand)
    _, i2 = _first_argmax([jnp.where(i1 == i, -jnp.inf, cand[i]) for i in range(EXPERTS_PER_GROUP)])
    s1 = _select(i1, cand_sc)
    s2 = _select(i2, cand_sc)
    tot = s1 + s2
    eid_ref[0:1, :] = grp * EXPERTS_PER_GROUP + i1
    eid_ref[1:2, :] = grp * EXPERTS_PER_GROUP + i2
    gate_ref[0:1, :] = s1 / tot
    gate_ref[1:2, :] = s2 / tot


def _merge_call(x2d, oa, ob, oc, w, shared):
    n = x2d.shape[0]
    tm = MERGE_TM
    weights = [w['wg'], w['wbr'], w['wo'], w['ln1_g'], w['ln1_b'], shared['wrh'], shared['wrl'], shared['rb']]
    tok = lambda width: pl.BlockSpec((tm, width), lambda i: (i, 0))
    route = pl.BlockSpec((TOP_K, tm), lambda i: (0, i))
    return pl.pallas_call(
        _merge_kernel,
        grid=(n // tm,),
        in_specs=[tok(D_MODEL), tok(A_WIDTH), tok(NA_WIDTH), tok(MLA_WIDTH)]
                 + [_const_spec(t.shape) for t in weights],
        out_specs=[tok(D_MODEL), route, route],
        out_shape=[jax.ShapeDtypeStruct((n, D_MODEL), F32),
                   jax.ShapeDtypeStruct((TOP_K, n), jnp.int32),
                   jax.ShapeDtypeStruct((TOP_K, n), F32)],
        compiler_params=_params("parallel"),
        name="merge",
    )(x2d, oa, ob, oc, *weights)


def _row_copy(src, src_row, dst, dst_row, sem):
    return pltpu.make_async_copy(src.at[pl.ds(src_row, 1)], dst.at[pl.ds(dst_row, 1)], sem)


def _dispatch_kernel(pstart_ref, pend_ref, nused_ref, dest_ref, x_hbm, xs_hbm, zbuf, zsem, sem):
    i = pl.program_id(0)
    tm = dest_ref.shape[2]
    n_blocks = xs_hbm.shape[0] // MOE_BLOCK

    @pl.when(i == 0)
    def _():
        zbuf[...] = jnp.zeros_like(zbuf)

        def zero_block(first_slot):
            dst = xs_hbm.at[pl.ds(pl.multiple_of(first_slot, MOE_BLOCK), MOE_BLOCK)]
            return pltpu.make_async_copy(zbuf, dst, zsem)

        def for_each_zero_block(fn):
            for e in range(N_EXPERTS):
                @pl.when(pend_ref[e] > pstart_ref[e])
                def _():
                    fn(zero_block(pend_ref[e] - MOE_BLOCK))

                @pl.when(nused_ref[0] + e < n_blocks)
                def _():
                    fn(zero_block((nused_ref[0] + e) * MOE_BLOCK))

        for_each_zero_block(lambda cp: cp.start())
        for_each_zero_block(lambda cp: cp.wait())

    def issue(j, c):
        for k in range(TOP_K):
            _row_copy(x_hbm, i * tm + j, xs_hbm, dest_ref[0, k, j], sem).start()
        return c

    def wait(j, c):
        for k in range(TOP_K):
            _row_copy(x_hbm, 0, xs_hbm, 0, sem).wait()
        return c

    lax.fori_loop(0, tm, issue, 0, unroll=8)
    lax.fori_loop(0, tm, wait, 0, unroll=8)


def _dispatch_call(x1, dest3, pad_start, pad_end, n_used, n_slots):
    n = x1.shape[0]
    tm = dest3.shape[2]
    grid_spec = pltpu.PrefetchScalarGridSpec(
        num_scalar_prefetch=3,
        grid=(n // tm,),
        in_specs=[pl.BlockSpec((1, TOP_K, tm), lambda i, ps, pe, nu: (i, 0, 0), memory_space=pltpu.SMEM),
                  pl.BlockSpec(memory_space=pl.ANY)],
        out_specs=pl.BlockSpec(memory_space=pl.ANY),
        scratch_shapes=[pltpu.VMEM((MOE_BLOCK, D_MODEL), F32), pltpu.SemaphoreType.DMA(()),
                        pltpu.SemaphoreType.DMA(())],
    )
    return pl.pallas_call(
        _dispatch_kernel,
        grid_spec=grid_spec,
        out_shape=jax.ShapeDtypeStruct((n_slots, D_MODEL), F32),
        compiler_params=_params("arbitrary"),
        name="dispatch",
    )(pad_start, pad_end, n_used, dest3, x1)


def _experts_kernel(be_ref, nused_ref, x_ref, wgu_ref, wd_ref, y_ref):
    b = pl.program_id(0)

    @pl.when(b < nused_ref[0])
    def _():
        gu = _dot(x_ref[...].astype(BF16), wgu_ref[0])
        gt = gu[:, :D_EXPERT]
        hid = gt * _sigmoid(gt) * gu[:, D_EXPERT:]
        y_ref[...] = _dot(hid.astype(BF16), wd_ref[0])

    @pl.when(b >= nused_ref[0])
    def _():
        y_ref[...] = jnp.zeros_like(y_ref)


def _experts_call(xs, block_e, n_used, w):
    n_blocks = block_e.shape[0]
    grid_spec = pltpu.PrefetchScalarGridSpec(
        num_scalar_prefetch=2,
        grid=(n_blocks,),
        in_specs=[pl.BlockSpec((MOE_BLOCK, D_MODEL), lambda b, be, nu: (jnp.minimum(b, nu[0] - 1), 0)),
                  pl.BlockSpec((1, D_MODEL, 2 * D_EXPERT), lambda b, be, nu: (be[b], 0, 0)),
                  pl.BlockSpec((1, D_EXPERT, D_MODEL), lambda b, be, nu: (be[b], 0, 0))],
        out_specs=pl.BlockSpec((MOE_BLOCK, D_MODEL), lambda b, be, nu: (b, 0)),
    )
    return pl.pallas_call(
        _experts_kernel,
        grid_spec=grid_spec,
        out_shape=jax.ShapeDtypeStruct((n_blocks * MOE_BLOCK, D_MODEL), F32),
        compiler_params=_params("arbitrary"),
        name="experts",
    )(block_e, n_used, xs, w['wgu'], w['wd'])


def _lane_column(row, width):
    tm = row.shape[1]
    col = jnp.broadcast_to(row, (LANES, tm)).T
    return jnp.concatenate([col] * (width // LANES), axis=1)


def _combine_kernel(dest_ref, gate_ref, x1_ref, y_hbm, lng_ref, lnb_ref, o_ref, buf, sem):
    tm = x1_ref.shape[0]

    def issue(j, c):
        for k in range(TOP_K):
            _row_copy(y_hbm, dest_ref[0, k, j], buf.at[k], j, sem).start()
        return c

    def wait(j, c):
        for k in range(TOP_K):
            _row_copy(y_hbm, 0, buf.at[k], j, sem).wait()
        return c

    lax.fori_loop(0, tm, issue, 0, unroll=8)
    lax.fori_loop(0, tm, wait, 0, unroll=8)
    f = (_lane_column(gate_ref[0:1, :], D_MODEL) * buf[0]
         + _lane_column(gate_ref[1:2, :], D_MODEL) * buf[1])
    o_ref[...] = _layer_norm(DEEPNORM_ALPHA * x1_ref[...] + f, lng_ref[...], lnb_ref[...])


def _combine_call(x1, yb, dest3, gate, w):
    n = x1.shape[0]
    tm = dest3.shape[2]
    return pl.pallas_call(
        _combine_kernel,
        grid=(n // tm,),
        in_specs=[pl.BlockSpec((1, TOP_K, tm), lambda i: (i, 0, 0), memory_space=pltpu.SMEM),
                  pl.BlockSpec((TOP_K, tm), lambda i: (0, i)),
                  pl.BlockSpec((tm, D_MODEL), lambda i: (i, 0)),
                  pl.BlockSpec(memory_space=pl.ANY),
                  _const_spec(w['ln2_g'].shape), _const_spec(w['ln2_b'].shape)],
        out_specs=pl.BlockSpec((tm, D_MODEL), lambda i: (i, 0)),
        out_shape=jax.ShapeDtypeStruct((n, D_MODEL), F32),
        scratch_shapes=[pltpu.VMEM((TOP_K, tm, D_MODEL), F32), pltpu.SemaphoreType.DMA(())],
        compiler_params=_params("arbitrary"),
        name="combine",
    )(dest3, gate, x1, yb, w['ln2_g'], w['ln2_b'])


def _dispatch_plan(eid):
    n = eid.shape[1]
    n_blocks = -(-(n * TOP_K) // MOE_BLOCK) + N_EXPERTS
    experts = jnp.arange(N_EXPERTS, dtype=jnp.int32)
    hit = [eid[k][:, None] == experts for k in range(TOP_K)]
    onehot = (hit[0] | hit[1]).astype(jnp.int32)
    incl = jnp.cumsum(onehot, axis=0)
    counts = incl[-1]
    padded = (counts + MOE_BLOCK - 1) // MOE_BLOCK * MOE_BLOCK
    pad_end = jnp.cumsum(padded).astype(jnp.int32)
    pad_start = pad_end - padded
    slot = pad_start[None, :] + (incl - onehot)
    dest = jnp.stack([jnp.sum(jnp.where(h, slot, 0), axis=1) for h in hit]).astype(jnp.int32)
    tm = COMBINE_TM
    dest3 = dest.reshape(TOP_K, n // tm, tm).transpose(1, 0, 2)
    first_slot = jnp.arange(n_blocks, dtype=jnp.int32) * MOE_BLOCK
    block_e = jnp.minimum(jnp.sum(pad_end[None, :] <= first_slot[:, None], axis=1),
                          N_EXPERTS - 1).astype(jnp.int32)
    n_used = pad_end[-1:] // MOE_BLOCK
    return dest3, pad_start, pad_end, block_e, n_used, n_blocks * MOE_BLOCK


def _rope_tables(seq):
    inv = 1.0 / (ROPE_BASE ** (jnp.arange(0, QK_ROPE, 2, dtype=F32) / QK_ROPE))
    ang = jnp.arange(seq, dtype=F32)[:, None] * inv[None, :]
    pad = jnp.zeros((seq, LANES - QK_ROPE), F32)
    cos, sin = jnp.cos(ang), jnp.sin(ang)
    return jnp.concatenate([cos, cos, pad], -1), jnp.concatenate([sin, sin, pad], -1)


def _pad_cols(t, width):
    return jnp.pad(t, ((0, 0), (0, width - t.shape[1])))


def _rope_weight_pair(w_rope):
    half = QK_ROPE // 2
    swapped = jnp.concatenate([-w_rope[:, half:], w_rope[:, :half]], axis=1)
    return _pad_cols(w_rope, LANES), _pad_cols(swapped, LANES)


def _layer_weights(l, w_in, a_ln_g, a_ln_b, a_ws, a_bs, na_rpb, mla_q_norm, mla_kv_norm, mla_w_uq,
                   mla_w_ukv, w_br_a, w_br_b, w_br_c, w_o, ln1_g, ln1_b, ln2_g, ln2_b, w_gate, w_up,
                   w_down):
    sizes = (2 * A_WIDTH, NA_WIDTH, NA_WIDTH, NA_WIDTH, Q_LORA, KV_LORA, QK_ROPE, N_BRANCH * D_MODEL)
    cuts = np.cumsum(sizes)[:-1].tolist()
    wa, wq, wk, wv, wcq, wckv, wkr, wg = jnp.split(w_in[l], cuts, axis=-1)
    row = lambda t: t.reshape(1, -1).astype(F32)
    uq = mla_w_uq[l].reshape(Q_LORA, MLA_HEADS, QK_NOPE + QK_ROPE)
    wqn = uq[:, :, :QK_NOPE].reshape(Q_LORA, MLA_HEADS * QK_NOPE)
    rope_pairs = [_rope_weight_pair(uq[:, h, QK_NOPE:]) for h in range(MLA_HEADS)]
    wqr = jnp.concatenate([p[0] for p in rope_pairs], axis=1)
    wqs = jnp.concatenate([p[1] for p in rope_pairs], axis=1)
    ukv = mla_w_ukv[l].reshape(KV_LORA, MLA_HEADS, QK_NOPE + V_DIM)
    wukv = jnp.concatenate([ukv[:, :, :QK_NOPE].reshape(KV_LORA, -1), ukv[:, :, QK_NOPE:].reshape(KV_LORA, -1)], 1)
    wkr_p, wkrs_p = _rope_weight_pair(wkr)
    return {
        'wa': wa.astype(BF16), 'a_ln_g': row(a_ln_g[l]), 'a_ln_b': row(a_ln_b[l]),
        'ws': a_ws[l].astype(BF16),
        'bs': jnp.broadcast_to(a_bs[l][:, :, None], (A_GROUPS, CHUNK, CHUNK)).astype(F32),
        'wqkv': jnp.concatenate([wq * NA_SCALE, wk, wv], axis=1).astype(BF16),
        'wcq': wcq.astype(BF16), 'q_norm': row(mla_q_norm[l]),
        'wqn': wqn.astype(BF16), 'wqr': wqr.astype(BF16), 'wqs': wqs.astype(BF16),
        'wckv': wckv.astype(BF16), 'kv_norm': row(mla_kv_norm[l]), 'wukv': wukv.astype(BF16),
        'wkr': wkr_p.astype(BF16), 'wkrs': wkrs_p.astype(BF16),
        'na_bias': _natten_bias(na_rpb[l]),
        'wg': wg.astype(BF16),
        'wbr': jnp.stack([w_br_a[l], w_br_b[l], w_br_c[l]]).astype(BF16),
        'wo': w_o[l].astype(BF16), 'ln1_g': row(ln1_g[l]), 'ln1_b': row(ln1_b[l]),
        'ln2_g': row(ln2_g[l]), 'ln2_b': row(ln2_b[l]),
        'wgu': jnp.concatenate([w_gate[l], w_up[l]], axis=-1).astype(BF16),
        'wd': w_down[l].astype(BF16),
    }


def _shared_weights(w_router, router_bias):
    wr_t = w_router.T.astype(F32)
    wrh = wr_t.astype(BF16)
    wrl = (wr_t - wrh.astype(F32)).astype(BF16)
    rb = jnp.broadcast_to(router_bias.astype(F32)[:, None], (N_EXPERTS, MERGE_TM))
    return {'wrh': wrh, 'wrl': wrl, 'rb': rb}


def _trunk(x, layers, shared):
    bsz, seq, _ = x.shape
    cos2, sin2 = _rope_tables(seq)
    x2d = x.reshape(bsz * seq, D_MODEL)
    for w in layers:
        oa, q, k, v, qm, km, vm = _pre_call(x2d, seq, cos2, sin2, w)
        ob = _natten_call(q, k, v, w['na_bias'], bsz, seq)
        oc = _mla_call(qm, km, vm, bsz, seq)
        x1, eid, gate = _merge_call(x2d, oa, ob, oc, w, shared)
        dest3, pad_start, pad_end, block_e, n_used, n_slots = _dispatch_plan(eid)
        xs = _dispatch_call(x1, dest3, pad_start, pad_end, n_used, n_slots)
        yb = _experts_call(xs, block_e, n_used, w)
        x2d = _combine_call(x1, yb, dest3, gate, w)
    return x2d.reshape(bsz, seq, D_MODEL)


def kernel(x_prompt, x_sample, w_in, a_ln_g, a_ln_b, a_ws, a_bs, na_rpb, mla_q_norm, mla_kv_norm, mla_w_uq, mla_w_ukv, w_br_a, w_br_b, w_br_c, w_o, ln1_g, ln1_b, ln2_g, ln2_b, w_router, router_bias, w_gate, w_up, w_down):
    layers = [_layer_weights(l, w_in, a_ln_g, a_ln_b, a_ws, a_bs, na_rpb, mla_q_norm, mla_kv_norm,
                             mla_w_uq, mla_w_ukv, w_br_a, w_br_b, w_br_c, w_o, ln1_g, ln1_b, ln2_g,
                             ln2_b, w_gate, w_up, w_down) for l in range(DEPTH)]
    shared = _shared_weights(w_router, router_bias)
    return (_trunk(x_prompt, layers, shared), _trunk(x_sample, layers, shared))
```

```python
import functools
import math

import jax
import jax.numpy as jnp
import numpy as np
from jax import lax
from jax.experimental import pallas as pl
from jax.experimental.pallas import tpu as pltpu

F32 = jnp.float32
BF16 = jnp.bfloat16

D_MODEL = 1024
DEPTH = 2
GRID_W = 64
CHUNK = 128
A_WIDTH = 512
A_GROUPS = 4
NA_HEADS = 8
NA_HEAD_DIM = 64
NA_WIN_H = 8
NA_WIN_W = 16
NA_WIDTH = NA_HEADS * NA_HEAD_DIM
NA_SCALE = NA_HEAD_DIM ** -0.5
MLA_HEADS = 4
Q_LORA = 384
KV_LORA = 256
QK_NOPE = 128
QK_ROPE = 64
V_DIM = 128
MLA_WIDTH = MLA_HEADS * V_DIM
MLA_SCALE = (QK_NOPE + QK_ROPE) ** -0.5
ROPE_BASE = 10000.0
N_BRANCH = 3
N_EXPERTS = 16
N_GROUPS = 4
EXPERTS_PER_GROUP = N_EXPERTS // N_GROUPS
TOP_K = 2
D_EXPERT = 256
MOE_BLOCK = 256
DEEPNORM_ALPHA = (2 * DEPTH) ** 0.25
LN_EPS = 1e-5
RMS_EPS = 1e-6
NEG_INF = -1e30
LOG2E = math.log2(math.e)

VMEM_LIMIT_BYTES = 56 * 1024 * 1024
LANES = 128
MLA_HEAD_PAD = 2 * LANES
NA_ROW_BLOCK = 8
NA_TOK_BLOCK = NA_ROW_BLOCK * GRID_W

PRE_TM = 512
MERGE_TM = 512
COMBINE_TM = 256
MLA_TQ = 1024
MLA_TK = 1024
MLA_MAX_UNROLL = 5
MLA_SHIFT_LANE = QK_NOPE + QK_ROPE
MLA_L_MIN = 2.0 ** -60
MLA_L_MAX = 2.0 ** 100
MLA_BOUND_SLACK = 1.0 + 2.0 ** -7


def _params(*sem):
    return pltpu.CompilerParams(dimension_semantics=sem, vmem_limit_bytes=VMEM_LIMIT_BYTES)


def _const_spec(shape):
    nd = len(shape)
    return pl.BlockSpec(shape, lambda *_: (0,) * nd)


def _layer_norm(y, g, b):
    mu = jnp.mean(y, -1, keepdims=True)
    yc = y - mu
    var = jnp.mean(yc * yc, -1, keepdims=True)
    return yc * lax.rsqrt(var + LN_EPS) * g + b


def _rms_norm(y, g):
    return y * lax.rsqrt(jnp.mean(y * y, -1, keepdims=True) + RMS_EPS) * g


def _gelu_tanh(x):
    return 0.5 * x * (1.0 + jnp.tanh(math.sqrt(2.0 / math.pi) * (x + 0.044715 * (x * x * x))))


def _sigmoid(x):
    return 1.0 / (1.0 + jnp.exp(-x))


def _dot(a, b):
    return jnp.dot(a, b, preferred_element_type=F32)


def _dot_nt(a, b):
    return lax.dot_general(a, b, (((1,), (1,)), ((), ())), preferred_element_type=F32)


def _pre_kernel(x_ref, cos_ref, sin_ref, wa_ref, lng_ref, lnb_ref, ws_ref, bs_ref, wqkv_ref,
                wcq_ref, qnorm_ref, wqn_ref, wqr_ref, wqs_ref, wckv_ref, kvnorm_ref, wukv_ref,
                wkr_ref, wkrs_ref,
                oa_ref, q_ref, k_ref, v_ref, qm_ref, km_ref, vm_ref):
    tm = x_ref.shape[0]
    xb = x_ref[...].astype(BF16)
    cos2 = cos_ref[...]
    sin2 = sin_ref[...]

    a = _gelu_tanh(_dot(xb, wa_ref[...]))
    u = a[:, :A_WIDTH]
    vn = _layer_norm(a[:, A_WIDTH:], lng_ref[...], lnb_ref[...]).astype(BF16)
    for c in range(tm // CHUNK):
        rs = slice(c * CHUNK, (c + 1) * CHUNK)
        for g in range(A_GROUPS):
            cs = slice(g * CHUNK, (g + 1) * CHUNK)
            sp = _dot(ws_ref[g], vn[rs, cs]) + bs_ref[g]
            oa_ref[rs, cs] = (u[rs, cs] * sp).astype(BF16)

    qkv = _dot(xb, wqkv_ref[...])
    q_ref[...] = qkv[:, :NA_WIDTH].astype(BF16)
    k_ref[...] = qkv[:, NA_WIDTH:2 * NA_WIDTH].astype(BF16)
    v_ref[...] = qkv[:, 2 * NA_WIDTH:].astype(BF16)

    cqn = _rms_norm(_dot(xb, wcq_ref[...]), qnorm_ref[...]).astype(BF16)
    qn = _dot(cqn, wqn_ref[...])
    qr = _dot(cqn, wqr_ref[...])
    qs = _dot(cqn, wqs_ref[...])
    qscale = MLA_SCALE * LOG2E
    shift_lane = lax.broadcasted_iota(jnp.int32, (tm, LANES), 1) == MLA_SHIFT_LANE - QK_NOPE
    for h in range(MLA_HEADS):
        hs = slice(h * LANES, (h + 1) * LANES)
        nope = (qn[:, hs] * qscale).astype(BF16)
        rot = ((qr[:, hs] * cos2 + qs[:, hs] * sin2) * qscale).astype(BF16)
        nf, rf = nope.astype(F32), rot.astype(F32)
        norm = jnp.sqrt(jnp.sum(nf * nf, -1, keepdims=True) + jnp.sum(rf * rf, -1, keepdims=True))
        norm = jnp.broadcast_to(norm * MLA_BOUND_SLACK, rot.shape).astype(BF16)
        qm_ref[:, h * MLA_HEAD_PAD:h * MLA_HEAD_PAD + LANES] = nope
        qm_ref[:, h * MLA_HEAD_PAD + LANES:(h + 1) * MLA_HEAD_PAD] = jnp.where(shift_lane, norm, rot)

    ckvn = _rms_norm(_dot(xb, wckv_ref[...]), kvnorm_ref[...]).astype(BF16)
    kv = _dot(ckvn, wukv_ref[...])
    krope = _dot(xb, wkr_ref[...]) * cos2 + _dot(xb, wkrs_ref[...]) * sin2
    lane = lax.broadcasted_iota(jnp.int32, krope.shape, 1)
    krope = jnp.where(lane == MLA_SHIFT_LANE - QK_NOPE, -1.0, krope).astype(BF16)
    for h in range(MLA_HEADS):
        km_ref[:, h * MLA_HEAD_PAD:h * MLA_HEAD_PAD + LANES] = kv[:, h * LANES:(h + 1) * LANES].astype(BF16)
        km_ref[:, h * MLA_HEAD_PAD + LANES:(h + 1) * MLA_HEAD_PAD] = krope
    vm_ref[...] = kv[:, MLA_HEADS * QK_NOPE:].astype(BF16)


def _pre_call(x2d, seq, cos2, sin2, w):
    n = x2d.shape[0]
    tm = PRE_TM
    per_seq = seq // tm
    weights = [w['wa'], w['a_ln_g'], w['a_ln_b'], w['ws'], w['bs'], w['wqkv'], w['wcq'], w['q_norm'],
               w['wqn'], w['wqr'], w['wqs'], w['wckv'], w['kv_norm'], w['wukv'], w['wkr'], w['wkrs']]
    tok = lambda width: pl.BlockSpec((tm, width), lambda i: (i, 0))
    pos = pl.BlockSpec((tm, LANES), lambda i: (i % per_seq, 0))
    out_widths = [A_WIDTH, NA_WIDTH, NA_WIDTH, NA_WIDTH, MLA_HEADS * MLA_HEAD_PAD,
                  MLA_HEADS * MLA_HEAD_PAD, MLA_WIDTH]
    return pl.pallas_call(
        _pre_kernel,
        grid=(n // tm,),
        in_specs=[tok(D_MODEL), pos, pos] + [_const_spec(t.shape) for t in weights],
        out_specs=[tok(wd) for wd in out_widths],
        out_shape=[jax.ShapeDtypeStruct((n, wd), BF16) for wd in out_widths],
        compiler_params=_params("parallel"),
        name="pre",
    )(x2d, cos2, sin2, *weights)


def _natten_kernel(q_ref, kp_ref, kc_ref, kn_ref, vp_ref, vc_ref, vn_ref, bias_ref, o_ref,
                   kcat, vcat, *, rows):
    j = pl.program_id(1)
    nt = NA_TOK_BLOCK
    kcat[0:nt] = kp_ref[...]
    kcat[nt:2 * nt] = kc_ref[...]
    kcat[2 * nt:3 * nt] = kn_ref[...]
    vcat[0:nt] = vp_ref[...]
    vcat[nt:2 * nt] = vc_ref[...]
    vcat[2 * nt:3 * nt] = vn_ref[...]
    r0 = j * NA_ROW_BLOCK
    lane_lo = lax.broadcasted_iota(jnp.int32, (GRID_W, LANES), 1) < NA_HEAD_DIM
    for i in range(NA_ROW_BLOCK):
        r = r0 + i
        start = jnp.clip(r - NA_WIN_H // 2, 0, rows - NA_WIN_H)
        pat = start - r + (NA_WIN_H - 1)
        koff = pl.multiple_of((start - r0 + NA_ROW_BLOCK) * GRID_W, GRID_W)
        qs = slice(i * GRID_W, (i + 1) * GRID_W)
        for hp in range(NA_HEADS // 2):
            cs = slice(hp * LANES, (hp + 1) * LANES)
            qp = q_ref[qs, cs]
            zero = jnp.zeros_like(qp)
            q2 = jnp.concatenate([jnp.where(lane_lo, qp, zero), jnp.where(lane_lo, zero, qp)], axis=0)
            kk = kcat[pl.ds(koff, NA_WIN_H * GRID_W), cs]
            vv = vcat[pl.ds(koff, NA_WIN_H * GRID_W), cs]
            s = _dot_nt(q2, kk) + bias_ref[pat, hp]
            m = jnp.max(s, -1, keepdims=True)
            e = jnp.exp(s - m)
            l = jnp.sum(e, -1, keepdims=True)
            o2 = _dot(e.astype(BF16), vv) / l
            o_ref[qs, cs] = jnp.where(lane_lo, o2[:GRID_W], o2[GRID_W:]).astype(BF16)


def _natten_call(q, k, v, bias, bsz, seq):
    rows = seq // GRID_W
    nrb = rows // NA_ROW_BLOCK
    nt = NA_TOK_BLOCK
    cur = pl.BlockSpec((nt, NA_WIDTH), lambda b, j: (b * nrb + j, 0))
    prev = pl.BlockSpec((nt, NA_WIDTH), lambda b, j: (b * nrb + jnp.maximum(j - 1, 0), 0))
    nxt = pl.BlockSpec((nt, NA_WIDTH), lambda b, j: (b * nrb + jnp.minimum(j + 1, nrb - 1), 0))
    return pl.pallas_call(
        functools.partial(_natten_kernel, rows=rows),
        grid=(bsz, nrb),
        in_specs=[cur, prev, cur, nxt, prev, cur, nxt, _const_spec(bias.shape)],
        out_specs=cur,
        out_shape=jax.ShapeDtypeStruct((bsz * seq, NA_WIDTH), BF16),
        scratch_shapes=[pltpu.VMEM((3 * nt, NA_WIDTH), BF16), pltpu.VMEM((3 * nt, NA_WIDTH), BF16)],
        compiler_params=_params("parallel", "parallel"),
        name="natten",
    )(q, k, k, k, v, v, v, bias)


def _natten_bias(rpb):
    c = jnp.arange(GRID_W)
    col_start = jnp.clip(c - NA_WIN_W // 2, 0, GRID_W - NA_WIN_W)
    in_win = (c[None, :] >= col_start[:, None]) & (c[None, :] < col_start[:, None] + NA_WIN_W)
    dc = jnp.clip(c[None, :] - c[:, None] + (NA_WIN_W - 1), 0, 2 * NA_WIN_W - 2)
    onehot = (dc[:, :, None] == jnp.arange(2 * NA_WIN_W - 1)).astype(F32)
    t = jnp.einsum('hrd,qkd->hrqk', rpb.astype(F32), onehot, precision=lax.Precision.HIGHEST)
    t = jnp.where(in_win[None, None], t, NEG_INF)
    b = jnp.stack([t[:, p:p + NA_WIN_H] for p in range(NA_WIN_H)])
    b = b.transpose(0, 1, 3, 2, 4)
    return b.reshape(NA_WIN_H, NA_HEADS // 2, 2 * GRID_W, NA_WIN_H * GRID_W)


def _mla_online_softmax(q, k_ref, v_ref, tk):
    tq = q.shape[0]

    def body(j, carry):
        m, l, acc = carry
        ks = pl.ds(pl.multiple_of(j * tk, tk), tk)
        s = _dot_nt(q, k_ref[ks, :])
        m_new = jnp.maximum(m, jnp.max(s, -1, keepdims=True))
        alpha = jnp.exp2(m - m_new)
        p = jnp.exp2(s - m_new)
        l = alpha * l + jnp.sum(p, -1, keepdims=True)
        acc = alpha * acc + _dot(p.astype(BF16), v_ref[ks, :])
        return m_new, l, acc

    init = (jnp.full((tq, 1), NEG_INF, F32), jnp.zeros((tq, 1), F32), jnp.zeros((tq, V_DIM), F32))
    _, l, acc = lax.fori_loop(0, k_ref.shape[0] // tk, body, init)
    return acc, l


def _mla_kernel(q_ref, k_ref, v_ref, o_ref, kmax_ref, *, tk, unroll):
    q = q_ref[...]
    tq = q.shape[0]
    seq = k_ref.shape[0]

    @pl.when(pl.program_id(2) == 0)
    def _():
        def body(c, mx):
            kc = k_ref[pl.ds(pl.multiple_of(c * tk, tk), tk), :].astype(F32)
            n2 = jnp.sum(kc * kc, -1, keepdims=True)
            return jnp.maximum(mx, jnp.max(n2, 0, keepdims=True))

        mx = lax.fori_loop(0, seq // tk, body, jnp.zeros((1, 1), F32))
        kmax = jnp.broadcast_to(jnp.sqrt(mx) * MLA_BOUND_SLACK, kmax_ref.shape)
        lane = lax.broadcasted_iota(jnp.int32, kmax_ref.shape, 1)
        kmax_ref[...] = jnp.where(lane == MLA_SHIFT_LANE, kmax, 1.0).astype(BF16)

    q_shift = q * kmax_ref[0:1, :]
    ones = jnp.ones((tk, LANES), BF16)

    def body(j, acc):
        ks = pl.ds(pl.multiple_of(j * tk, tk), tk)
        p = jnp.exp2(_dot_nt(q_shift, k_ref[ks, :])).astype(BF16)
        return acc + _dot(p, jnp.concatenate([v_ref[ks, :], ones], axis=1))

    nk = seq // tk
    acc = lax.fori_loop(0, nk - 1, body, jnp.zeros((tq, V_DIM + LANES), F32), unroll=unroll)
    l_part = acc[:, V_DIM:]
    usable = (l_part >= MLA_L_MIN) & (l_part <= MLA_L_MAX)
    n_bad = jnp.sum(jnp.where(usable, 0.0, 1.0))
    acc = body(nk - 1, acc)
    l = acc[:, V_DIM:]
    o_ref[...] = (acc[:, :V_DIM] / l).astype(BF16)

    @pl.when(n_bad > 0.0)
    def _():
        acc, l = _mla_online_softmax(q, k_ref, v_ref, tk)
        o_ref[...] = (acc / l).astype(BF16)


def _mla_call(qm, km, vm, bsz, seq):
    tq = min(MLA_TQ, seq)
    nq = seq // tq
    tk = min(MLA_TK, seq // 2)
    looped = seq // tk - 1
    unroll = max(u for u in range(1, MLA_MAX_UNROLL + 1) if looped % u == 0)
    return pl.pallas_call(
        functools.partial(_mla_kernel, tk=tk, unroll=unroll),
        grid=(bsz, MLA_HEADS, nq),
        in_specs=[pl.BlockSpec((tq, MLA_HEAD_PAD), lambda b, h, i: (b * nq + i, h)),
                  pl.BlockSpec((seq, MLA_HEAD_PAD), lambda b, h, i: (b, h)),
                  pl.BlockSpec((seq, V_DIM), lambda b, h, i: (b, h))],
        out_specs=pl.BlockSpec((tq, V_DIM), lambda b, h, i: (b * nq + i, h)),
        out_shape=jax.ShapeDtypeStruct((bsz * seq, MLA_WIDTH), BF16),
        scratch_shapes=[pltpu.VMEM((16, MLA_HEAD_PAD), BF16)],
        compiler_params=_params("parallel", "parallel", "arbitrary"),
        name="mla",
    )(qm, km, vm)


def _first_argmax(vals):
    best, idx = vals[0], jnp.zeros(vals[0].shape, jnp.int32)
    for i in range(1, len(vals)):
        better = vals[i] > best
        idx = jnp.where(better, i, idx)
        best = jnp.where(better, vals[i], best)
    return best, idx


def _select(idx, vals):
    out = vals[0]
    for i in range(1, len(vals)):
        out = jnp.where(idx == i, vals[i], out)
    return out


def _merge_kernel(x_ref, oa_ref, ob_ref, oc_ref, wg_ref, wbr_ref, wo_ref, lng_ref, lnb_ref,
                  wrh_ref, wrl_ref, rb_ref, x1_ref, eid_ref, gate_ref):
    x = x_ref[...]
    xb = x.astype(BF16)
    merged = None
    for i, o_ref in enumerate((oa_ref, ob_ref, oc_ref)):
        g = _sigmoid(_dot(xb, wg_ref[:, i * D_MODEL:(i + 1) * D_MODEL]))
        term = g * _dot(o_ref[...], wbr_ref[i])
        merged = term if merged is None else merged + term
    m = _dot(merged.astype(BF16), wo_ref[...])
    x1 = _layer_norm(DEEPNORM_ALPHA * x + m, lng_ref[...], lnb_ref[...])
    x1_ref[...] = x1

    x1h = x1.astype(BF16)
    x1l = (x1 - x1h.astype(F32)).astype(BF16)
    logits = _dot_nt(wrh_ref[...], x1h) + (_dot_nt(wrl_ref[...], x1h) + _dot_nt(wrh_ref[...], x1l))
    scores = _sigmoid(logits)
    biased = scores + rb_ref[...]
    sc = [scores[e:e + 1, :] for e in range(N_EXPERTS)]
    bi = [biased[e:e + 1, :] for e in range(N_EXPERTS)]
    grp_scores = []
    for g in range(N_GROUPS):
        v = bi[g * EXPERTS_PER_GROUP:(g + 1) * EXPERTS_PER_GROUP]
        top2 = None
        for i in range(EXPERTS_PER_GROUP):
            for k in range(i + 1, EXPERTS_PER_GROUP):
                top2 = v[i] + v[k] if top2 is None else jnp.maximum(top2, v[i] + v[k])
        grp_scores.append(top2)
    _, grp = _first_argmax(grp_scores)
    cand = [_select(grp, [bi[g * EXPERTS_PER_GROUP + i] for g in range(N_GROUPS)])
            for i in range(EXPERTS_PER_GROUP)]
    cand_sc = [_select(grp, [sc[g * EXPERTS_PER_GROUP + i] for g in range(N_GROUPS)])
               for i in range(EXPERTS_PER_GROUP)]
    _, i1 = _first_argmax(cand)
    _, i2 = _first_argmax([jnp.where(i1 == i, -jnp.inf, cand[i]) for i in range(EXPERTS_PER_GROUP)])
    s1 = _select(i1, cand_sc)
    s2 = _select(i2, cand_sc)
    tot = s1 + s2
    eid_ref[0:1, :] = grp * EXPERTS_PER_GROUP + i1
    eid_ref[1:2, :] = grp * EXPERTS_PER_GROUP + i2
    gate_ref[0:1, :] = s1 / tot
    gate_ref[1:2, :] = s2 / tot


def _merge_call(x2d, oa, ob, oc, w, shared):
    n = x2d.shape[0]
    tm = MERGE_TM
    weights = [w['wg'], w['wbr'], w['wo'], w['ln1_g'], w['ln1_b'], shared['wrh'], shared['wrl'], shared['rb']]
    tok = lambda width: pl.BlockSpec((tm, width), lambda i: (i, 0))
    route = pl.BlockSpec((TOP_K, tm), lambda i: (0, i))
    return pl.pallas_call(
        _merge_kernel,
        grid=(n // tm,),
        in_specs=[tok(D_MODEL), tok(A_WIDTH), tok(NA_WIDTH), tok(MLA_WIDTH)]
                 + [_const_spec(t.shape) for t in weights],
        out_specs=[tok(D_MODEL), route, route],
        out_shape=[jax.ShapeDtypeStruct((n, D_MODEL), F32),
                   jax.ShapeDtypeStruct((TOP_K, n), jnp.int32),
                   jax.ShapeDtypeStruct((TOP_K, n), F32)],
        compiler_params=_params("parallel"),
        name="merge",
    )(x2d, oa, ob, oc, *weights)


def _row_copy(src, src_row, dst, dst_row, sem):
    return pltpu.make_async_copy(src.at[pl.ds(src_row, 1)], dst.at[pl.ds(dst_row, 1)], sem)


def _dispatch_kernel(pstart_ref, pend_ref, nused_ref, dest_ref, x_ref, xs_hbm, zbuf, zsem, sem):
    i = pl.program_id(0)
    tm = dest_ref.shape[2]
    n_blocks = xs_hbm.shape[0] // MOE_BLOCK

    @pl.when(i == 0)
    def _():
        zbuf[...] = jnp.zeros_like(zbuf)

        def zero_block(first_slot):
            dst = xs_hbm.at[pl.ds(pl.multiple_of(first_slot, MOE_BLOCK), MOE_BLOCK)]
            return pltpu.make_async_copy(zbuf, dst, zsem)

        def for_each_zero_block(fn):
            for e in range(N_EXPERTS):
                @pl.when(pend_ref[e] > pstart_ref[e])
                def _():
                    fn(zero_block(pend_ref[e] - MOE_BLOCK))

                @pl.when(nused_ref[0] + e < n_blocks)
                def _():
                    fn(zero_block((nused_ref[0] + e) * MOE_BLOCK))

        for_each_zero_block(lambda cp: cp.start())
        for_each_zero_block(lambda cp: cp.wait())

    def wait(j, c):
        for k in range(TOP_K):
            _row_copy(x_ref, 0, xs_hbm, 0, sem).wait()
        return c

    for j in range(tm):
        for k in range(TOP_K):
            _row_copy(x_ref, j, xs_hbm, dest_ref[0, k, j], sem).start()
    lax.fori_loop(0, tm, wait, 0, unroll=8)


def _dispatch_call(x1, dest3, pad_start, pad_end, n_used, n_slots):
    n = x1.shape[0]
    tm = dest3.shape[2]
    grid_spec = pltpu.PrefetchScalarGridSpec(
        num_scalar_prefetch=3,
        grid=(n // tm,),
        in_specs=[pl.BlockSpec((1, TOP_K, tm), lambda i, ps, pe, nu: (i, 0, 0), memory_space=pltpu.SMEM),
                  pl.BlockSpec((tm, D_MODEL), lambda i, ps, pe, nu: (i, 0))],
        out_specs=pl.BlockSpec(memory_space=pl.ANY),
        scratch_shapes=[pltpu.VMEM((MOE_BLOCK, D_MODEL), F32), pltpu.SemaphoreType.DMA(()),
                        pltpu.SemaphoreType.DMA(())],
    )
    return pl.pallas_call(
        _dispatch_kernel,
        grid_spec=grid_spec,
        out_shape=jax.ShapeDtypeStruct((n_slots, D_MODEL), F32),
        compiler_params=_params("arbitrary"),
        name="dispatch",
    )(pad_start, pad_end, n_used, dest3, x1)


def _experts_kernel(be_ref, nused_ref, x_ref, wgu_ref, wd_ref, y_ref):
    b = pl.program_id(0)

    @pl.when(b < nused_ref[0])
    def _():
        gu = _dot(x_ref[...].astype(BF16), wgu_ref[0])
        gt = gu[:, :D_EXPERT]
        hid = gt * _sigmoid(gt) * gu[:, D_EXPERT:]
        y_ref[...] = _dot(hid.astype(BF16), wd_ref[0])

    @pl.when(b >= nused_ref[0])
    def _():
        y_ref[...] = jnp.zeros_like(y_ref)


def _experts_call(xs, block_e, n_used, w):
    n_blocks = block_e.shape[0]
    grid_spec = pltpu.PrefetchScalarGridSpec(
        num_scalar_prefetch=2,
        grid=(n_blocks,),
        in_specs=[pl.BlockSpec((MOE_BLOCK, D_MODEL), lambda b, be, nu: (jnp.minimum(b, nu[0] - 1), 0)),
                  pl.BlockSpec((1, D_MODEL, 2 * D_EXPERT), lambda b, be, nu: (be[b], 0, 0)),
                  pl.BlockSpec((1, D_EXPERT, D_MODEL), lambda b, be, nu: (be[b], 0, 0))],
        out_specs=pl.BlockSpec((MOE_BLOCK, D_MODEL), lambda b, be, nu: (b, 0)),
    )
    return pl.pallas_call(
        _experts_kernel,
        grid_spec=grid_spec,
        out_shape=jax.ShapeDtypeStruct((n_blocks * MOE_BLOCK, D_MODEL), F32),
        compiler_params=_params("arbitrary"),
        name="experts",
    )(block_e, n_used, xs, w['wgu'], w['wd'])


def _lane_column(row, width):
    tm = row.shape[1]
    col = jnp.broadcast_to(row, (LANES, tm)).T
    return jnp.concatenate([col] * (width // LANES), axis=1)


def _combine_kernel(dest_ref, gate_ref, x1_ref, y_hbm, lng_ref, lnb_ref, o_ref, buf, sem):
    tm = x1_ref.shape[0]

    def wait(j, c):
        for k in range(TOP_K):
            _row_copy(y_hbm, 0, buf.at[k], j, sem).wait()
        return c

    for j in range(tm):
        for k in range(TOP_K):
            _row_copy(y_hbm, dest_ref[0, k, j], buf.at[k], j, sem).start()
    lax.fori_loop(0, tm, wait, 0, unroll=8)
    f = (_lane_column(gate_ref[0:1, :], D_MODEL) * buf[0]
         + _lane_column(gate_ref[1:2, :], D_MODEL) * buf[1])
    o_ref[...] = _layer_norm(DEEPNORM_ALPHA * x1_ref[...] + f, lng_ref[...], lnb_ref[...])


def _combine_call(x1, yb, dest3, gate, w):
    n = x1.shape[0]
    tm = dest3.shape[2]
    return pl.pallas_call(
        _combine_kernel,
        grid=(n // tm,),
        in_specs=[pl.BlockSpec((1, TOP_K, tm), lambda i: (i, 0, 0), memory_space=pltpu.SMEM),
                  pl.BlockSpec((TOP_K, tm), lambda i: (0, i)),
                  pl.BlockSpec((tm, D_MODEL), lambda i: (i, 0)),
                  pl.BlockSpec(memory_space=pl.ANY),
                  _const_spec(w['ln2_g'].shape), _const_spec(w['ln2_b'].shape)],
        out_specs=pl.BlockSpec((tm, D_MODEL), lambda i: (i, 0)),
        out_shape=jax.ShapeDtypeStruct((n, D_MODEL), F32),
        scratch_shapes=[pltpu.VMEM((TOP_K, tm, D_MODEL), F32), pltpu.SemaphoreType.DMA(())],
        compiler_params=_params("arbitrary"),
        name="combine",
    )(dest3, gate, x1, yb, w['ln2_g'], w['ln2_b'])


def _dispatch_plan(eid):
    n = eid.shape[1]
    n_blocks = -(-(n * TOP_K) // MOE_BLOCK) + N_EXPERTS
    experts = jnp.arange(N_EXPERTS, dtype=jnp.int32)
    hit = [eid[k][:, None] == experts for k in range(TOP_K)]
    onehot = (hit[0] | hit[1]).astype(jnp.int32)
    incl = jnp.cumsum(onehot, axis=0)
    counts = incl[-1]
    padded = (counts + MOE_BLOCK - 1) // MOE_BLOCK * MOE_BLOCK
    pad_end = jnp.cumsum(padded).astype(jnp.int32)
    pad_start = pad_end - padded
    slot = pad_start[None, :] + (incl - onehot)
    dest = jnp.stack([jnp.sum(jnp.where(h, slot, 0), axis=1) for h in hit]).astype(jnp.int32)
    tm = COMBINE_TM
    dest3 = dest.reshape(TOP_K, n // tm, tm).transpose(1, 0, 2)
    first_slot = jnp.arange(n_blocks, dtype=jnp.int32) * MOE_BLOCK
    block_e = jnp.minimum(jnp.sum(pad_end[None, :] <= first_slot[:, None], axis=1),
                          N_EXPERTS - 1).astype(jnp.int32)
    n_used = pad_end[-1:] // MOE_BLOCK
    return dest3, pad_start, pad_end, block_e, n_used, n_blocks * MOE_BLOCK


def _rope_tables(seq):
    inv = 1.0 / (ROPE_BASE ** (jnp.arange(0, QK_ROPE, 2, dtype=F32) / QK_ROPE))
    ang = jnp.arange(seq, dtype=F32)[:, None] * inv[None, :]
    pad = jnp.zeros((seq, LANES - QK_ROPE), F32)
    cos, sin = jnp.cos(ang), jnp.sin(ang)
    return jnp.concatenate([cos, cos, pad], -1), jnp.concatenate([sin, sin, pad], -1)


def _pad_cols(t, width):
    return jnp.pad(t, ((0, 0), (0, width - t.shape[1])))


def _rope_weight_pair(w_rope):
    half = QK_ROPE // 2
    swapped = jnp.concatenate([-w_rope[:, half:], w_rope[:, :half]], axis=1)
    return _pad_cols(w_rope, LANES), _pad_cols(swapped, LANES)


def _layer_weights(l, w_in, a_ln_g, a_ln_b, a_ws, a_bs, na_rpb, mla_q_norm, mla_kv_norm, mla_w_uq,
                   mla_w_ukv, w_br_a, w_br_b, w_br_c, w_o, ln1_g, ln1_b, ln2_g, ln2_b, w_gate, w_up,
                   w_down):
    sizes = (2 * A_WIDTH, NA_WIDTH, NA_WIDTH, NA_WIDTH, Q_LORA, KV_LORA, QK_ROPE, N_BRANCH * D_MODEL)
    cuts = np.cumsum(sizes)[:-1].tolist()
    wa, wq, wk, wv, wcq, wckv, wkr, wg = jnp.split(w_in[l], cuts, axis=-1)
    row = lambda t: t.reshape(1, -1).astype(F32)
    uq = mla_w_uq[l].reshape(Q_LORA, MLA_HEADS, QK_NOPE + QK_ROPE)
    wqn = uq[:, :, :QK_NOPE].reshape(Q_LORA, MLA_HEADS * QK_NOPE)
    rope_pairs = [_rope_weight_pair(uq[:, h, QK_NOPE:]) for h in range(MLA_HEADS)]
    wqr = jnp.concatenate([p[0] for p in rope_pairs], axis=1)
    wqs = jnp.concatenate([p[1] for p in rope_pairs], axis=1)
    ukv = mla_w_ukv[l].reshape(KV_LORA, MLA_HEADS, QK_NOPE + V_DIM)
    wukv = jnp.concatenate([ukv[:, :, :QK_NOPE].reshape(KV_LORA, -1), ukv[:, :, QK_NOPE:].reshape(KV_LORA, -1)], 1)
    wkr_p, wkrs_p = _rope_weight_pair(wkr)
    return {
        'wa': wa.astype(BF16), 'a_ln_g': row(a_ln_g[l]), 'a_ln_b': row(a_ln_b[l]),
        'ws': a_ws[l].astype(BF16),
        'bs': jnp.broadcast_to(a_bs[l][:, :, None], (A_GROUPS, CHUNK, CHUNK)).astype(F32),
        'wqkv': jnp.concatenate([wq * NA_SCALE, wk, wv], axis=1).astype(BF16),
        'wcq': wcq.astype(BF16), 'q_norm': row(mla_q_norm[l]),
        'wqn': wqn.astype(BF16), 'wqr': wqr.astype(BF16), 'wqs': wqs.astype(BF16),
        'wckv': wckv.astype(BF16), 'kv_norm': row(mla_kv_norm[l]), 'wukv': wukv.astype(BF16),
        'wkr': wkr_p.astype(BF16), 'wkrs': wkrs_p.astype(BF16),
        'na_bias': _natten_bias(na_rpb[l]),
        'wg': wg.astype(BF16),
        'wbr': jnp.stack([w_br_a[l], w_br_b[l], w_br_c[l]]).astype(BF16),
        'wo': w_o[l].astype(BF16), 'ln1_g': row(ln1_g[l]), 'ln1_b': row(ln1_b[l]),
        'ln2_g': row(ln2_g[l]), 'ln2_b': row(ln2_b[l]),
        'wgu': jnp.concatenate([w_gate[l], w_up[l]], axis=-1).astype(BF16),
        'wd': w_down[l].astype(BF16),
    }


def _shared_weights(w_router, router_bias):
    wr_t = w_router.T.astype(F32)
    wrh = wr_t.astype(BF16)
    wrl = (wr_t - wrh.astype(F32)).astype(BF16)
    rb = jnp.broadcast_to(router_bias.astype(F32)[:, None], (N_EXPERTS, MERGE_TM))
    return {'wrh': wrh, 'wrl': wrl, 'rb': rb}


def _trunk(x, layers, shared):
    bsz, seq, _ = x.shape
    cos2, sin2 = _rope_tables(seq)
    x2d = x.reshape(bsz * seq, D_MODEL)
    for w in layers:
        oa, q, k, v, qm, km, vm = _pre_call(x2d, seq, cos2, sin2, w)
        ob = _natten_call(q, k, v, w['na_bias'], bsz, seq)
        oc = _mla_call(qm, km, vm, bsz, seq)
        x1, eid, gate = _merge_call(x2d, oa, ob, oc, w, shared)
        dest3, pad_start, pad_end, block_e, n_used, n_slots = _dispatch_plan(eid)
        xs = _dispatch_call(x1, dest3, pad_start, pad_end, n_used, n_slots)
        yb = _experts_call(xs, block_e, n_used, w)
        x2d = _combine_call(x1, yb, dest3, gate, w)
    return x2d.reshape(bsz, seq, D_MODEL)


def kernel(x_prompt, x_sample, w_in, a_ln_g, a_ln_b, a_ws, a_bs, na_rpb, mla_q_norm, mla_kv_norm, mla_w_uq, mla_w_ukv, w_br_a, w_br_b, w_br_c, w_o, ln1_g, ln1_b, ln2_g, ln2_b, w_router, router_bias, w_gate, w_up, w_down):
    layers = [_layer_weights(l, w_in, a_ln_g, a_ln_b, a_ws, a_bs, na_rpb, mla_q_norm, mla_kv_norm,
                             mla_w_uq, mla_w_ukv, w_br_a, w_br_b, w_br_c, w_o, ln1_g, ln1_b, ln2_g,
                             ln2_b, w_gate, w_up, w_down) for l in range(DEPTH)]
    shared = _shared_weights(w_router, router_bias)
    return (_trunk(x_prompt, layers, shared), _trunk(x_sample, layers, shared))
```

```python
import functools
import math

import jax
import jax.numpy as jnp
import numpy as np
from jax import lax
from jax.experimental import pallas as pl
from jax.experimental.pallas import tpu as pltpu

F32 = jnp.float32
BF16 = jnp.bfloat16

D_MODEL = 1024
DEPTH = 2
GRID_W = 64
CHUNK = 128
A_WIDTH = 512
A_GROUPS = 4
NA_HEADS = 8
NA_HEAD_DIM = 64
NA_WIN_H = 8
NA_WIN_W = 16
NA_WIDTH = NA_HEADS * NA_HEAD_DIM
NA_SCALE = NA_HEAD_DIM ** -0.5
MLA_HEADS = 4
Q_LORA = 384
KV_LORA = 256
QK_NOPE = 128
QK_ROPE = 64
V_DIM = 128
MLA_WIDTH = MLA_HEADS * V_DIM
MLA_SCALE = (QK_NOPE + QK_ROPE) ** -0.5
ROPE_BASE = 10000.0
N_BRANCH = 3
N_EXPERTS = 16
N_GROUPS = 4
EXPERTS_PER_GROUP = N_EXPERTS // N_GROUPS
TOP_K = 2
D_EXPERT = 256
MOE_BLOCK = 256
DEEPNORM_ALPHA = (2 * DEPTH) ** 0.25
LN_EPS = 1e-5
RMS_EPS = 1e-6
NEG_INF = -1e30
LOG2E = math.log2(math.e)

VMEM_LIMIT_BYTES = 56 * 1024 * 1024
LANES = 128
MLA_HEAD_PAD = 2 * LANES
NA_ROW_BLOCK = 8
NA_TOK_BLOCK = NA_ROW_BLOCK * GRID_W

ROW_W = D_MODEL + LANES
PRE_TM = 512
MERGE_TM = 512
ROUTE_TM = 256
MLA_TQ = 1024
MLA_TK = 1024
MLA_MAX_UNROLL = 5
MLA_SHIFT_LANE = QK_NOPE + QK_ROPE
MLA_L_MIN = 2.0 ** -60
MLA_L_MAX = 2.0 ** 100
MLA_BOUND_SLACK = 1.0 + 2.0 ** -7


def _params(*sem):
    return pltpu.CompilerParams(dimension_semantics=sem, vmem_limit_bytes=VMEM_LIMIT_BYTES)


def _const_spec(shape):
    nd = len(shape)
    return pl.BlockSpec(shape, lambda *_: (0,) * nd)


def _layer_norm(y, g, b):
    mu = jnp.mean(y, -1, keepdims=True)
    yc = y - mu
    var = jnp.mean(yc * yc, -1, keepdims=True)
    return yc * lax.rsqrt(var + LN_EPS) * g + b


def _rms_norm(y, g):
    return y * lax.rsqrt(jnp.mean(y * y, -1, keepdims=True) + RMS_EPS) * g


def _gelu_tanh(x):
    return 0.5 * x * (1.0 + jnp.tanh(math.sqrt(2.0 / math.pi) * (x + 0.044715 * (x * x * x))))


def _sigmoid(x):
    return 1.0 / (1.0 + jnp.exp(-x))


def _dot(a, b):
    return jnp.dot(a, b, preferred_element_type=F32)


def _dot_nt(a, b):
    return lax.dot_general(a, b, (((1,), (1,)), ((), ())), preferred_element_type=F32)


def _pre_kernel(x_ref, cos_ref, sin_ref, wa_ref, lng_ref, lnb_ref, ws_ref, bs_ref, wqkv_ref,
                wcq_ref, qnorm_ref, wqn_ref, wqr_ref, wqs_ref, wckv_ref, kvnorm_ref, wukv_ref,
                wkr_ref, wkrs_ref,
                oa_ref, q_ref, k_ref, v_ref, qm_ref, km_ref, vm_ref):
    tm = x_ref.shape[0]
    xb = x_ref[...].astype(BF16)
    cos2 = cos_ref[...]
    sin2 = sin_ref[...]

    a = _gelu_tanh(_dot(xb, wa_ref[...]))
    u = a[:, :A_WIDTH]
    vn = _layer_norm(a[:, A_WIDTH:], lng_ref[...], lnb_ref[...]).astype(BF16)
    for c in range(tm // CHUNK):
        rs = slice(c * CHUNK, (c + 1) * CHUNK)
        for g in range(A_GROUPS):
            cs = slice(g * CHUNK, (g + 1) * CHUNK)
            sp = _dot(ws_ref[g], vn[rs, cs]) + bs_ref[g]
            oa_ref[rs, cs] = (u[rs, cs] * sp).astype(BF16)

    qkv = _dot(xb, wqkv_ref[...])
    q_ref[...] = qkv[:, :NA_WIDTH].astype(BF16)
    k_ref[...] = qkv[:, NA_WIDTH:2 * NA_WIDTH].astype(BF16)
    v_ref[...] = qkv[:, 2 * NA_WIDTH:].astype(BF16)

    cqn = _rms_norm(_dot(xb, wcq_ref[...]), qnorm_ref[...]).astype(BF16)
    qn = _dot(cqn, wqn_ref[...])
    qr = _dot(cqn, wqr_ref[...])
    qs = _dot(cqn, wqs_ref[...])
    qscale = MLA_SCALE * LOG2E
    shift_lane = lax.broadcasted_iota(jnp.int32, (tm, LANES), 1) == MLA_SHIFT_LANE - QK_NOPE
    for h in range(MLA_HEADS):
        hs = slice(h * LANES, (h + 1) * LANES)
        nope = (qn[:, hs] * qscale).astype(BF16)
        rot = ((qr[:, hs] * cos2 + qs[:, hs] * sin2) * qscale).astype(BF16)
        nf, rf = nope.astype(F32), rot.astype(F32)
        norm = jnp.sqrt(jnp.sum(nf * nf, -1, keepdims=True) + jnp.sum(rf * rf, -1, keepdims=True))
        norm = jnp.broadcast_to(norm * MLA_BOUND_SLACK, rot.shape).astype(BF16)
        qm_ref[:, h * MLA_HEAD_PAD:h * MLA_HEAD_PAD + LANES] = nope
        qm_ref[:, h * MLA_HEAD_PAD + LANES:(h + 1) * MLA_HEAD_PAD] = jnp.where(shift_lane, norm, rot)

    ckvn = _rms_norm(_dot(xb, wckv_ref[...]), kvnorm_ref[...]).astype(BF16)
    kv = _dot(ckvn, wukv_ref[...])
    krope = _dot(xb, wkr_ref[...]) * cos2 + _dot(xb, wkrs_ref[...]) * sin2
    lane = lax.broadcasted_iota(jnp.int32, krope.shape, 1)
    krope = jnp.where(lane == MLA_SHIFT_LANE - QK_NOPE, -1.0, krope).astype(BF16)
    for h in range(MLA_HEADS):
        km_ref[:, h * MLA_HEAD_PAD:h * MLA_HEAD_PAD + LANES] = kv[:, h * LANES:(h + 1) * LANES].astype(BF16)
        km_ref[:, h * MLA_HEAD_PAD + LANES:(h + 1) * MLA_HEAD_PAD] = krope
    vm_ref[...] = kv[:, MLA_HEADS * QK_NOPE:].astype(BF16)


def _pre_call(x2d, seq, cos2, sin2, w):
    n = x2d.shape[0]
    tm = PRE_TM
    per_seq = seq // tm
    weights = [w['wa'], w['a_ln_g'], w['a_ln_b'], w['ws'], w['bs'], w['wqkv'], w['wcq'], w['q_norm'],
               w['wqn'], w['wqr'], w['wqs'], w['wckv'], w['kv_norm'], w['wukv'], w['wkr'], w['wkrs']]
    tok = lambda width: pl.BlockSpec((tm, width), lambda i: (i, 0))
    pos = pl.BlockSpec((tm, LANES), lambda i: (i % per_seq, 0))
    out_widths = [A_WIDTH, NA_WIDTH, NA_WIDTH, NA_WIDTH, MLA_HEADS * MLA_HEAD_PAD,
                  MLA_HEADS * MLA_HEAD_PAD, MLA_WIDTH]
    return pl.pallas_call(
        _pre_kernel,
        grid=(n // tm,),
        in_specs=[tok(D_MODEL), pos, pos] + [_const_spec(t.shape) for t in weights],
        out_specs=[tok(wd) for wd in out_widths],
        out_shape=[jax.ShapeDtypeStruct((n, wd), BF16) for wd in out_widths],
        compiler_params=_params("parallel"),
        name="pre",
    )(x2d, cos2, sin2, *weights)


def _natten_kernel(q_ref, kp_ref, kc_ref, kn_ref, vp_ref, vc_ref, vn_ref, bias_ref, o_ref,
                   kcat, vcat, *, rows):
    j = pl.program_id(1)
    nt = NA_TOK_BLOCK
    kcat[0:nt] = kp_ref[...]
    kcat[nt:2 * nt] = kc_ref[...]
    kcat[2 * nt:3 * nt] = kn_ref[...]
    vcat[0:nt] = vp_ref[...]
    vcat[nt:2 * nt] = vc_ref[...]
    vcat[2 * nt:3 * nt] = vn_ref[...]
    r0 = j * NA_ROW_BLOCK
    lane_lo = lax.broadcasted_iota(jnp.int32, (GRID_W, LANES), 1) < NA_HEAD_DIM
    for i in range(NA_ROW_BLOCK):
        r = r0 + i
        start = jnp.clip(r - NA_WIN_H // 2, 0, rows - NA_WIN_H)
        pat = start - r + (NA_WIN_H - 1)
        koff = pl.multiple_of((start - r0 + NA_ROW_BLOCK) * GRID_W, GRID_W)
        qs = slice(i * GRID_W, (i + 1) * GRID_W)
        for hp in range(NA_HEADS // 2):
            cs = slice(hp * LANES, (hp + 1) * LANES)
            qp = q_ref[qs, cs]
            zero = jnp.zeros_like(qp)
            q2 = jnp.concatenate([jnp.where(lane_lo, qp, zero), jnp.where(lane_lo, zero, qp)], axis=0)
            kk = kcat[pl.ds(koff, NA_WIN_H * GRID_W), cs]
            vv = vcat[pl.ds(koff, NA_WIN_H * GRID_W), cs]
            s = _dot_nt(q2, kk) + bias_ref[pat, hp]
            m = jnp.max(s, -1, keepdims=True)
            e = jnp.exp(s - m)
            l = jnp.sum(e, -1, keepdims=True)
            o2 = _dot(e.astype(BF16), vv) / l
            o_ref[qs, cs] = jnp.where(lane_lo, o2[:GRID_W], o2[GRID_W:]).astype(BF16)


def _natten_call(q, k, v, bias, bsz, seq):
    rows = seq // GRID_W
    nrb = rows // NA_ROW_BLOCK
    nt = NA_TOK_BLOCK
    cur = pl.BlockSpec((nt, NA_WIDTH), lambda b, j: (b * nrb + j, 0))
    prev = pl.BlockSpec((nt, NA_WIDTH), lambda b, j: (b * nrb + jnp.maximum(j - 1, 0), 0))
    nxt = pl.BlockSpec((nt, NA_WIDTH), lambda b, j: (b * nrb + jnp.minimum(j + 1, nrb - 1), 0))
    return pl.pallas_call(
        functools.partial(_natten_kernel, rows=rows),
        grid=(bsz, nrb),
        in_specs=[cur, prev, cur, nxt, prev, cur, nxt, _const_spec(bias.shape)],
        out_specs=cur,
        out_shape=jax.ShapeDtypeStruct((bsz * seq, NA_WIDTH), BF16),
        scratch_shapes=[pltpu.VMEM((3 * nt, NA_WIDTH), BF16), pltpu.VMEM((3 * nt, NA_WIDTH), BF16)],
        compiler_params=_params("parallel", "parallel"),
        name="natten",
    )(q, k, k, k, v, v, v, bias)


def _natten_bias(rpb):
    c = jnp.arange(GRID_W)
    col_start = jnp.clip(c - NA_WIN_W // 2, 0, GRID_W - NA_WIN_W)
    in_win = (c[None, :] >= col_start[:, None]) & (c[None, :] < col_start[:, None] + NA_WIN_W)
    dc = jnp.clip(c[None, :] - c[:, None] + (NA_WIN_W - 1), 0, 2 * NA_WIN_W - 2)
    onehot = (dc[:, :, None] == jnp.arange(2 * NA_WIN_W - 1)).astype(F32)
    t = jnp.einsum('hrd,qkd->hrqk', rpb.astype(F32), onehot, precision=lax.Precision.HIGHEST)
    t = jnp.where(in_win[None, None], t, NEG_INF)
    b = jnp.stack([t[:, p:p + NA_WIN_H] for p in range(NA_WIN_H)])
    b = b.transpose(0, 1, 3, 2, 4)
    return b.reshape(NA_WIN_H, NA_HEADS // 2, 2 * GRID_W, NA_WIN_H * GRID_W)


def _mla_online_softmax(q, k_ref, v_ref, tk):
    tq = q.shape[0]

    def body(j, carry):
        m, l, acc = carry
        ks = pl.ds(pl.multiple_of(j * tk, tk), tk)
        s = _dot_nt(q, k_ref[ks, :])
        m_new = jnp.maximum(m, jnp.max(s, -1, keepdims=True))
        alpha = jnp.exp2(m - m_new)
        p = jnp.exp2(s - m_new)
        l = alpha * l + jnp.sum(p, -1, keepdims=True)
        acc = alpha * acc + _dot(p.astype(BF16), v_ref[ks, :])
        return m_new, l, acc

    init = (jnp.full((tq, 1), NEG_INF, F32), jnp.zeros((tq, 1), F32), jnp.zeros((tq, V_DIM), F32))
    _, l, acc = lax.fori_loop(0, k_ref.shape[0] // tk, body, init)
    return acc, l


def _mla_kernel(q_ref, k_ref, v_ref, o_ref, kmax_ref, *, tk, unroll):
    q = q_ref[...]
    tq = q.shape[0]
    seq = k_ref.shape[0]

    @pl.when(pl.program_id(2) == 0)
    def _():
        def body(c, mx):
            kc = k_ref[pl.ds(pl.multiple_of(c * tk, tk), tk), :].astype(F32)
            n2 = jnp.sum(kc * kc, -1, keepdims=True)
            return jnp.maximum(mx, jnp.max(n2, 0, keepdims=True))

        mx = lax.fori_loop(0, seq // tk, body, jnp.zeros((1, 1), F32))
        kmax = jnp.broadcast_to(jnp.sqrt(mx) * MLA_BOUND_SLACK, kmax_ref.shape)
        lane = lax.broadcasted_iota(jnp.int32, kmax_ref.shape, 1)
        kmax_ref[...] = jnp.where(lane == MLA_SHIFT_LANE, kmax, 1.0).astype(BF16)

    q_shift = q * kmax_ref[0:1, :]
    ones = jnp.ones((tk, LANES), BF16)

    def body(j, acc):
        ks = pl.ds(pl.multiple_of(j * tk, tk), tk)
        p = jnp.exp2(_dot_nt(q_shift, k_ref[ks, :])).astype(BF16)
        return acc + _dot(p, jnp.concatenate([v_ref[ks, :], ones], axis=1))

    nk = seq // tk
    acc = lax.fori_loop(0, nk - 1, body, jnp.zeros((tq, V_DIM + LANES), F32), unroll=unroll)
    l_part = acc[:, V_DIM:]
    usable = (l_part >= MLA_L_MIN) & (l_part <= MLA_L_MAX)
    n_bad = jnp.sum(jnp.where(usable, 0.0, 1.0))
    acc = body(nk - 1, acc)
    l = acc[:, V_DIM:]
    o_ref[...] = (acc[:, :V_DIM] / l).astype(BF16)

    @pl.when(n_bad > 0.0)
    def _():
        acc, l = _mla_online_softmax(q, k_ref, v_ref, tk)
        o_ref[...] = (acc / l).astype(BF16)


def _mla_call(qm, km, vm, bsz, seq):
    tq = min(MLA_TQ, seq)
    nq = seq // tq
    tk = min(MLA_TK, seq // 2)
    looped = seq // tk - 1
    unroll = max(u for u in range(1, MLA_MAX_UNROLL + 1) if looped % u == 0)
    return pl.pallas_call(
        functools.partial(_mla_kernel, tk=tk, unroll=unroll),
        grid=(bsz, MLA_HEADS, nq),
        in_specs=[pl.BlockSpec((tq, MLA_HEAD_PAD), lambda b, h, i: (b * nq + i, h)),
                  pl.BlockSpec((seq, MLA_HEAD_PAD), lambda b, h, i: (b, h)),
                  pl.BlockSpec((seq, V_DIM), lambda b, h, i: (b, h))],
        out_specs=pl.BlockSpec((tq, V_DIM), lambda b, h, i: (b * nq + i, h)),
        out_shape=jax.ShapeDtypeStruct((bsz * seq, MLA_WIDTH), BF16),
        scratch_shapes=[pltpu.VMEM((16, MLA_HEAD_PAD), BF16)],
        compiler_params=_params("parallel", "parallel", "arbitrary"),
        name="mla",
    )(qm, km, vm)


def _first_argmax(vals):
    best, idx = vals[0], jnp.zeros(vals[0].shape, jnp.int32)
    for i in range(1, len(vals)):
        better = vals[i] > best
        idx = jnp.where(better, i, idx)
        best = jnp.where(better, vals[i], best)
    return best, idx


def _select(idx, vals):
    out = vals[0]
    for i in range(1, len(vals)):
        out = jnp.where(idx == i, vals[i], out)
    return out


def _merge_kernel(x_ref, oa_ref, ob_ref, oc_ref, wg_ref, wbr_ref, wo_ref, lng_ref, lnb_ref,
                  wrh_ref, wrl_ref, rb_ref, x1_ref, grp_ref):
    x = x_ref[...]
    xb = x.astype(BF16)
    merged = None
    for i, o_ref in enumerate((oa_ref, ob_ref, oc_ref)):
        g = _sigmoid(_dot(xb, wg_ref[:, i * D_MODEL:(i + 1) * D_MODEL]))
        term = g * _dot(o_ref[...], wbr_ref[i])
        merged = term if merged is None else merged + term
    m = _dot(merged.astype(BF16), wo_ref[...])
    x1 = _layer_norm(DEEPNORM_ALPHA * x + m, lng_ref[...], lnb_ref[...])
    x1_ref[:, :D_MODEL] = x1

    x1h = x1.astype(BF16)
    x1l = (x1 - x1h.astype(F32)).astype(BF16)
    logits = _dot_nt(wrh_ref[...], x1h) + (_dot_nt(wrl_ref[...], x1h) + _dot_nt(wrh_ref[...], x1l))
    scores = _sigmoid(logits)
    biased = scores + rb_ref[...]
    sc = [scores[e:e + 1, :] for e in range(N_EXPERTS)]
    bi = [biased[e:e + 1, :] for e in range(N_EXPERTS)]
    grp_scores = []
    for g in range(N_GROUPS):
        v = bi[g * EXPERTS_PER_GROUP:(g + 1) * EXPERTS_PER_GROUP]
        top2 = None
        for i in range(EXPERTS_PER_GROUP):
            for k in range(i + 1, EXPERTS_PER_GROUP):
                top2 = v[i] + v[k] if top2 is None else jnp.maximum(top2, v[i] + v[k])
        grp_scores.append(top2)
    _, grp = _first_argmax(grp_scores)
    cand = [_select(grp, [bi[g * EXPERTS_PER_GROUP + i] for g in range(N_GROUPS)])
            for i in range(EXPERTS_PER_GROUP)]
    cand_sc = [_select(grp, [sc[g * EXPERTS_PER_GROUP + i] for g in range(N_GROUPS)])
               for i in range(EXPERTS_PER_GROUP)]
    _, i1 = _first_argmax(cand)
    _, i2 = _first_argmax([jnp.where(i1 == i, -jnp.inf, cand[i]) for i in range(EXPERTS_PER_GROUP)])
    s1 = _select(i1, cand_sc)
    s2 = _select(i2, cand_sc)
    tot = s1 + s2
    g1 = s1 / tot
    g2 = s2 / tot
    grp_ref[...] = grp
    tm = x.shape[0]
    sub = lax.broadcasted_iota(jnp.int32, (LANES, tm), 0)
    gmat = jnp.zeros((LANES, tm), F32)
    for j in range(EXPERTS_PER_GROUP):
        gj = jnp.where(i1 == j, g1, jnp.where(i2 == j, g2, 0.0))
        gmat = jnp.where(sub == j, jnp.broadcast_to(gj, (LANES, tm)), gmat)
    x1_ref[:, D_MODEL:] = gmat.T


def _merge_call(x2d, oa, ob, oc, w, shared):
    n = x2d.shape[0]
    tm = MERGE_TM
    weights = [w['wg'], w['wbr'], w['wo'], w['ln1_g'], w['ln1_b'], shared['wrh'], shared['wrl'], shared['rb']]
    tok = lambda width: pl.BlockSpec((tm, width), lambda i: (i, 0))
    return pl.pallas_call(
        _merge_kernel,
        grid=(n // tm,),
        in_specs=[tok(D_MODEL), tok(A_WIDTH), tok(NA_WIDTH), tok(MLA_WIDTH)]
                 + [_const_spec(t.shape) for t in weights],
        out_specs=[tok(ROW_W), pl.BlockSpec((1, tm), lambda i: (0, i))],
        out_shape=[jax.ShapeDtypeStruct((n, ROW_W), F32),
                   jax.ShapeDtypeStruct((1, n), jnp.int32)],
        compiler_params=_params("parallel"),
        name="merge",
    )(x2d, oa, ob, oc, *weights)


def _row_copy(src, src_row, dst, dst_row, sem):
    return pltpu.make_async_copy(src.at[pl.ds(src_row, 1)], dst.at[pl.ds(dst_row, 1)], sem)


def _dispatch_kernel(pstart_ref, pend_ref, nused_ref, dest_ref, x_ref, xs_hbm, zbuf, zsem, sem):
    i = pl.program_id(0)
    tm = dest_ref.shape[2]
    n_blocks = xs_hbm.shape[0] // MOE_BLOCK

    @pl.when(i == 0)
    def _():
        zbuf[...] = jnp.zeros_like(zbuf)

        def zero_block(first_slot):
            dst = xs_hbm.at[pl.ds(pl.multiple_of(first_slot, MOE_BLOCK), MOE_BLOCK)]
            return pltpu.make_async_copy(zbuf, dst, zsem)

        def for_each_zero_block(fn):
            for g in range(N_GROUPS):
                @pl.when(pend_ref[g] > pstart_ref[g])
                def _():
                    fn(zero_block(pend_ref[g] - MOE_BLOCK))

                @pl.when(nused_ref[0] + g < n_blocks)
                def _():
                    fn(zero_block((nused_ref[0] + g) * MOE_BLOCK))

        for_each_zero_block(lambda cp: cp.start())
        for_each_zero_block(lambda cp: cp.wait())

    def wait(j, c):
        _row_copy(x_ref, 0, xs_hbm, 0, sem).wait()
        return c

    for j in range(tm):
        _row_copy(x_ref, j, xs_hbm, dest_ref[0, 0, j], sem).start()
    lax.fori_loop(0, tm, wait, 0, unroll=8)


def _dispatch_call(x1g, dest3, pad_start, pad_end, n_used, n_slots):
    n = x1g.shape[0]
    tm = dest3.shape[2]
    grid_spec = pltpu.PrefetchScalarGridSpec(
        num_scalar_prefetch=3,
        grid=(n // tm,),
        in_specs=[pl.BlockSpec((1, 1, tm), lambda i, ps, pe, nu: (i, 0, 0), memory_space=pltpu.SMEM),
                  pl.BlockSpec((tm, ROW_W), lambda i, ps, pe, nu: (i, 0))],
        out_specs=pl.BlockSpec(memory_space=pl.ANY),
        scratch_shapes=[pltpu.VMEM((MOE_BLOCK, ROW_W), F32), pltpu.SemaphoreType.DMA(()),
                        pltpu.SemaphoreType.DMA(())],
    )
    return pl.pallas_call(
        _dispatch_kernel,
        grid_spec=grid_spec,
        out_shape=jax.ShapeDtypeStruct((n_slots, ROW_W), F32),
        compiler_params=_params("arbitrary"),
        name="dispatch",
    )(pad_start, pad_end, n_used, dest3, x1g)


def _experts_kernel(bg_ref, nused_ref, x_ref, wgu_ref, wd_ref, y_ref):
    b = pl.program_id(0)

    @pl.when(b < nused_ref[0])
    def _():
        gates = x_ref[:, D_MODEL:]
        gu = _dot(x_ref[:, :D_MODEL].astype(BF16), wgu_ref[0])
        hid = []
        for e in range(EXPERTS_PER_GROUP):
            gt = gu[:, 2 * e * D_EXPERT:(2 * e + 1) * D_EXPERT]
            up = gu[:, (2 * e + 1) * D_EXPERT:(2 * e + 2) * D_EXPERT]
            ge = gates[:, e:e + 1]
            hid.append(jnp.where(ge > 0.0, gt * _sigmoid(gt) * up * ge, 0.0).astype(BF16))
        y_ref[...] = _dot(jnp.concatenate(hid, axis=1), wd_ref[0])

    @pl.when(b >= nused_ref[0])
    def _():
        y_ref[...] = jnp.zeros_like(y_ref)


def _experts_call(xs, block_g, n_used, w):
    n_blocks = block_g.shape[0]
    grid_spec = pltpu.PrefetchScalarGridSpec(
        num_scalar_prefetch=2,
        grid=(n_blocks,),
        in_specs=[pl.BlockSpec((MOE_BLOCK, ROW_W), lambda b, bg, nu: (jnp.minimum(b, nu[0] - 1), 0)),
                  pl.BlockSpec((1,) + w['wgu'].shape[1:], lambda b, bg, nu: (bg[b], 0, 0)),
                  pl.BlockSpec((1,) + w['wd'].shape[1:], lambda b, bg, nu: (bg[b], 0, 0))],
        out_specs=pl.BlockSpec((MOE_BLOCK, D_MODEL), lambda b, bg, nu: (b, 0)),
    )
    return pl.pallas_call(
        _experts_kernel,
        grid_spec=grid_spec,
        out_shape=jax.ShapeDtypeStruct((n_blocks * MOE_BLOCK, D_MODEL), F32),
        compiler_params=_params("arbitrary"),
        name="experts",
    )(block_g, n_used, xs, w['wgu'], w['wd'])


def _combine_kernel(dest_ref, x1_ref, y_hbm, lng_ref, lnb_ref, o_ref, buf, sem):
    tm = x1_ref.shape[0]

    def wait(j, c):
        _row_copy(y_hbm, 0, buf, j, sem).wait()
        return c

    for j in range(tm):
        _row_copy(y_hbm, dest_ref[0, 0, j], buf, j, sem).start()
    lax.fori_loop(0, tm, wait, 0, unroll=8)
    o_ref[...] = _layer_norm(DEEPNORM_ALPHA * x1_ref[...] + buf[...], lng_ref[...], lnb_ref[...])


def _combine_call(x1g, yb, dest3, w):
    n = x1g.shape[0]
    tm = dest3.shape[2]
    return pl.pallas_call(
        _combine_kernel,
        grid=(n // tm,),
        in_specs=[pl.BlockSpec((1, 1, tm), lambda i: (i, 0, 0), memory_space=pltpu.SMEM),
                  pl.BlockSpec((tm, D_MODEL), lambda i: (i, 0)),
                  pl.BlockSpec(memory_space=pl.ANY),
                  _const_spec(w['ln2_g'].shape), _const_spec(w['ln2_b'].shape)],
        out_specs=pl.BlockSpec((tm, D_MODEL), lambda i: (i, 0)),
        out_shape=jax.ShapeDtypeStruct((n, D_MODEL), F32),
        scratch_shapes=[pltpu.VMEM((tm, D_MODEL), F32), pltpu.SemaphoreType.DMA(())],
        compiler_params=_params("arbitrary"),
        name="combine",
    )(dest3, x1g, yb, w['ln2_g'], w['ln2_b'])


def _dispatch_plan(grp):
    n = grp.shape[1]
    n_blocks = -(-n // MOE_BLOCK) + N_GROUPS
    onehot = (grp[0][:, None] == jnp.arange(N_GROUPS, dtype=jnp.int32)).astype(jnp.int32)
    incl = jnp.cumsum(onehot, axis=0)
    counts = incl[-1]
    padded = (counts + MOE_BLOCK - 1) // MOE_BLOCK * MOE_BLOCK
    pad_end = jnp.cumsum(padded).astype(jnp.int32)
    pad_start = pad_end - padded
    slot = pad_start[None, :] + (incl - onehot)
    dest = jnp.sum(onehot * slot, axis=1).astype(jnp.int32)
    dest3 = dest.reshape(n // ROUTE_TM, 1, ROUTE_TM)
    first_slot = jnp.arange(n_blocks, dtype=jnp.int32) * MOE_BLOCK
    block_g = jnp.minimum(jnp.sum(pad_end[None, :] <= first_slot[:, None], axis=1),
                          N_GROUPS - 1).astype(jnp.int32)
    n_used = pad_end[-1:] // MOE_BLOCK
    return dest3, pad_start, pad_end, block_g, n_used, n_blocks * MOE_BLOCK


def _rope_tables(seq):
    inv = 1.0 / (ROPE_BASE ** (jnp.arange(0, QK_ROPE, 2, dtype=F32) / QK_ROPE))
    ang = jnp.arange(seq, dtype=F32)[:, None] * inv[None, :]
    pad = jnp.zeros((seq, LANES - QK_ROPE), F32)
    cos, sin = jnp.cos(ang), jnp.sin(ang)
    return jnp.concatenate([cos, cos, pad], -1), jnp.concatenate([sin, sin, pad], -1)


def _pad_cols(t, width):
    return jnp.pad(t, ((0, 0), (0, width - t.shape[1])))


def _rope_weight_pair(w_rope):
    half = QK_ROPE // 2
    swapped = jnp.concatenate([-w_rope[:, half:], w_rope[:, :half]], axis=1)
    return _pad_cols(w_rope, LANES), _pad_cols(swapped, LANES)


def _layer_weights(l, w_in, a_ln_g, a_ln_b, a_ws, a_bs, na_rpb, mla_q_norm, mla_kv_norm, mla_w_uq,
                   mla_w_ukv, w_br_a, w_br_b, w_br_c, w_o, ln1_g, ln1_b, ln2_g, ln2_b, w_gate, w_up,
                   w_down):
    sizes = (2 * A_WIDTH, NA_WIDTH, NA_WIDTH, NA_WIDTH, Q_LORA, KV_LORA, QK_ROPE, N_BRANCH * D_MODEL)
    cuts = np.cumsum(sizes)[:-1].tolist()
    wa, wq, wk, wv, wcq, wckv, wkr, wg = jnp.split(w_in[l], cuts, axis=-1)
    row = lambda t: t.reshape(1, -1).astype(F32)
    uq = mla_w_uq[l].reshape(Q_LORA, MLA_HEADS, QK_NOPE + QK_ROPE)
    wqn = uq[:, :, :QK_NOPE].reshape(Q_LORA, MLA_HEADS * QK_NOPE)
    rope_pairs = [_rope_weight_pair(uq[:, h, QK_NOPE:]) for h in range(MLA_HEADS)]
    wqr = jnp.concatenate([p[0] for p in rope_pairs], axis=1)
    wqs = jnp.concatenate([p[1] for p in rope_pairs], axis=1)
    ukv = mla_w_ukv[l].reshape(KV_LORA, MLA_HEADS, QK_NOPE + V_DIM)
    wukv = jnp.concatenate([ukv[:, :, :QK_NOPE].reshape(KV_LORA, -1), ukv[:, :, QK_NOPE:].reshape(KV_LORA, -1)], 1)
    wkr_p, wkrs_p = _rope_weight_pair(wkr)
    return {
        'wa': wa.astype(BF16), 'a_ln_g': row(a_ln_g[l]), 'a_ln_b': row(a_ln_b[l]),
        'ws': a_ws[l].astype(BF16),
        'bs': jnp.broadcast_to(a_bs[l][:, :, None], (A_GROUPS, CHUNK, CHUNK)).astype(F32),
        'wqkv': jnp.concatenate([wq * NA_SCALE, wk, wv], axis=1).astype(BF16),
        'wcq': wcq.astype(BF16), 'q_norm': row(mla_q_norm[l]),
        'wqn': wqn.astype(BF16), 'wqr': wqr.astype(BF16), 'wqs': wqs.astype(BF16),
        'wckv': wckv.astype(BF16), 'kv_norm': row(mla_kv_norm[l]), 'wukv': wukv.astype(BF16),
        'wkr': wkr_p.astype(BF16), 'wkrs': wkrs_p.astype(BF16),
        'na_bias': _natten_bias(na_rpb[l]),
        'wg': wg.astype(BF16),
        'wbr': jnp.stack([w_br_a[l], w_br_b[l], w_br_c[l]]).astype(BF16),
        'wo': w_o[l].astype(BF16), 'ln1_g': row(ln1_g[l]), 'ln1_b': row(ln1_b[l]),
        'ln2_g': row(ln2_g[l]), 'ln2_b': row(ln2_b[l]),
        'wgu': jnp.concatenate([w_gate[l], w_up[l]], axis=-1).reshape(
            N_GROUPS, EXPERTS_PER_GROUP, D_MODEL, 2 * D_EXPERT).transpose(0, 2, 1, 3).reshape(
            N_GROUPS, D_MODEL, EXPERTS_PER_GROUP * 2 * D_EXPERT).astype(BF16),
        'wd': w_down[l].reshape(N_GROUPS, EXPERTS_PER_GROUP * D_EXPERT, D_MODEL).astype(BF16),
    }


def _shared_weights(w_router, router_bias):
    wr_t = w_router.T.astype(F32)
    wrh = wr_t.astype(BF16)
    wrl = (wr_t - wrh.astype(F32)).astype(BF16)
    rb = jnp.broadcast_to(router_bias.astype(F32)[:, None], (N_EXPERTS, MERGE_TM))
    return {'wrh': wrh, 'wrl': wrl, 'rb': rb}


def _trunk(x, layers, shared):
    bsz, seq, _ = x.shape
    cos2, sin2 = _rope_tables(seq)
    x2d = x.reshape(bsz * seq, D_MODEL)
    for w in layers:
        oa, q, k, v, qm, km, vm = _pre_call(x2d, seq, cos2, sin2, w)
        ob = _natten_call(q, k, v, w['na_bias'], bsz, seq)
        oc = _mla_call(qm, km, vm, bsz, seq)
        x1g, grp = _merge_call(x2d, oa, ob, oc, w, shared)
        dest3, pad_start, pad_end, block_g, n_used, n_slots = _dispatch_plan(grp)
        xs = _dispatch_call(x1g, dest3, pad_start, pad_end, n_used, n_slots)
        yb = _experts_call(xs, block_g, n_used, w)
        x2d = _combine_call(x1g, yb, dest3, w)
    return x2d.reshape(bsz, seq, D_MODEL)


def kernel(x_prompt, x_sample, w_in, a_ln_g, a_ln_b, a_ws, a_bs, na_rpb, mla_q_norm, mla_kv_norm, mla_w_uq, mla_w_ukv, w_br_a, w_br_b, w_br_c, w_o, ln1_g, ln1_b, ln2_g, ln2_b, w_router, router_bias, w_gate, w_up, w_down):
    layers = [_layer_weights(l, w_in, a_ln_g, a_ln_b, a_ws, a_bs, na_rpb, mla_q_norm, mla_kv_norm,
                             mla_w_uq, mla_w_ukv, w_br_a, w_br_b, w_br_c, w_o, ln1_g, ln1_b, ln2_g,
                             ln2_b, w_gate, w_up, w_down) for l in range(DEPTH)]
    shared = _shared_weights(w_router, router_bias)
    return (_trunk(x_prompt, layers, shared), _trunk(x_sample, layers, shared))
```

```python
import functools
import math

import jax
import jax.numpy as jnp
import numpy as np
from jax import lax
from jax.experimental import pallas as pl
from jax.experimental.pallas import tpu as pltpu

F32 = jnp.float32
BF16 = jnp.bfloat16

D_MODEL = 1024
DEPTH = 2
GRID_W = 64
CHUNK = 128
A_WIDTH = 512
A_GROUPS = 4
NA_HEADS = 8
NA_HEAD_DIM = 64
NA_WIN_H = 8
NA_WIN_W = 16
NA_WIDTH = NA_HEADS * NA_HEAD_DIM
NA_SCALE = NA_HEAD_DIM ** -0.5
MLA_HEADS = 4
Q_LORA = 384
KV_LORA = 256
QK_NOPE = 128
QK_ROPE = 64
V_DIM = 128
MLA_WIDTH = MLA_HEADS * V_DIM
MLA_SCALE = (QK_NOPE + QK_ROPE) ** -0.5
ROPE_BASE = 10000.0
N_BRANCH = 3
N_EXPERTS = 16
N_GROUPS = 4
EXPERTS_PER_GROUP = N_EXPERTS // N_GROUPS
TOP_K = 2
D_EXPERT = 256
MOE_BLOCK = 256
DEEPNORM_ALPHA = (2 * DEPTH) ** 0.25
LN_EPS = 1e-5
RMS_EPS = 1e-6
NEG_INF = -1e30
LOG2E = math.log2(math.e)

VMEM_LIMIT_BYTES = 56 * 1024 * 1024
LANES = 128
MLA_HEAD_PAD = 2 * LANES
NA_ROW_BLOCK = 8
NA_TOK_BLOCK = NA_ROW_BLOCK * GRID_W

ROW_W = D_MODEL // 2 + LANES
PAIRS_PER_GROUP = EXPERTS_PER_GROUP * (EXPERTS_PER_GROUP - 1) // 2
N_CLASSES = N_GROUPS * PAIRS_PER_GROUP
HIGH_HALF = -65536
PRE_TM = 512
MERGE_TM = 512
ROUTE_TM = 256
MLA_TQ = 1024
MLA_TK = 1024
MLA_MAX_UNROLL = 5
MLA_SHIFT_LANE = QK_NOPE + QK_ROPE
MLA_L_MIN = 2.0 ** -60
MLA_L_MAX = 2.0 ** 100
MLA_BOUND_SLACK = 1.0 + 2.0 ** -7


def _params(*sem):
    return pltpu.CompilerParams(dimension_semantics=sem, vmem_limit_bytes=VMEM_LIMIT_BYTES)


def _const_spec(shape):
    nd = len(shape)
    return pl.BlockSpec(shape, lambda *_: (0,) * nd)


def _layer_norm(y, g, b):
    mu = jnp.mean(y, -1, keepdims=True)
    yc = y - mu
    var = jnp.mean(yc * yc, -1, keepdims=True)
    return yc * lax.rsqrt(var + LN_EPS) * g + b


def _rms_norm(y, g):
    return y * lax.rsqrt(jnp.mean(y * y, -1, keepdims=True) + RMS_EPS) * g


def _gelu_tanh(x):
    return 0.5 * x * (1.0 + jnp.tanh(math.sqrt(2.0 / math.pi) * (x + 0.044715 * (x * x * x))))


def _sigmoid(x):
    return 1.0 / (1.0 + jnp.exp(-x))


def _pack_bf16_pairs(y):
    w = y.shape[1] // 2
    lo = lax.bitcast_convert_type(y[:, :w].astype(BF16).astype(F32), jnp.int32)
    hi = lax.bitcast_convert_type(y[:, w:].astype(BF16).astype(F32), jnp.int32)
    return lax.shift_right_logical(lo, 16) | (hi & HIGH_HALF)


def _unpack_bf16_pairs(p):
    lo = lax.bitcast_convert_type(lax.shift_left(p, 16), F32)
    hi = lax.bitcast_convert_type(p & HIGH_HALF, F32)
    return lo, hi


def _dot(a, b):
    return jnp.dot(a, b, preferred_element_type=F32)


def _dot_nt(a, b):
    return lax.dot_general(a, b, (((1,), (1,)), ((), ())), preferred_element_type=F32)


def _pre_kernel(x_ref, cos_ref, sin_ref, wa_ref, lng_ref, lnb_ref, ws_ref, bs_ref, wqkv_ref,
                wcq_ref, qnorm_ref, wqn_ref, wqr_ref, wqs_ref, wckv_ref, kvnorm_ref, wukv_ref,
                wkr_ref, wkrs_ref,
                oa_ref, q_ref, k_ref, v_ref, qm_ref, km_ref, vm_ref):
    tm = x_ref.shape[0]
    xb = x_ref[...].astype(BF16)
    cos2 = cos_ref[...]
    sin2 = sin_ref[...]

    a = _gelu_tanh(_dot(xb, wa_ref[...]))
    u = a[:, :A_WIDTH]
    vn = _layer_norm(a[:, A_WIDTH:], lng_ref[...], lnb_ref[...]).astype(BF16)
    for c in range(tm // CHUNK):
        rs = slice(c * CHUNK, (c + 1) * CHUNK)
        for g in range(A_GROUPS):
            cs = slice(g * CHUNK, (g + 1) * CHUNK)
            sp = _dot(ws_ref[g], vn[rs, cs]) + bs_ref[g]
            oa_ref[rs, cs] = (u[rs, cs] * sp).astype(BF16)

    qkv = _dot(xb, wqkv_ref[...])
    q_ref[...] = qkv[:, :NA_WIDTH].astype(BF16)
    k_ref[...] = qkv[:, NA_WIDTH:2 * NA_WIDTH].astype(BF16)
    v_ref[...] = qkv[:, 2 * NA_WIDTH:].astype(BF16)

    cqn = _rms_norm(_dot(xb, wcq_ref[...]), qnorm_ref[...]).astype(BF16)
    qn = _dot(cqn, wqn_ref[...])
    qr = _dot(cqn, wqr_ref[...])
    qs = _dot(cqn, wqs_ref[...])
    qscale = MLA_SCALE * LOG2E
    shift_lane = lax.broadcasted_iota(jnp.int32, (tm, LANES), 1) == MLA_SHIFT_LANE - QK_NOPE
    for h in range(MLA_HEADS):
        hs = slice(h * LANES, (h + 1) * LANES)
        nope = (qn[:, hs] * qscale).astype(BF16)
        rot = ((qr[:, hs] * cos2 + qs[:, hs] * sin2) * qscale).astype(BF16)
        nf, rf = nope.astype(F32), rot.astype(F32)
        norm = jnp.sqrt(jnp.sum(nf * nf, -1, keepdims=True) + jnp.sum(rf * rf, -1, keepdims=True))
        norm = jnp.broadcast_to(norm * MLA_BOUND_SLACK, rot.shape).astype(BF16)
        qm_ref[:, h * MLA_HEAD_PAD:h * MLA_HEAD_PAD + LANES] = nope
        qm_ref[:, h * MLA_HEAD_PAD + LANES:(h + 1) * MLA_HEAD_PAD] = jnp.where(shift_lane, norm, rot)

    ckvn = _rms_norm(_dot(xb, wckv_ref[...]), kvnorm_ref[...]).astype(BF16)
    kv = _dot(ckvn, wukv_ref[...])
    krope = _dot(xb, wkr_ref[...]) * cos2 + _dot(xb, wkrs_ref[...]) * sin2
    lane = lax.broadcasted_iota(jnp.int32, krope.shape, 1)
    krope = jnp.where(lane == MLA_SHIFT_LANE - QK_NOPE, -1.0, krope).astype(BF16)
    for h in range(MLA_HEADS):
        km_ref[:, h * MLA_HEAD_PAD:h * MLA_HEAD_PAD + LANES] = kv[:, h * LANES:(h + 1) * LANES].astype(BF16)
        km_ref[:, h * MLA_HEAD_PAD + LANES:(h + 1) * MLA_HEAD_PAD] = krope
    vm_ref[...] = kv[:, MLA_HEADS * QK_NOPE:].astype(BF16)


def _pre_call(x2d, seq, cos2, sin2, w):
    n = x2d.shape[0]
    tm = PRE_TM
    per_seq = seq // tm
    weights = [w['wa'], w['a_ln_g'], w['a_ln_b'], w['ws'], w['bs'], w['wqkv'], w['wcq'], w['q_norm'],
               w['wqn'], w['wqr'], w['wqs'], w['wckv'], w['kv_norm'], w['wukv'], w['wkr'], w['wkrs']]
    tok = lambda width: pl.BlockSpec((tm, width), lambda i: (i, 0))
    pos = pl.BlockSpec((tm, LANES), lambda i: (i % per_seq, 0))
    out_widths = [A_WIDTH, NA_WIDTH, NA_WIDTH, NA_WIDTH, MLA_HEADS * MLA_HEAD_PAD,
                  MLA_HEADS * MLA_HEAD_PAD, MLA_WIDTH]
    return pl.pallas_call(
        _pre_kernel,
        grid=(n // tm,),
        in_specs=[tok(D_MODEL), pos, pos] + [_const_spec(t.shape) for t in weights],
        out_specs=[tok(wd) for wd in out_widths],
        out_shape=[jax.ShapeDtypeStruct((n, wd), BF16) for wd in out_widths],
        compiler_params=_params("parallel"),
        name="pre",
    )(x2d, cos2, sin2, *weights)


def _natten_kernel(q_ref, kp_ref, kc_ref, kn_ref, vp_ref, vc_ref, vn_ref, bias_ref, o_ref,
                   kcat, vcat, *, rows):
    j = pl.program_id(1)
    nt = NA_TOK_BLOCK
    kcat[0:nt] = kp_ref[...]
    kcat[nt:2 * nt] = kc_ref[...]
    kcat[2 * nt:3 * nt] = kn_ref[...]
    vcat[0:nt] = vp_ref[...]
    vcat[nt:2 * nt] = vc_ref[...]
    vcat[2 * nt:3 * nt] = vn_ref[...]
    r0 = j * NA_ROW_BLOCK
    lane_lo = lax.broadcasted_iota(jnp.int32, (GRID_W, LANES), 1) < NA_HEAD_DIM
    for i in range(NA_ROW_BLOCK):
        r = r0 + i
        start = jnp.clip(r - NA_WIN_H // 2, 0, rows - NA_WIN_H)
        pat = start - r + (NA_WIN_H - 1)
        koff = pl.multiple_of((start - r0 + NA_ROW_BLOCK) * GRID_W, GRID_W)
        qs = slice(i * GRID_W, (i + 1) * GRID_W)
        for hp in range(NA_HEADS // 2):
            cs = slice(hp * LANES, (hp + 1) * LANES)
            qp = q_ref[qs, cs]
            zero = jnp.zeros_like(qp)
            q2 = jnp.concatenate([jnp.where(lane_lo, qp, zero), jnp.where(lane_lo, zero, qp)], axis=0)
            kk = kcat[pl.ds(koff, NA_WIN_H * GRID_W), cs]
            vv = vcat[pl.ds(koff, NA_WIN_H * GRID_W), cs]
            s = _dot_nt(q2, kk) + bias_ref[pat, hp]
            m = jnp.max(s, -1, keepdims=True)
            e = jnp.exp(s - m)
            l = jnp.sum(e, -1, keepdims=True)
            o2 = _dot(e.astype(BF16), vv) / l
            o_ref[qs, cs] = jnp.where(lane_lo, o2[:GRID_W], o2[GRID_W:]).astype(BF16)


def _natten_call(q, k, v, bias, bsz, seq):
    rows = seq // GRID_W
    nrb = rows // NA_ROW_BLOCK
    nt = NA_TOK_BLOCK
    cur = pl.BlockSpec((nt, NA_WIDTH), lambda b, j: (b * nrb + j, 0))
    prev = pl.BlockSpec((nt, NA_WIDTH), lambda b, j: (b * nrb + jnp.maximum(j - 1, 0), 0))
    nxt = pl.BlockSpec((nt, NA_WIDTH), lambda b, j: (b * nrb + jnp.minimum(j + 1, nrb - 1), 0))
    return pl.pallas_call(
        functools.partial(_natten_kernel, rows=rows),
        grid=(bsz, nrb),
        in_specs=[cur, prev, cur, nxt, prev, cur, nxt, _const_spec(bias.shape)],
        out_specs=cur,
        out_shape=jax.ShapeDtypeStruct((bsz * seq, NA_WIDTH), BF16),
        scratch_shapes=[pltpu.VMEM((3 * nt, NA_WIDTH), BF16), pltpu.VMEM((3 * nt, NA_WIDTH), BF16)],
        compiler_params=_params("parallel", "parallel"),
        name="natten",
    )(q, k, k, k, v, v, v, bias)


def _natten_bias(rpb):
    c = jnp.arange(GRID_W)
    col_start = jnp.clip(c - NA_WIN_W // 2, 0, GRID_W - NA_WIN_W)
    in_win = (c[None, :] >= col_start[:, None]) & (c[None, :] < col_start[:, None] + NA_WIN_W)
    dc = jnp.clip(c[None, :] - c[:, None] + (NA_WIN_W - 1), 0, 2 * NA_WIN_W - 2)
    onehot = (dc[:, :, None] == jnp.arange(2 * NA_WIN_W - 1)).astype(F32)
    t = jnp.einsum('hrd,qkd->hrqk', rpb.astype(F32), onehot, precision=lax.Precision.HIGHEST)
    t = jnp.where(in_win[None, None], t, NEG_INF)
    b = jnp.stack([t[:, p:p + NA_WIN_H] for p in range(NA_WIN_H)])
    b = b.transpose(0, 1, 3, 2, 4)
    return b.reshape(NA_WIN_H, NA_HEADS // 2, 2 * GRID_W, NA_WIN_H * GRID_W)


def _mla_online_softmax(q, k_ref, v_ref, tk):
    tq = q.shape[0]

    def body(j, carry):
        m, l, acc = carry
        ks = pl.ds(pl.multiple_of(j * tk, tk), tk)
        s = _dot_nt(q, k_ref[ks, :])
        m_new = jnp.maximum(m, jnp.max(s, -1, keepdims=True))
        alpha = jnp.exp2(m - m_new)
        p = jnp.exp2(s - m_new)
        l = alpha * l + jnp.sum(p, -1, keepdims=True)
        acc = alpha * acc + _dot(p.astype(BF16), v_ref[ks, :])
        return m_new, l, acc

    init = (jnp.full((tq, 1), NEG_INF, F32), jnp.zeros((tq, 1), F32), jnp.zeros((tq, V_DIM), F32))
    _, l, acc = lax.fori_loop(0, k_ref.shape[0] // tk, body, init)
    return acc, l


def _mla_kernel(q_ref, k_ref, v_ref, o_ref, kmax_ref, *, tk, unroll):
    q = q_ref[...]
    tq = q.shape[0]
    seq = k_ref.shape[0]

    @pl.when(pl.program_id(2) == 0)
    def _():
        def body(c, mx):
            kc = k_ref[pl.ds(pl.multiple_of(c * tk, tk), tk), :].astype(F32)
            n2 = jnp.sum(kc * kc, -1, keepdims=True)
            return jnp.maximum(mx, jnp.max(n2, 0, keepdims=True))

        mx = lax.fori_loop(0, seq // tk, body, jnp.zeros((1, 1), F32))
        kmax = jnp.broadcast_to(jnp.sqrt(mx) * MLA_BOUND_SLACK, kmax_ref.shape)
        lane = lax.broadcasted_iota(jnp.int32, kmax_ref.shape, 1)
        kmax_ref[...] = jnp.where(lane == MLA_SHIFT_LANE, kmax, 1.0).astype(BF16)

    q_shift = q * kmax_ref[0:1, :]
    ones = jnp.ones((tk, LANES), BF16)

    def body(j, acc):
        ks = pl.ds(pl.multiple_of(j * tk, tk), tk)
        p = jnp.exp2(_dot_nt(q_shift, k_ref[ks, :])).astype(BF16)
        return acc + _dot(p, jnp.concatenate([v_ref[ks, :], ones], axis=1))

    nk = seq // tk
    acc = lax.fori_loop(0, nk - 1, body, jnp.zeros((tq, V_DIM + LANES), F32), unroll=unroll)
    l_part = acc[:, V_DIM:]
    usable = (l_part >= MLA_L_MIN) & (l_part <= MLA_L_MAX)
    n_bad = jnp.sum(jnp.where(usable, 0.0, 1.0))
    acc = body(nk - 1, acc)
    l = acc[:, V_DIM:]
    o_ref[...] = (acc[:, :V_DIM] / l).astype(BF16)

    @pl.when(n_bad > 0.0)
    def _():
        acc, l = _mla_online_softmax(q, k_ref, v_ref, tk)
        o_ref[...] = (acc / l).astype(BF16)


def _mla_call(qm, km, vm, bsz, seq):
    tq = min(MLA_TQ, seq)
    nq = seq // tq
    tk = min(MLA_TK, seq // 2)
    looped = seq // tk - 1
    unroll = max(u for u in range(1, MLA_MAX_UNROLL + 1) if looped % u == 0)
    return pl.pallas_call(
        functools.partial(_mla_kernel, tk=tk, unroll=unroll),
        grid=(bsz, MLA_HEADS, nq),
        in_specs=[pl.BlockSpec((tq, MLA_HEAD_PAD), lambda b, h, i: (b * nq + i, h)),
                  pl.BlockSpec((seq, MLA_HEAD_PAD), lambda b, h, i: (b, h)),
                  pl.BlockSpec((seq, V_DIM), lambda b, h, i: (b, h))],
        out_specs=pl.BlockSpec((tq, V_DIM), lambda b, h, i: (b * nq + i, h)),
        out_shape=jax.ShapeDtypeStruct((bsz * seq, MLA_WIDTH), BF16),
        scratch_shapes=[pltpu.VMEM((16, MLA_HEAD_PAD), BF16)],
        compiler_params=_params("parallel", "parallel", "arbitrary"),
        name="mla",
    )(qm, km, vm)


def _first_argmax(vals):
    best, idx = vals[0], jnp.zeros(vals[0].shape, jnp.int32)
    for i in range(1, len(vals)):
        better = vals[i] > best
        idx = jnp.where(better, i, idx)
        best = jnp.where(better, vals[i], best)
    return best, idx


def _select(idx, vals):
    out = vals[0]
    for i in range(1, len(vals)):
        out = jnp.where(idx == i, vals[i], out)
    return out


def _merge_kernel(x_ref, oa_ref, ob_ref, oc_ref, wg_ref, wbr_ref, wo_ref, lng_ref, lnb_ref,
                  wrh_ref, wrl_ref, rb_ref, x1_ref, xpk_ref, cls_ref):
    x = x_ref[...]
    xb = x.astype(BF16)
    merged = None
    for i, o_ref in enumerate((oa_ref, ob_ref, oc_ref)):
        g = _sigmoid(_dot(xb, wg_ref[:, i * D_MODEL:(i + 1) * D_MODEL]))
        term = g * _dot(o_ref[...], wbr_ref[i])
        merged = term if merged is None else merged + term
    m = _dot(merged.astype(BF16), wo_ref[...])
    x1 = _layer_norm(DEEPNORM_ALPHA * x + m, lng_ref[...], lnb_ref[...])
    x1_ref[...] = x1

    x1h = x1.astype(BF16)
    x1l = (x1 - x1h.astype(F32)).astype(BF16)
    logits = _dot_nt(wrh_ref[...], x1h) + (_dot_nt(wrl_ref[...], x1h) + _dot_nt(wrh_ref[...], x1l))
    scores = _sigmoid(logits)
    biased = scores + rb_ref[...]
    sc = [scores[e:e + 1, :] for e in range(N_EXPERTS)]
    bi = [biased[e:e + 1, :] for e in range(N_EXPERTS)]
    grp_scores = []
    for g in range(N_GROUPS):
        v = bi[g * EXPERTS_PER_GROUP:(g + 1) * EXPERTS_PER_GROUP]
        top2 = None
        for i in range(EXPERTS_PER_GROUP):
            for k in range(i + 1, EXPERTS_PER_GROUP):
                top2 = v[i] + v[k] if top2 is None else jnp.maximum(top2, v[i] + v[k])
        grp_scores.append(top2)
    _, grp = _first_argmax(grp_scores)
    cand = [_select(grp, [bi[g * EXPERTS_PER_GROUP + i] for g in range(N_GROUPS)])
            for i in range(EXPERTS_PER_GROUP)]
    cand_sc = [_select(grp, [sc[g * EXPERTS_PER_GROUP + i] for g in range(N_GROUPS)])
               for i in range(EXPERTS_PER_GROUP)]
    _, i1 = _first_argmax(cand)
    _, i2 = _first_argmax([jnp.where(i1 == i, -jnp.inf, cand[i]) for i in range(EXPERTS_PER_GROUP)])
    s1 = _select(i1, cand_sc)
    s2 = _select(i2, cand_sc)
    tot = s1 + s2
    g1 = s1 / tot
    g2 = s2 / tot
    first_lower = i1 < i2
    a = jnp.where(first_lower, i1, i2)
    b = jnp.where(first_lower, i2, i1)
    cls_ref[...] = grp * PAIRS_PER_GROUP + (((a * (7 - a)) >> 1) + (b - a - 1))
    xpk_ref[:, :D_MODEL // 2] = _pack_bf16_pairs(x1)
    tm = x.shape[0]
    sub = lax.broadcasted_iota(jnp.int32, (LANES, tm), 0)
    ga = jnp.broadcast_to(jnp.where(first_lower, g1, g2), (LANES, tm))
    gb = jnp.broadcast_to(jnp.where(first_lower, g2, g1), (LANES, tm))
    gmat = jnp.where(sub == 0, ga, jnp.where(sub == 1, gb, 0.0))
    xpk_ref[:, D_MODEL // 2:] = lax.bitcast_convert_type(gmat.T, jnp.int32)


def _merge_call(x2d, oa, ob, oc, w, shared):
    n = x2d.shape[0]
    tm = MERGE_TM
    weights = [w['wg'], w['wbr'], w['wo'], w['ln1_g'], w['ln1_b'], shared['wrh'], shared['wrl'], shared['rb']]
    tok = lambda width: pl.BlockSpec((tm, width), lambda i: (i, 0))
    return pl.pallas_call(
        _merge_kernel,
        grid=(n // tm,),
        in_specs=[tok(D_MODEL), tok(A_WIDTH), tok(NA_WIDTH), tok(MLA_WIDTH)]
                 + [_const_spec(t.shape) for t in weights],
        out_specs=[tok(D_MODEL), tok(ROW_W), pl.BlockSpec((1, tm), lambda i: (0, i))],
        out_shape=[jax.ShapeDtypeStruct((n, D_MODEL), F32),
                   jax.ShapeDtypeStruct((n, ROW_W), jnp.int32),
                   jax.ShapeDtypeStruct((1, n), jnp.int32)],
        compiler_params=_params("parallel"),
        name="merge",
    )(x2d, oa, ob, oc, *weights)


def _row_copy(src, src_row, dst, dst_row, sem):
    return pltpu.make_async_copy(src.at[pl.ds(src_row, 1)], dst.at[pl.ds(dst_row, 1)], sem)


def _dispatch_kernel(pstart_ref, pend_ref, nused_ref, dest_ref, x_ref, xs_hbm, zbuf, zsem, sem):
    i = pl.program_id(0)
    tm = dest_ref.shape[2]
    n_blocks = xs_hbm.shape[0] // MOE_BLOCK

    @pl.when(i == 0)
    def _():
        zbuf[...] = jnp.zeros_like(zbuf)

        def zero_block(first_slot):
            dst = xs_hbm.at[pl.ds(pl.multiple_of(first_slot, MOE_BLOCK), MOE_BLOCK)]
            return pltpu.make_async_copy(zbuf, dst, zsem)

        def for_each_zero_block(fn):
            for g in range(N_CLASSES):
                @pl.when(pend_ref[g] > pstart_ref[g])
                def _():
                    fn(zero_block(pend_ref[g] - MOE_BLOCK))

                @pl.when(nused_ref[0] + g < n_blocks)
                def _():
                    fn(zero_block((nused_ref[0] + g) * MOE_BLOCK))

        for_each_zero_block(lambda cp: cp.start())
        for_each_zero_block(lambda cp: cp.wait())

    def wait(j, c):
        _row_copy(x_ref, 0, xs_hbm, 0, sem).wait()
        return c

    for j in range(tm):
        _row_copy(x_ref, j, xs_hbm, dest_ref[0, 0, j], sem).start()
    lax.fori_loop(0, tm, wait, 0, unroll=8)


def _dispatch_call(xpk, dest3, pad_start, pad_end, n_used, n_slots):
    n = xpk.shape[0]
    tm = dest3.shape[2]
    grid_spec = pltpu.PrefetchScalarGridSpec(
        num_scalar_prefetch=3,
        grid=(n // tm,),
        in_specs=[pl.BlockSpec((1, 1, tm), lambda i, ps, pe, nu: (i, 0, 0), memory_space=pltpu.SMEM),
                  pl.BlockSpec((tm, ROW_W), lambda i, ps, pe, nu: (i, 0))],
        out_specs=pl.BlockSpec(memory_space=pl.ANY),
        scratch_shapes=[pltpu.VMEM((MOE_BLOCK, ROW_W), jnp.int32), pltpu.SemaphoreType.DMA(()),
                        pltpu.SemaphoreType.DMA(())],
    )
    return pl.pallas_call(
        _dispatch_kernel,
        grid_spec=grid_spec,
        out_shape=jax.ShapeDtypeStruct((n_slots, ROW_W), jnp.int32),
        compiler_params=_params("arbitrary"),
        name="dispatch",
    )(pad_start, pad_end, n_used, dest3, xpk)


def _experts_kernel(ba_ref, bb_ref, nused_ref, x_ref, wgu_a_ref, wgu_b_ref, wd_a_ref, wd_b_ref, y_ref):
    b = pl.program_id(0)

    @pl.when(b < nused_ref[0])
    def _():
        lo, hi = _unpack_bf16_pairs(x_ref[:, :D_MODEL // 2])
        lo, hi = lo.astype(BF16), hi.astype(BF16)
        gates = lax.bitcast_convert_type(x_ref[:, D_MODEL // 2:], F32)
        y = None
        for lane, (wgu_ref, wd_ref) in enumerate(((wgu_a_ref, wd_a_ref), (wgu_b_ref, wd_b_ref))):
            gu = _dot(lo, wgu_ref[0, :D_MODEL // 2, :]) + _dot(hi, wgu_ref[0, D_MODEL // 2:, :])
            gt = gu[:, :D_EXPERT]
            hid = gt * _sigmoid(gt) * gu[:, D_EXPERT:] * gates[:, lane:lane + 1]
            term = _dot(hid.astype(BF16), wd_ref[0])
            y = term if y is None else y + term
        y_ref[...] = _pack_bf16_pairs(y)

    @pl.when(b >= nused_ref[0])
    def _():
        y_ref[...] = jnp.zeros_like(y_ref)


def _experts_call(xs, block_a, block_b, n_used, w):
    n_blocks = block_a.shape[0]
    wgu_block = (1, D_MODEL, 2 * D_EXPERT)
    wd_block = (1, D_EXPERT, D_MODEL)
    grid_spec = pltpu.PrefetchScalarGridSpec(
        num_scalar_prefetch=3,
        grid=(n_blocks,),
        in_specs=[pl.BlockSpec((MOE_BLOCK, ROW_W), lambda b, ba, bb, nu: (jnp.minimum(b, nu[0] - 1), 0)),
                  pl.BlockSpec(wgu_block, lambda b, ba, bb, nu: (ba[b], 0, 0)),
                  pl.BlockSpec(wgu_block, lambda b, ba, bb, nu: (bb[b], 0, 0)),
                  pl.BlockSpec(wd_block, lambda b, ba, bb, nu: (ba[b], 0, 0)),
                  pl.BlockSpec(wd_block, lambda b, ba, bb, nu: (bb[b], 0, 0))],
        out_specs=pl.BlockSpec((MOE_BLOCK, D_MODEL // 2), lambda b, ba, bb, nu: (b, 0)),
    )
    return pl.pallas_call(
        _experts_kernel,
        grid_spec=grid_spec,
        out_shape=jax.ShapeDtypeStruct((n_blocks * MOE_BLOCK, D_MODEL // 2), jnp.int32),
        compiler_params=_params("arbitrary"),
        name="experts",
    )(block_a, block_b, n_used, xs, w['wgu'], w['wgu'], w['wd'], w['wd'])


def _combine_kernel(dest_ref, x1_ref, y_hbm, lng_ref, lnb_ref, o_ref, buf, sem):
    tm = x1_ref.shape[0]

    def wait(j, c):
        _row_copy(y_hbm, 0, buf, j, sem).wait()
        return c

    for j in range(tm):
        _row_copy(y_hbm, dest_ref[0, 0, j], buf, j, sem).start()
    lax.fori_loop(0, tm, wait, 0, unroll=8)
    f = jnp.concatenate(_unpack_bf16_pairs(buf[...]), axis=1)
    o_ref[...] = _layer_norm(DEEPNORM_ALPHA * x1_ref[...] + f, lng_ref[...], lnb_ref[...])


def _combine_call(x1, yb, dest3, w):
    n = x1.shape[0]
    tm = dest3.shape[2]
    return pl.pallas_call(
        _combine_kernel,
        grid=(n // tm,),
        in_specs=[pl.BlockSpec((1, 1, tm), lambda i: (i, 0, 0), memory_space=pltpu.SMEM),
                  pl.BlockSpec((tm, D_MODEL), lambda i: (i, 0)),
                  pl.BlockSpec(memory_space=pl.ANY),
                  _const_spec(w['ln2_g'].shape), _const_spec(w['ln2_b'].shape)],
        out_specs=pl.BlockSpec((tm, D_MODEL), lambda i: (i, 0)),
        out_shape=jax.ShapeDtypeStruct((n, D_MODEL), F32),
        scratch_shapes=[pltpu.VMEM((tm, D_MODEL // 2), jnp.int32), pltpu.SemaphoreType.DMA(())],
        compiler_params=_params("arbitrary"),
        name="combine",
    )(dest3, x1, yb, w['ln2_g'], w['ln2_b'])


def _class_experts():
    pairs = [(a, b) for a in range(EXPERTS_PER_GROUP) for b in range(a + 1, EXPERTS_PER_GROUP)]
    ea = [g * EXPERTS_PER_GROUP + a for g in range(N_GROUPS) for a, _ in pairs]
    eb = [g * EXPERTS_PER_GROUP + b for g in range(N_GROUPS) for _, b in pairs]
    return jnp.asarray(ea, jnp.int32), jnp.asarray(eb, jnp.int32)


def _dispatch_plan(cls):
    n = cls.shape[1]
    n_blocks = -(-n // MOE_BLOCK) + N_CLASSES
    onehot = (cls[0][:, None] == jnp.arange(N_CLASSES, dtype=jnp.int32)).astype(jnp.int32)
    incl = jnp.cumsum(onehot, axis=0)
    counts = incl[-1]
    padded = (counts + MOE_BLOCK - 1) // MOE_BLOCK * MOE_BLOCK
    pad_end = jnp.cumsum(padded).astype(jnp.int32)
    pad_start = pad_end - padded
    slot = pad_start[None, :] + (incl - onehot)
    dest = jnp.sum(onehot * slot, axis=1).astype(jnp.int32)
    dest3 = dest.reshape(n // ROUTE_TM, 1, ROUTE_TM)
    first_slot = jnp.arange(n_blocks, dtype=jnp.int32) * MOE_BLOCK
    block_c = jnp.minimum(jnp.sum(pad_end[None, :] <= first_slot[:, None], axis=1), N_CLASSES - 1)
    ea, eb = _class_experts()
    n_used = pad_end[-1:] // MOE_BLOCK
    return dest3, pad_start, pad_end, ea[block_c], eb[block_c], n_used, n_blocks * MOE_BLOCK


def _rope_tables(seq):
    inv = 1.0 / (ROPE_BASE ** (jnp.arange(0, QK_ROPE, 2, dtype=F32) / QK_ROPE))
    ang = jnp.arange(seq, dtype=F32)[:, None] * inv[None, :]
    pad = jnp.zeros((seq, LANES - QK_ROPE), F32)
    cos, sin = jnp.cos(ang), jnp.sin(ang)
    return jnp.concatenate([cos, cos, pad], -1), jnp.concatenate([sin, sin, pad], -1)


def _pad_cols(t, width):
    return jnp.pad(t, ((0, 0), (0, width - t.shape[1])))


def _rope_weight_pair(w_rope):
    half = QK_ROPE // 2
    swapped = jnp.concatenate([-w_rope[:, half:], w_rope[:, :half]], axis=1)
    return _pad_cols(w_rope, LANES), _pad_cols(swapped, LANES)


def _layer_weights(l, w_in, a_ln_g, a_ln_b, a_ws, a_bs, na_rpb, mla_q_norm, mla_kv_norm, mla_w_uq,
                   mla_w_ukv, w_br_a, w_br_b, w_br_c, w_o, ln1_g, ln1_b, ln2_g, ln2_b, w_gate, w_up,
                   w_down):
    sizes = (2 * A_WIDTH, NA_WIDTH, NA_WIDTH, NA_WIDTH, Q_LORA, KV_LORA, QK_ROPE, N_BRANCH * D_MODEL)
    cuts = np.cumsum(sizes)[:-1].tolist()
    wa, wq, wk, wv, wcq, wckv, wkr, wg = jnp.split(w_in[l], cuts, axis=-1)
    row = lambda t: t.reshape(1, -1).astype(F32)
    uq = mla_w_uq[l].reshape(Q_LORA, MLA_HEADS, QK_NOPE + QK_ROPE)
    wqn = uq[:, :, :QK_NOPE].reshape(Q_LORA, MLA_HEADS * QK_NOPE)
    rope_pairs = [_rope_weight_pair(uq[:, h, QK_NOPE:]) for h in range(MLA_HEADS)]
    wqr = jnp.concatenate([p[0] for p in rope_pairs], axis=1)
    wqs = jnp.concatenate([p[1] for p in rope_pairs], axis=1)
    ukv = mla_w_ukv[l].reshape(KV_LORA, MLA_HEADS, QK_NOPE + V_DIM)
    wukv = jnp.concatenate([ukv[:, :, :QK_NOPE].reshape(KV_LORA, -1), ukv[:, :, QK_NOPE:].reshape(KV_LORA, -1)], 1)
    wkr_p, wkrs_p = _rope_weight_pair(wkr)
    return {
        'wa': wa.astype(BF16), 'a_ln_g': row(a_ln_g[l]), 'a_ln_b': row(a_ln_b[l]),
        'ws': a_ws[l].astype(BF16),
        'bs': jnp.broadcast_to(a_bs[l][:, :, None], (A_GROUPS, CHUNK, CHUNK)).astype(F32),
        'wqkv': jnp.concatenate([wq * NA_SCALE, wk, wv], axis=1).astype(BF16),
        'wcq': wcq.astype(BF16), 'q_norm': row(mla_q_norm[l]),
        'wqn': wqn.astype(BF16), 'wqr': wqr.astype(BF16), 'wqs': wqs.astype(BF16),
        'wckv': wckv.astype(BF16), 'kv_norm': row(mla_kv_norm[l]), 'wukv': wukv.astype(BF16),
        'wkr': wkr_p.astype(BF16), 'wkrs': wkrs_p.astype(BF16),
        'na_bias': _natten_bias(na_rpb[l]),
        'wg': wg.astype(BF16),
        'wbr': jnp.stack([w_br_a[l], w_br_b[l], w_br_c[l]]).astype(BF16),
        'wo': w_o[l].astype(BF16), 'ln1_g': row(ln1_g[l]), 'ln1_b': row(ln1_b[l]),
        'ln2_g': row(ln2_g[l]), 'ln2_b': row(ln2_b[l]),
        'wgu': jnp.concatenate([w_gate[l], w_up[l]], axis=-1).astype(BF16),
        'wd': w_down[l].astype(BF16),
    }


def _shared_weights(w_router, router_bias):
    wr_t = w_router.T.astype(F32)
    wrh = wr_t.astype(BF16)
    wrl = (wr_t - wrh.astype(F32)).astype(BF16)
    rb = jnp.broadcast_to(router_bias.astype(F32)[:, None], (N_EXPERTS, MERGE_TM))
    return {'wrh': wrh, 'wrl': wrl, 'rb': rb}


def _trunk(x, layers, shared):
    bsz, seq, _ = x.shape
    cos2, sin2 = _rope_tables(seq)
    x2d = x.reshape(bsz * seq, D_MODEL)
    for w in layers:
        oa, q, k, v, qm, km, vm = _pre_call(x2d, seq, cos2, sin2, w)
        ob = _natten_call(q, k, v, w['na_bias'], bsz, seq)
        oc = _mla_call(qm, km, vm, bsz, seq)
        x1, xpk, cls = _merge_call(x2d, oa, ob, oc, w, shared)
        dest3, pad_start, pad_end, block_a, block_b, n_used, n_slots = _dispatch_plan(cls)
        xs = _dispatch_call(xpk, dest3, pad_start, pad_end, n_used, n_slots)
        yb = _experts_call(xs, block_a, block_b, n_used, w)
        x2d = _combine_call(x1, yb, dest3, w)
    return x2d.reshape(bsz, seq, D_MODEL)


def kernel(x_prompt, x_sample, w_in, a_ln_g, a_ln_b, a_ws, a_bs, na_rpb, mla_q_norm, mla_kv_norm, mla_w_uq, mla_w_ukv, w_br_a, w_br_b, w_br_c, w_o, ln1_g, ln1_b, ln2_g, ln2_b, w_router, router_bias, w_gate, w_up, w_down):
    layers = [_layer_weights(l, w_in, a_ln_g, a_ln_b, a_ws, a_bs, na_rpb, mla_q_norm, mla_kv_norm,
                             mla_w_uq, mla_w_ukv, w_br_a, w_br_b, w_br_c, w_o, ln1_g, ln1_b, ln2_g,
                             ln2_b, w_gate, w_up, w_down) for l in range(DEPTH)]
    shared = _shared_weights(w_router, router_bias)
    return (_trunk(x_prompt, layers, shared), _trunk(x_sample, layers, shared))
```

```python
import functools
import math

import jax
import jax.numpy as jnp
import numpy as np
from jax import lax
from jax.experimental import pallas as pl
from jax.experimental.pallas import tpu as pltpu

F32 = jnp.float32
BF16 = jnp.bfloat16

D_MODEL = 1024
DEPTH = 2
GRID_W = 64
CHUNK = 128
A_WIDTH = 512
A_GROUPS = 4
NA_HEADS = 8
NA_HEAD_DIM = 64
NA_WIN_H = 8
NA_WIN_W = 16
NA_WIDTH = NA_HEADS * NA_HEAD_DIM
NA_SCALE = NA_HEAD_DIM ** -0.5
MLA_HEADS = 4
Q_LORA = 384
KV_LORA = 256
QK_NOPE = 128
QK_ROPE = 64
V_DIM = 128
MLA_WIDTH = MLA_HEADS * V_DIM
MLA_SCALE = (QK_NOPE + QK_ROPE) ** -0.5
ROPE_BASE = 10000.0
N_BRANCH = 3
N_EXPERTS = 16
N_GROUPS = 4
EXPERTS_PER_GROUP = N_EXPERTS // N_GROUPS
TOP_K = 2
D_EXPERT = 256
MOE_BLOCK = 256
DEEPNORM_ALPHA = (2 * DEPTH) ** 0.25
LN_EPS = 1e-5
RMS_EPS = 1e-6
NEG_INF = -1e30
LOG2E = math.log2(math.e)

VMEM_LIMIT_BYTES = 56 * 1024 * 1024
LANES = 128
MLA_HEAD_PAD = 2 * LANES
NA_ROW_BLOCK = 8
NA_TOK_BLOCK = NA_ROW_BLOCK * GRID_W

ROW_W = D_MODEL // 2 + LANES
PAIRS_PER_GROUP = EXPERTS_PER_GROUP * (EXPERTS_PER_GROUP - 1) // 2
N_CLASSES = N_GROUPS * PAIRS_PER_GROUP
HIGH_HALF = -65536
PRE_TM = 512
MERGE_TM = 512
ROUTE_TM = 256
MLA_TQ = 1024
MLA_TK = 1024
MLA_MAX_UNROLL = 5
MLA_SHIFT_LANE = QK_NOPE + QK_ROPE
MLA_L_MIN = 2.0 ** -60
MLA_L_MAX = 2.0 ** 100
MLA_BOUND_SLACK = 1.0 + 2.0 ** -7
NA_BOUND_SLACK = 1.0 + 2.0 ** -6
NA_L_MIN = 2.0 ** -60
NA_L_MAX = 2.0 ** 100


def _params(*sem):
    return pltpu.CompilerParams(dimension_semantics=sem, vmem_limit_bytes=VMEM_LIMIT_BYTES)


def _const_spec(shape):
    nd = len(shape)
    return pl.BlockSpec(shape, lambda *_: (0,) * nd)


def _layer_norm(y, g, b):
    mu = jnp.mean(y, -1, keepdims=True)
    yc = y - mu
    var = jnp.mean(yc * yc, -1, keepdims=True)
    return yc * lax.rsqrt(var + LN_EPS) * g + b


def _rms_norm(y, g):
    return y * lax.rsqrt(jnp.mean(y * y, -1, keepdims=True) + RMS_EPS) * g


def _gelu_tanh(x):
    return 0.5 * x * (1.0 + jnp.tanh(math.sqrt(2.0 / math.pi) * (x + 0.044715 * (x * x * x))))


def _sigmoid(x):
    return 1.0 / (1.0 + jnp.exp(-x))


def _pack_bf16_pairs(y):
    w = y.shape[1] // 2
    lo = lax.bitcast_convert_type(y[:, :w].astype(BF16).astype(F32), jnp.int32)
    hi = lax.bitcast_convert_type(y[:, w:].astype(BF16).astype(F32), jnp.int32)
    return lax.shift_right_logical(lo, 16) | (hi & HIGH_HALF)


def _unpack_bf16_pairs(p):
    lo = lax.bitcast_convert_type(lax.shift_left(p, 16), F32)
    hi = lax.bitcast_convert_type(p & HIGH_HALF, F32)
    return lo, hi


def _dot(a, b):
    return jnp.dot(a, b, preferred_element_type=F32)


def _dot_nt(a, b):
    return lax.dot_general(a, b, (((1,), (1,)), ((), ())), preferred_element_type=F32)


def _pre_kernel(x_ref, cos_ref, sin_ref, wa_ref, lng_ref, lnb_ref, ws_ref, bs_ref, wqkv_ref,
                wcq_ref, qnorm_ref, wqn_ref, wqr_ref, wqs_ref, wckv_ref, kvnorm_ref, wukv_ref,
                wkr_ref, wkrs_ref, hsum_ref,
                oa_ref, q_ref, k_ref, v_ref, qm_ref, km_ref, vm_ref, kn2_ref):
    tm = x_ref.shape[0]
    xb = x_ref[...].astype(BF16)
    cos2 = cos_ref[...]
    sin2 = sin_ref[...]

    a = _gelu_tanh(_dot(xb, wa_ref[...]))
    u = a[:, :A_WIDTH]
    vn = _layer_norm(a[:, A_WIDTH:], lng_ref[...], lnb_ref[...]).astype(BF16)
    for c in range(tm // CHUNK):
        rs = slice(c * CHUNK, (c + 1) * CHUNK)
        for g in range(A_GROUPS):
            cs = slice(g * CHUNK, (g + 1) * CHUNK)
            sp = _dot(ws_ref[g], vn[rs, cs]) + bs_ref[g]
            oa_ref[rs, cs] = (u[rs, cs] * sp).astype(BF16)

    qkv = _dot(xb, wqkv_ref[...])
    q_ref[...] = qkv[:, :NA_WIDTH].astype(BF16)
    kb = qkv[:, NA_WIDTH:2 * NA_WIDTH].astype(BF16)
    k_ref[...] = kb
    v_ref[...] = qkv[:, 2 * NA_WIDTH:].astype(BF16)
    kf = kb.astype(F32)
    kn2_ref[...] = _dot((kf * kf).astype(BF16), hsum_ref[...])

    cqn = _rms_norm(_dot(xb, wcq_ref[...]), qnorm_ref[...]).astype(BF16)
    qn = _dot(cqn, wqn_ref[...])
    qr = _dot(cqn, wqr_ref[...])
    qs = _dot(cqn, wqs_ref[...])
    qscale = MLA_SCALE * LOG2E
    shift_lane = lax.broadcasted_iota(jnp.int32, (tm, LANES), 1) == MLA_SHIFT_LANE - QK_NOPE
    for h in range(MLA_HEADS):
        hs = slice(h * LANES, (h + 1) * LANES)
        nope = (qn[:, hs] * qscale).astype(BF16)
        rot = ((qr[:, hs] * cos2 + qs[:, hs] * sin2) * qscale).astype(BF16)
        nf, rf = nope.astype(F32), rot.astype(F32)
        norm = jnp.sqrt(jnp.sum(nf * nf, -1, keepdims=True) + jnp.sum(rf * rf, -1, keepdims=True))
        norm = jnp.broadcast_to(norm * MLA_BOUND_SLACK, rot.shape).astype(BF16)
        qm_ref[:, h * MLA_HEAD_PAD:h * MLA_HEAD_PAD + LANES] = nope
        qm_ref[:, h * MLA_HEAD_PAD + LANES:(h + 1) * MLA_HEAD_PAD] = jnp.where(shift_lane, norm, rot)

    ckvn = _rms_norm(_dot(xb, wckv_ref[...]), kvnorm_ref[...]).astype(BF16)
    kv = _dot(ckvn, wukv_ref[...])
    krope = _dot(xb, wkr_ref[...]) * cos2 + _dot(xb, wkrs_ref[...]) * sin2
    lane = lax.broadcasted_iota(jnp.int32, krope.shape, 1)
    krope = jnp.where(lane == MLA_SHIFT_LANE - QK_NOPE, -1.0, krope).astype(BF16)
    for h in range(MLA_HEADS):
        km_ref[:, h * MLA_HEAD_PAD:h * MLA_HEAD_PAD + LANES] = kv[:, h * LANES:(h + 1) * LANES].astype(BF16)
        km_ref[:, h * MLA_HEAD_PAD + LANES:(h + 1) * MLA_HEAD_PAD] = krope
    vm_ref[...] = kv[:, MLA_HEADS * QK_NOPE:].astype(BF16)


def _head_sum_matrix():
    feat = np.arange(NA_WIDTH)[:, None] // NA_HEAD_DIM
    return jnp.asarray(feat == np.arange(LANES)[None, :], BF16)


def _pre_call(x2d, seq, cos2, sin2, w):
    n = x2d.shape[0]
    tm = PRE_TM
    per_seq = seq // tm
    weights = [w['wa'], w['a_ln_g'], w['a_ln_b'], w['ws'], w['bs'], w['wqkv'], w['wcq'], w['q_norm'],
               w['wqn'], w['wqr'], w['wqs'], w['wckv'], w['kv_norm'], w['wukv'], w['wkr'], w['wkrs'],
               _head_sum_matrix()]
    tok = lambda width: pl.BlockSpec((tm, width), lambda i: (i, 0))
    pos = pl.BlockSpec((tm, LANES), lambda i: (i % per_seq, 0))
    out_widths = [A_WIDTH, NA_WIDTH, NA_WIDTH, NA_WIDTH, MLA_HEADS * MLA_HEAD_PAD,
                  MLA_HEADS * MLA_HEAD_PAD, MLA_WIDTH]
    return pl.pallas_call(
        _pre_kernel,
        grid=(n // tm,),
        in_specs=[tok(D_MODEL), pos, pos] + [_const_spec(t.shape) for t in weights],
        out_specs=[tok(wd) for wd in out_widths] + [tok(LANES)],
        out_shape=[jax.ShapeDtypeStruct((n, wd), BF16) for wd in out_widths]
                  + [jax.ShapeDtypeStruct((n, LANES), F32)],
        compiler_params=_params("parallel"),
        name="pre",
    )(x2d, cos2, sin2, *weights)


def _natten_kernel(q_ref, kp_ref, kc_ref, kn_ref, vp_ref, vc_ref, vn_ref, np_ref, nc_ref, nn_ref,
                   bias_ref, bmax_ref, o_ref, kcat, vcat, *, rows):
    j = pl.program_id(1)
    nt = NA_TOK_BLOCK
    kcat[0:nt] = kp_ref[...]
    kcat[nt:2 * nt] = kc_ref[...]
    kcat[2 * nt:3 * nt] = kn_ref[...]
    vcat[0:nt] = vp_ref[...]
    vcat[nt:2 * nt] = vc_ref[...]
    vcat[2 * nt:3 * nt] = vn_ref[...]
    r0 = j * NA_ROW_BLOCK
    lane_lo = lax.broadcasted_iota(jnp.int32, (GRID_W, LANES), 1) < NA_HEAD_DIM

    kmax2 = jnp.maximum(jnp.maximum(jnp.max(np_ref[...], 0, keepdims=True),
                                    jnp.max(nc_ref[...], 0, keepdims=True)),
                        jnp.max(nn_ref[...], 0, keepdims=True))
    kmax = jnp.broadcast_to(jnp.sqrt(kmax2) * NA_BOUND_SLACK, (2 * GRID_W, LANES))
    row_lo = lax.broadcasted_iota(jnp.int32, (2 * GRID_W, LANES), 0) < GRID_W
    lane = lax.broadcasted_iota(jnp.int32, (2 * GRID_W, LANES), 1)
    kcols = [jnp.sum(jnp.where(lane == jnp.where(row_lo, 2 * hp, 2 * hp + 1), kmax, 0.0), -1, keepdims=True)
             for hp in range(NA_HEADS // 2)]

    def attend(exact):
        bad = jnp.zeros((2 * GRID_W, 1), F32)
        for i in range(NA_ROW_BLOCK):
            r = r0 + i
            start = jnp.clip(r - NA_WIN_H // 2, 0, rows - NA_WIN_H)
            pat = start - r + (NA_WIN_H - 1)
            koff = pl.multiple_of((start - r0 + NA_ROW_BLOCK) * GRID_W, GRID_W)
            qs = slice(i * GRID_W, (i + 1) * GRID_W)
            for hp in range(NA_HEADS // 2):
                cs = slice(hp * LANES, (hp + 1) * LANES)
                qp = q_ref[qs, cs]
                zero = jnp.zeros_like(qp)
                q2 = jnp.concatenate([jnp.where(lane_lo, qp, zero), jnp.where(lane_lo, zero, qp)], axis=0)
                kk = kcat[pl.ds(koff, NA_WIN_H * GRID_W), cs]
                vv = vcat[pl.ds(koff, NA_WIN_H * GRID_W), cs]
                s = _dot_nt(q2, kk) + bias_ref[pat, hp]
                if exact:
                    m = jnp.max(s, -1, keepdims=True)
                else:
                    qf = q2.astype(F32)
                    m = jnp.sqrt(jnp.sum(qf * qf, -1, keepdims=True)) * kcols[hp] + bmax_ref[hp][:, 0:1]
                e = jnp.exp(s - m)
                l = jnp.sum(e, -1, keepdims=True)
                if not exact:
                    bad = bad + jnp.where((l >= NA_L_MIN) & (l <= NA_L_MAX), 0.0, 1.0)
                o2 = _dot(e.astype(BF16), vv) / l
                o_ref[qs, cs] = jnp.where(lane_lo, o2[:GRID_W], o2[GRID_W:]).astype(BF16)
        return bad

    n_bad = jnp.sum(attend(exact=False))

    @pl.when(n_bad > 0.0)
    def _():
        attend(exact=True)


def _natten_call(q, k, v, knorm, bias, bmax, bsz, seq):
    rows = seq // GRID_W
    nrb = rows // NA_ROW_BLOCK
    nt = NA_TOK_BLOCK
    cur = lambda width: pl.BlockSpec((nt, width), lambda b, j: (b * nrb + j, 0))
    prev = lambda width: pl.BlockSpec((nt, width), lambda b, j: (b * nrb + jnp.maximum(j - 1, 0), 0))
    nxt = lambda width: pl.BlockSpec((nt, width), lambda b, j: (b * nrb + jnp.minimum(j + 1, nrb - 1), 0))
    w = NA_WIDTH
    return pl.pallas_call(
        functools.partial(_natten_kernel, rows=rows),
        grid=(bsz, nrb),
        in_specs=[cur(w), prev(w), cur(w), nxt(w), prev(w), cur(w), nxt(w),
                  prev(LANES), cur(LANES), nxt(LANES), _const_spec(bias.shape), _const_spec(bmax.shape)],
        out_specs=cur(w),
        out_shape=jax.ShapeDtypeStruct((bsz * seq, NA_WIDTH), BF16),
        scratch_shapes=[pltpu.VMEM((3 * nt, NA_WIDTH), BF16), pltpu.VMEM((3 * nt, NA_WIDTH), BF16)],
        compiler_params=_params("parallel", "parallel"),
        name="natten",
    )(q, k, k, k, v, v, v, knorm, knorm, knorm, bias, bmax)


def _natten_bias_max(rpb):
    m = jnp.max(rpb.astype(F32), axis=(1, 2)).reshape(NA_HEADS // 2, 2, 1, 1)
    return jnp.broadcast_to(m, (NA_HEADS // 2, 2, GRID_W, LANES)).reshape(NA_HEADS // 2, 2 * GRID_W, LANES)


def _natten_bias(rpb):
    c = jnp.arange(GRID_W)
    col_start = jnp.clip(c - NA_WIN_W // 2, 0, GRID_W - NA_WIN_W)
    in_win = (c[None, :] >= col_start[:, None]) & (c[None, :] < col_start[:, None] + NA_WIN_W)
    dc = jnp.clip(c[None, :] - c[:, None] + (NA_WIN_W - 1), 0, 2 * NA_WIN_W - 2)
    onehot = (dc[:, :, None] == jnp.arange(2 * NA_WIN_W - 1)).astype(F32)
    t = jnp.einsum('hrd,qkd->hrqk', rpb.astype(F32), onehot, precision=lax.Precision.HIGHEST)
    t = jnp.where(in_win[None, None], t, NEG_INF)
    b = jnp.stack([t[:, p:p + NA_WIN_H] for p in range(NA_WIN_H)])
    b = b.transpose(0, 1, 3, 2, 4)
    return b.reshape(NA_WIN_H, NA_HEADS // 2, 2 * GRID_W, NA_WIN_H * GRID_W)


def _mla_online_softmax(q, k_ref, v_ref, tk):
    tq = q.shape[0]

    def body(j, carry):
        m, l, acc = carry
        ks = pl.ds(pl.multiple_of(j * tk, tk), tk)
        s = _dot_nt(q, k_ref[ks, :])
        m_new = jnp.maximum(m, jnp.max(s, -1, keepdims=True))
        alpha = jnp.exp2(m - m_new)
        p = jnp.exp2(s - m_new)
        l = alpha * l + jnp.sum(p, -1, keepdims=True)
        acc = alpha * acc + _dot(p.astype(BF16), v_ref[ks, :])
        return m_new, l, acc

    init = (jnp.full((tq, 1), NEG_INF, F32), jnp.zeros((tq, 1), F32), jnp.zeros((tq, V_DIM), F32))
    _, l, acc = lax.fori_loop(0, k_ref.shape[0] // tk, body, init)
    return acc, l


def _mla_kernel(q_ref, k_ref, v_ref, o_ref, kmax_ref, *, tk, unroll):
    q = q_ref[...]
    tq = q.shape[0]
    seq = k_ref.shape[0]

    @pl.when(pl.program_id(2) == 0)
    def _():
        def body(c, mx):
            kc = k_ref[pl.ds(pl.multiple_of(c * tk, tk), tk), :].astype(F32)
            n2 = jnp.sum(kc * kc, -1, keepdims=True)
            return jnp.maximum(mx, jnp.max(n2, 0, keepdims=True))

        mx = lax.fori_loop(0, seq // tk, body, jnp.zeros((1, 1), F32))
        kmax = jnp.broadcast_to(jnp.sqrt(mx) * MLA_BOUND_SLACK, kmax_ref.shape)
        lane = lax.broadcasted_iota(jnp.int32, kmax_ref.shape, 1)
        kmax_ref[...] = jnp.where(lane == MLA_SHIFT_LANE, kmax, 1.0).astype(BF16)

    q_shift = q * kmax_ref[0:1, :]
    ones = jnp.ones((tk, LANES), BF16)

    def body(j, acc):
        ks = pl.ds(pl.multiple_of(j * tk, tk), tk)
        p = jnp.exp2(_dot_nt(q_shift, k_ref[ks, :])).astype(BF16)
        return acc + _dot(p, jnp.concatenate([v_ref[ks, :], ones], axis=1))

    nk = seq // tk
    acc = lax.fori_loop(0, nk - 1, body, jnp.zeros((tq, V_DIM + LANES), F32), unroll=unroll)
    l_part = acc[:, V_DIM:]
    usable = (l_part >= MLA_L_MIN) & (l_part <= MLA_L_MAX)
    n_bad = jnp.sum(jnp.where(usable, 0.0, 1.0))
    acc = body(nk - 1, acc)
    l = acc[:, V_DIM:]
    o_ref[...] = (acc[:, :V_DIM] / l).astype(BF16)

    @pl.when(n_bad > 0.0)
    def _():
        acc, l = _mla_online_softmax(q, k_ref, v_ref, tk)
        o_ref[...] = (acc / l).astype(BF16)


def _mla_call(qm, km, vm, bsz, seq):
    tq = min(MLA_TQ, seq)
    nq = seq // tq
    tk = min(MLA_TK, seq // 2)
    looped = seq // tk - 1
    unroll = max(u for u in range(1, MLA_MAX_UNROLL + 1) if looped % u == 0)
    return pl.pallas_call(
        functools.partial(_mla_kernel, tk=tk, unroll=unroll),
        grid=(bsz, MLA_HEADS, nq),
        in_specs=[pl.BlockSpec((tq, MLA_HEAD_PAD), lambda b, h, i: (b * nq + i, h)),
                  pl.BlockSpec((seq, MLA_HEAD_PAD), lambda b, h, i: (b, h)),
                  pl.BlockSpec((seq, V_DIM), lambda b, h, i: (b, h))],
        out_specs=pl.BlockSpec((tq, V_DIM), lambda b, h, i: (b * nq + i, h)),
        out_shape=jax.ShapeDtypeStruct((bsz * seq, MLA_WIDTH), BF16),
        scratch_shapes=[pltpu.VMEM((16, MLA_HEAD_PAD), BF16)],
        compiler_params=_params("parallel", "parallel", "arbitrary"),
        name="mla",
    )(qm, km, vm)


def _first_argmax(vals):
    best, idx = vals[0], jnp.zeros(vals[0].shape, jnp.int32)
    for i in range(1, len(vals)):
        better = vals[i] > best
        idx = jnp.where(better, i, idx)
        best = jnp.where(better, vals[i], best)
    return best, idx


def _select(idx, vals):
    out = vals[0]
    for i in range(1, len(vals)):
        out = jnp.where(idx == i, vals[i], out)
    return out


def _merge_kernel(x_ref, oa_ref, ob_ref, oc_ref, wg_ref, wbr_ref, wo_ref, lng_ref, lnb_ref,
                  wrh_ref, wrl_ref, rb_ref, x1_ref, xpk_ref, cls_ref):
    x = x_ref[...]
    xb = x.astype(BF16)
    merged = None
    for i, o_ref in enumerate((oa_ref, ob_ref, oc_ref)):
        g = _sigmoid(_dot(xb, wg_ref[:, i * D_MODEL:(i + 1) * D_MODEL]))
        term = g * _dot(o_ref[...], wbr_ref[i])
        merged = term if merged is None else merged + term
    m = _dot(merged.astype(BF16), wo_ref[...])
    x1 = _layer_norm(DEEPNORM_ALPHA * x + m, lng_ref[...], lnb_ref[...])
    x1_ref[...] = x1

    x1h = x1.astype(BF16)
    x1l = (x1 - x1h.astype(F32)).astype(BF16)
    logits = _dot_nt(wrh_ref[...], x1h) + (_dot_nt(wrl_ref[...], x1h) + _dot_nt(wrh_ref[...], x1l))
    scores = _sigmoid(logits)
    biased = scores + rb_ref[...]
    sc = [scores[e:e + 1, :] for e in range(N_EXPERTS)]
    bi = [biased[e:e + 1, :] for e in range(N_EXPERTS)]
    grp_scores = []
    for g in range(N_GROUPS):
        v = bi[g * EXPERTS_PER_GROUP:(g + 1) * EXPERTS_PER_GROUP]
        top2 = None
        for i in range(EXPERTS_PER_GROUP):
            for k in range(i + 1, EXPERTS_PER_GROUP):
                top2 = v[i] + v[k] if top2 is None else jnp.maximum(top2, v[i] + v[k])
        grp_scores.append(top2)
    _, grp = _first_argmax(grp_scores)
    cand = [_select(grp, [bi[g * EXPERTS_PER_GROUP + i] for g in range(N_GROUPS)])
            for i in range(EXPERTS_PER_GROUP)]
    cand_sc = [_select(grp, [sc[g * EXPERTS_PER_GROUP + i] for g in range(N_GROUPS)])
               for i in range(EXPERTS_PER_GROUP)]
    _, i1 = _first_argmax(cand)
    _, i2 = _first_argmax([jnp.where(i1 == i, -jnp.inf, cand[i]) for i in range(EXPERTS_PER_GROUP)])
    s1 = _select(i1, cand_sc)
    s2 = _select(i2, cand_sc)
    tot = s1 + s2
    g1 = s1 / tot
    g2 = s2 / tot
    first_lower = i1 < i2
    a = jnp.where(first_lower, i1, i2)
    b = jnp.where(first_lower, i2, i1)
    cls_ref[...] = grp * PAIRS_PER_GROUP + (((a * (7 - a)) >> 1) + (b - a - 1))
    xpk_ref[:, :D_MODEL // 2] = _pack_bf16_pairs(x1)
    tm = x.shape[0]
    sub = lax.broadcasted_iota(jnp.int32, (LANES, tm), 0)
    ga = jnp.broadcast_to(jnp.where(first_lower, g1, g2), (LANES, tm))
    gb = jnp.broadcast_to(jnp.where(first_lower, g2, g1), (LANES, tm))
    gmat = jnp.where(sub == 0, ga, jnp.where(sub == 1, gb, 0.0))
    xpk_ref[:, D_MODEL // 2:] = lax.bitcast_convert_type(gmat.T, jnp.int32)


def _merge_call(x2d, oa, ob, oc, w, shared):
    n = x2d.shape[0]
    tm = MERGE_TM
    weights = [w['wg'], w['wbr'], w['wo'], w['ln1_g'], w['ln1_b'], shared['wrh'], shared['wrl'], shared['rb']]
    tok = lambda width: pl.BlockSpec((tm, width), lambda i: (i, 0))
    return pl.pallas_call(
        _merge_kernel,
        grid=(n // tm,),
        in_specs=[tok(D_MODEL), tok(A_WIDTH), tok(NA_WIDTH), tok(MLA_WIDTH)]
                 + [_const_spec(t.shape) for t in weights],
        out_specs=[tok(D_MODEL), tok(ROW_W), pl.BlockSpec((1, tm), lambda i: (0, i))],
        out_shape=[jax.ShapeDtypeStruct((n, D_MODEL), F32),
                   jax.ShapeDtypeStruct((n, ROW_W), jnp.int32),
                   jax.ShapeDtypeStruct((1, n), jnp.int32)],
        compiler_params=_params("parallel"),
        name="merge",
    )(x2d, oa, ob, oc, *weights)


def _row_copy(src, src_row, dst, dst_row, sem):
    return pltpu.make_async_copy(src.at[pl.ds(src_row, 1)], dst.at[pl.ds(dst_row, 1)], sem)


def _dispatch_kernel(pstart_ref, pend_ref, nused_ref, dest_ref, x_ref, xs_hbm, zbuf, zsem, sem):
    i = pl.program_id(0)
    tm = dest_ref.shape[2]
    n_blocks = xs_hbm.shape[0] // MOE_BLOCK

    @pl.when(i == 0)
    def _():
        zbuf[...] = jnp.zeros_like(zbuf)

        def zero_block(first_slot):
            dst = xs_hbm.at[pl.ds(pl.multiple_of(first_slot, MOE_BLOCK), MOE_BLOCK)]
            return pltpu.make_async_copy(zbuf, dst, zsem)

        def for_each_zero_block(fn):
            for g in range(N_CLASSES):
                @pl.when(pend_ref[g] > pstart_ref[g])
                def _():
                    fn(zero_block(pend_ref[g] - MOE_BLOCK))

                @pl.when(nused_ref[0] + g < n_blocks)
                def _():
                    fn(zero_block((nused_ref[0] + g) * MOE_BLOCK))

        for_each_zero_block(lambda cp: cp.start())
        for_each_zero_block(lambda cp: cp.wait())

    def wait(j, c):
        _row_copy(x_ref, 0, xs_hbm, 0, sem).wait()
        return c

    for j in range(tm):
        _row_copy(x_ref, j, xs_hbm, dest_ref[0, 0, j], sem).start(priority=j % 2)
    lax.fori_loop(0, tm, wait, 0, unroll=8)


def _dispatch_call(xpk, dest3, pad_start, pad_end, n_used, n_slots):
    n = xpk.shape[0]
    tm = dest3.shape[2]
    grid_spec = pltpu.PrefetchScalarGridSpec(
        num_scalar_prefetch=3,
        grid=(n // tm,),
        in_specs=[pl.BlockSpec((1, 1, tm), lambda i, ps, pe, nu: (i, 0, 0), memory_space=pltpu.SMEM),
                  pl.BlockSpec((tm, ROW_W), lambda i, ps, pe, nu: (i, 0))],
        out_specs=pl.BlockSpec(memory_space=pl.ANY),
        scratch_shapes=[pltpu.VMEM((MOE_BLOCK, ROW_W), jnp.int32), pltpu.SemaphoreType.DMA(()),
                        pltpu.SemaphoreType.DMA(())],
    )
    return pl.pallas_call(
        _dispatch_kernel,
        grid_spec=grid_spec,
        out_shape=jax.ShapeDtypeStruct((n_slots, ROW_W), jnp.int32),
        compiler_params=_params("arbitrary"),
        name="dispatch",
    )(pad_start, pad_end, n_used, dest3, xpk)


def _experts_kernel(ba_ref, bb_ref, nused_ref, x_ref, wgu_a_ref, wgu_b_ref, wd_a_ref, wd_b_ref, y_ref):
    b = pl.program_id(0)

    @pl.when(b < nused_ref[0])
    def _():
        lo, hi = _unpack_bf16_pairs(x_ref[:, :D_MODEL // 2])
        lo, hi = lo.astype(BF16), hi.astype(BF16)
        gates = lax.bitcast_convert_type(x_ref[:, D_MODEL // 2:], F32)
        y = None
        for lane, (wgu_ref, wd_ref) in enumerate(((wgu_a_ref, wd_a_ref), (wgu_b_ref, wd_b_ref))):
            gu = _dot(lo, wgu_ref[0, :D_MODEL // 2, :]) + _dot(hi, wgu_ref[0, D_MODEL // 2:, :])
            gt = gu[:, :D_EXPERT]
            hid = gt * _sigmoid(gt) * gu[:, D_EXPERT:] * gates[:, lane:lane + 1]
            term = _dot(hid.astype(BF16), wd_ref[0])
            y = term if y is None else y + term
        y_ref[...] = _pack_bf16_pairs(y)

    @pl.when(b >= nused_ref[0])
    def _():
        y_ref[...] = jnp.zeros_like(y_ref)


def _experts_call(xs, block_a, block_b, n_used, w):
    n_blocks = block_a.shape[0]
    wgu_block = (1, D_MODEL, 2 * D_EXPERT)
    wd_block = (1, D_EXPERT, D_MODEL)
    grid_spec = pltpu.PrefetchScalarGridSpec(
        num_scalar_prefetch=3,
        grid=(n_blocks,),
        in_specs=[pl.BlockSpec((MOE_BLOCK, ROW_W), lambda b, ba, bb, nu: (jnp.minimum(b, nu[0] - 1), 0)),
                  pl.BlockSpec(wgu_block, lambda b, ba, bb, nu: (ba[b], 0, 0)),
                  pl.BlockSpec(wgu_block, lambda b, ba, bb, nu: (bb[b], 0, 0)),
                  pl.BlockSpec(wd_block, lambda b, ba, bb, nu: (ba[b], 0, 0)),
                  pl.BlockSpec(wd_block, lambda b, ba, bb, nu: (bb[b], 0, 0))],
        out_specs=pl.BlockSpec((MOE_BLOCK, D_MODEL // 2), lambda b, ba, bb, nu: (b, 0)),
    )
    return pl.pallas_call(
        _experts_kernel,
        grid_spec=grid_spec,
        out_shape=jax.ShapeDtypeStruct((n_blocks * MOE_BLOCK, D_MODEL // 2), jnp.int32),
        compiler_params=_params("arbitrary"),
        name="experts",
    )(block_a, block_b, n_used, xs, w['wgu'], w['wgu'], w['wd'], w['wd'])


def _combine_kernel(dest_ref, x1_ref, y_hbm, lng_ref, lnb_ref, o_ref, buf, sem):
    tm = x1_ref.shape[0]

    def wait(j, c):
        _row_copy(y_hbm, 0, buf, j, sem).wait()
        return c

    for j in range(tm):
        _row_copy(y_hbm, dest_ref[0, 0, j], buf, j, sem).start(priority=j % 2)
    lax.fori_loop(0, tm, wait, 0, unroll=8)
    f = jnp.concatenate(_unpack_bf16_pairs(buf[...]), axis=1)
    o_ref[...] = _layer_norm(DEEPNORM_ALPHA * x1_ref[...] + f, lng_ref[...], lnb_ref[...])


def _combine_call(x1, yb, dest3, w):
    n = x1.shape[0]
    tm = dest3.shape[2]
    return pl.pallas_call(
        _combine_kernel,
        grid=(n // tm,),
        in_specs=[pl.BlockSpec((1, 1, tm), lambda i: (i, 0, 0), memory_space=pltpu.SMEM),
                  pl.BlockSpec((tm, D_MODEL), lambda i: (i, 0)),
                  pl.BlockSpec(memory_space=pl.ANY),
                  _const_spec(w['ln2_g'].shape), _const_spec(w['ln2_b'].shape)],
        out_specs=pl.BlockSpec((tm, D_MODEL), lambda i: (i, 0)),
        out_shape=jax.ShapeDtypeStruct((n, D_MODEL), F32),
        scratch_shapes=[pltpu.VMEM((tm, D_MODEL // 2), jnp.int32), pltpu.SemaphoreType.DMA(())],
        compiler_params=_params("arbitrary"),
        name="combine",
    )(dest3, x1, yb, w['ln2_g'], w['ln2_b'])


def _class_experts():
    pairs = [(a, b) for a in range(EXPERTS_PER_GROUP) for b in range(a + 1, EXPERTS_PER_GROUP)]
    ea = [g * EXPERTS_PER_GROUP + a for g in range(N_GROUPS) for a, _ in pairs]
    eb = [g * EXPERTS_PER_GROUP + b for g in range(N_GROUPS) for _, b in pairs]
    return jnp.asarray(ea, jnp.int32), jnp.asarray(eb, jnp.int32)


def _dispatch_plan(cls):
    n = cls.shape[1]
    n_blocks = -(-n // MOE_BLOCK) + N_CLASSES
    onehot = (cls[0][:, None] == jnp.arange(N_CLASSES, dtype=jnp.int32)).astype(jnp.int32)
    incl = jnp.cumsum(onehot, axis=0)
    counts = incl[-1]
    padded = (counts + MOE_BLOCK - 1) // MOE_BLOCK * MOE_BLOCK
    pad_end = jnp.cumsum(padded).astype(jnp.int32)
    pad_start = pad_end - padded
    slot = pad_start[None, :] + (incl - onehot)
    dest = jnp.sum(onehot * slot, axis=1).astype(jnp.int32)
    dest3 = dest.reshape(n // ROUTE_TM, 1, ROUTE_TM)
    first_slot = jnp.arange(n_blocks, dtype=jnp.int32) * MOE_BLOCK
    block_c = jnp.minimum(jnp.sum(pad_end[None, :] <= first_slot[:, None], axis=1), N_CLASSES - 1)
    ea, eb = _class_experts()
    n_used = pad_end[-1:] // MOE_BLOCK
    return dest3, pad_start, pad_end, ea[block_c], eb[block_c], n_used, n_blocks * MOE_BLOCK


def _rope_tables(seq):
    inv = 1.0 / (ROPE_BASE ** (jnp.arange(0, QK_ROPE, 2, dtype=F32) / QK_ROPE))
    ang = jnp.arange(seq, dtype=F32)[:, None] * inv[None, :]
    pad = jnp.zeros((seq, LANES - QK_ROPE), F32)
    cos, sin = jnp.cos(ang), jnp.sin(ang)
    return jnp.concatenate([cos, cos, pad], -1), jnp.concatenate([sin, sin, pad], -1)


def _pad_cols(t, width):
    return jnp.pad(t, ((0, 0), (0, width - t.shape[1])))


def _rope_weight_pair(w_rope):
    half = QK_ROPE // 2
    swapped = jnp.concatenate([-w_rope[:, half:], w_rope[:, :half]], axis=1)
    return _pad_cols(w_rope, LANES), _pad_cols(swapped, LANES)


def _layer_weights(l, w_in, a_ln_g, a_ln_b, a_ws, a_bs, na_rpb, mla_q_norm, mla_kv_norm, mla_w_uq,
                   mla_w_ukv, w_br_a, w_br_b, w_br_c, w_o, ln1_g, ln1_b, ln2_g, ln2_b, w_gate, w_up,
                   w_down):
    sizes = (2 * A_WIDTH, NA_WIDTH, NA_WIDTH, NA_WIDTH, Q_LORA, KV_LORA, QK_ROPE, N_BRANCH * D_MODEL)
    cuts = np.cumsum(sizes)[:-1].tolist()
    wa, wq, wk, wv, wcq, wckv, wkr, wg = jnp.split(w_in[l], cuts, axis=-1)
    row = lambda t: t.reshape(1, -1).astype(F32)
    uq = mla_w_uq[l].reshape(Q_LORA, MLA_HEADS, QK_NOPE + QK_ROPE)
    wqn = uq[:, :, :QK_NOPE].reshape(Q_LORA, MLA_HEADS * QK_NOPE)
    rope_pairs = [_rope_weight_pair(uq[:, h, QK_NOPE:]) for h in range(MLA_HEADS)]
    wqr = jnp.concatenate([p[0] for p in rope_pairs], axis=1)
    wqs = jnp.concatenate([p[1] for p in rope_pairs], axis=1)
    ukv = mla_w_ukv[l].reshape(KV_LORA, MLA_HEADS, QK_NOPE + V_DIM)
    wukv = jnp.concatenate([ukv[:, :, :QK_NOPE].reshape(KV_LORA, -1), ukv[:, :, QK_NOPE:].reshape(KV_LORA, -1)], 1)
    wkr_p, wkrs_p = _rope_weight_pair(wkr)
    return {
        'wa': wa.astype(BF16), 'a_ln_g': row(a_ln_g[l]), 'a_ln_b': row(a_ln_b[l]),
        'ws': a_ws[l].astype(BF16),
        'bs': jnp.broadcast_to(a_bs[l][:, :, None], (A_GROUPS, CHUNK, CHUNK)).astype(F32),
        'wqkv': jnp.concatenate([wq * NA_SCALE, wk, wv], axis=1).astype(BF16),
        'wcq': wcq.astype(BF16), 'q_norm': row(mla_q_norm[l]),
        'wqn': wqn.astype(BF16), 'wqr': wqr.astype(BF16), 'wqs': wqs.astype(BF16),
        'wckv': wckv.astype(BF16), 'kv_norm': row(mla_kv_norm[l]), 'wukv': wukv.astype(BF16),
        'wkr': wkr_p.astype(BF16), 'wkrs': wkrs_p.astype(BF16),
        'na_bias': _natten_bias(na_rpb[l]), 'na_bmax': _natten_bias_max(na_rpb[l]),
        'wg': wg.astype(BF16),
        'wbr': jnp.stack([w_br_a[l], w_br_b[l], w_br_c[l]]).astype(BF16),
        'wo': w_o[l].astype(BF16), 'ln1_g': row(ln1_g[l]), 'ln1_b': row(ln1_b[l]),
        'ln2_g': row(ln2_g[l]), 'ln2_b': row(ln2_b[l]),
        'wgu': jnp.concatenate([w_gate[l], w_up[l]], axis=-1).astype(BF16),
        'wd': w_down[l].astype(BF16),
    }


def _shared_weights(w_router, router_bias):
    wr_t = w_router.T.astype(F32)
    wrh = wr_t.astype(BF16)
    wrl = (wr_t - wrh.astype(F32)).astype(BF16)
    rb = jnp.broadcast_to(router_bias.astype(F32)[:, None], (N_EXPERTS, MERGE_TM))
    return {'wrh': wrh, 'wrl': wrl, 'rb': rb}


def _trunk(x, layers, shared):
    bsz, seq, _ = x.shape
    cos2, sin2 = _rope_tables(seq)
    x2d = x.reshape(bsz * seq, D_MODEL)
    for w in layers:
        oa, q, k, v, qm, km, vm, kn2 = _pre_call(x2d, seq, cos2, sin2, w)
        ob = _natten_call(q, k, v, kn2, w['na_bias'], w['na_bmax'], bsz, seq)
        oc = _mla_call(qm, km, vm, bsz, seq)
        x1, xpk, cls = _merge_call(x2d, oa, ob, oc, w, shared)
        dest3, pad_start, pad_end, block_a, block_b, n_used, n_slots = _dispatch_plan(cls)
        xs = _dispatch_call(xpk, dest3, pad_start, pad_end, n_used, n_slots)
        yb = _experts_call(xs, block_a, block_b, n_used, w)
        x2d = _combine_call(x1, yb, dest3, w)
    return x2d.reshape(bsz, seq, D_MODEL)


def kernel(x_prompt, x_sample, w_in, a_ln_g, a_ln_b, a_ws, a_bs, na_rpb, mla_q_norm, mla_kv_norm, mla_w_uq, mla_w_ukv, w_br_a, w_br_b, w_br_c, w_o, ln1_g, ln1_b, ln2_g, ln2_b, w_router, router_bias, w_gate, w_up, w_down):
    layers = [_layer_weights(l, w_in, a_ln_g, a_ln_b, a_ws, a_bs, na_rpb, mla_q_norm, mla_kv_norm,
                             mla_w_uq, mla_w_ukv, w_br_a, w_br_b, w_br_c, w_o, ln1_g, ln1_b, ln2_g,
                             ln2_b, w_gate, w_up, w_down) for l in range(DEPTH)]
    shared = _shared_weights(w_router, router_bias)
    return (_trunk(x_prompt, layers, shared), _trunk(x_sample, layers, shared))
```

```python
import functools
import math

import jax
import jax.numpy as jnp
import numpy as np
from jax import lax
from jax.experimental import pallas as pl
from jax.experimental.pallas import tpu as pltpu

F32 = jnp.float32
BF16 = jnp.bfloat16

D_MODEL = 1024
DEPTH = 2
GRID_W = 64
CHUNK = 128
A_WIDTH = 512
A_GROUPS = 4
NA_HEADS = 8
NA_HEAD_DIM = 64
NA_WIN_H = 8
NA_WIN_W = 16
NA_WIDTH = NA_HEADS * NA_HEAD_DIM
NA_SCALE = NA_HEAD_DIM ** -0.5
MLA_HEADS = 4
Q_LORA = 384
KV_LORA = 256
QK_NOPE = 128
QK_ROPE = 64
V_DIM = 128
MLA_WIDTH = MLA_HEADS * V_DIM
MLA_SCALE = (QK_NOPE + QK_ROPE) ** -0.5
ROPE_BASE = 10000.0
N_BRANCH = 3
N_EXPERTS = 16
N_GROUPS = 4
EXPERTS_PER_GROUP = N_EXPERTS // N_GROUPS
TOP_K = 2
D_EXPERT = 256
MOE_BLOCK = 256
DEEPNORM_ALPHA = (2 * DEPTH) ** 0.25
LN_EPS = 1e-5
RMS_EPS = 1e-6
NEG_INF = -1e30
LOG2E = math.log2(math.e)

VMEM_LIMIT_BYTES = 56 * 1024 * 1024
LANES = 128
MLA_HEAD_PAD = 2 * LANES
NA_ROW_BLOCK = 8
NA_TOK_BLOCK = NA_ROW_BLOCK * GRID_W

ROW_W = D_MODEL // 2 + LANES
PAIRS_PER_GROUP = EXPERTS_PER_GROUP * (EXPERTS_PER_GROUP - 1) // 2
N_CLASSES = N_GROUPS * PAIRS_PER_GROUP
HIGH_HALF = -65536
PRE_TM = 512
MERGE_TM = 512
ROUTE_TM = 256
COMBINE_TM = 512
MLA_TQ = 1024
MLA_TK = 1024
MLA_MAX_UNROLL = 5
MLA_SHIFT_LANE = QK_NOPE + QK_ROPE
MLA_L_MIN = 2.0 ** -60
MLA_L_MAX = 2.0 ** 100
MLA_BOUND_SLACK = 1.0 + 2.0 ** -7
NA_BOUND_SLACK = 1.0 + 2.0 ** -6
NA_L_MIN = 2.0 ** -60
NA_L_MAX = 2.0 ** 100


def _params(*sem):
    return pltpu.CompilerParams(dimension_semantics=sem, vmem_limit_bytes=VMEM_LIMIT_BYTES)


def _const_spec(shape):
    nd = len(shape)
    return pl.BlockSpec(shape, lambda *_: (0,) * nd)


def _layer_norm(y, g, b):
    mu = jnp.mean(y, -1, keepdims=True)
    yc = y - mu
    var = jnp.mean(yc * yc, -1, keepdims=True)
    return yc * lax.rsqrt(var + LN_EPS) * g + b


def _rms_norm(y, g):
    return y * lax.rsqrt(jnp.mean(y * y, -1, keepdims=True) + RMS_EPS) * g


def _gelu_tanh(x):
    return 0.5 * x * (1.0 + jnp.tanh(math.sqrt(2.0 / math.pi) * (x + 0.044715 * (x * x * x))))


def _sigmoid(x):
    return 1.0 / (1.0 + jnp.exp(-x))


def _pack_bf16_pairs(y):
    w = y.shape[1] // 2
    lo = lax.bitcast_convert_type(y[:, :w].astype(BF16).astype(F32), jnp.int32)
    hi = lax.bitcast_convert_type(y[:, w:].astype(BF16).astype(F32), jnp.int32)
    return lax.shift_right_logical(lo, 16) | (hi & HIGH_HALF)


def _unpack_bf16_pairs(p):
    lo = lax.bitcast_convert_type(lax.shift_left(p, 16), F32)
    hi = lax.bitcast_convert_type(p & HIGH_HALF, F32)
    return lo, hi


def _dot(a, b):
    return jnp.dot(a, b, preferred_element_type=F32)


def _dot_nt(a, b):
    return lax.dot_general(a, b, (((1,), (1,)), ((), ())), preferred_element_type=F32)


def _pre_kernel(x_ref, cos_ref, sin_ref, wa_ref, lng_ref, lnb_ref, ws_ref, bs_ref, wqkv_ref,
                wcq_ref, qnorm_ref, wqn_ref, wqr_ref, wqs_ref, wckv_ref, kvnorm_ref, wukv_ref,
                wkr_ref, wkrs_ref, hsum_ref,
                oa_ref, q_ref, k_ref, v_ref, qm_ref, km_ref, vm_ref, kn2_ref):
    tm = x_ref.shape[0]
    xb = x_ref[...].astype(BF16)
    cos2 = cos_ref[...]
    sin2 = sin_ref[...]

    a = _gelu_tanh(_dot(xb, wa_ref[...]))
    u = a[:, :A_WIDTH]
    vn = _layer_norm(a[:, A_WIDTH:], lng_ref[...], lnb_ref[...]).astype(BF16)
    for c in range(tm // CHUNK):
        rs = slice(c * CHUNK, (c + 1) * CHUNK)
        for g in range(A_GROUPS):
            cs = slice(g * CHUNK, (g + 1) * CHUNK)
            sp = _dot(ws_ref[g], vn[rs, cs]) + bs_ref[g]
            oa_ref[rs, cs] = (u[rs, cs] * sp).astype(BF16)

    qkv = _dot(xb, wqkv_ref[...])
    q_ref[...] = qkv[:, :NA_WIDTH].astype(BF16)
    kb = qkv[:, NA_WIDTH:2 * NA_WIDTH].astype(BF16)
    k_ref[...] = kb
    v_ref[...] = qkv[:, 2 * NA_WIDTH:].astype(BF16)
    kf = kb.astype(F32)
    kn2_ref[...] = _dot((kf * kf).astype(BF16), hsum_ref[...])

    cqn = _rms_norm(_dot(xb, wcq_ref[...]), qnorm_ref[...]).astype(BF16)
    qn = _dot(cqn, wqn_ref[...])
    qr = _dot(cqn, wqr_ref[...])
    qs = _dot(cqn, wqs_ref[...])
    qscale = MLA_SCALE * LOG2E
    shift_lane = lax.broadcasted_iota(jnp.int32, (tm, LANES), 1) == MLA_SHIFT_LANE - QK_NOPE
    for h in range(MLA_HEADS):
        hs = slice(h * LANES, (h + 1) * LANES)
        nope = (qn[:, hs] * qscale).astype(BF16)
        rot = ((qr[:, hs] * cos2 + qs[:, hs] * sin2) * qscale).astype(BF16)
        nf, rf = nope.astype(F32), rot.astype(F32)
        norm = jnp.sqrt(jnp.sum(nf * nf, -1, keepdims=True) + jnp.sum(rf * rf, -1, keepdims=True))
        norm = jnp.broadcast_to(norm * MLA_BOUND_SLACK, rot.shape).astype(BF16)
        qm_ref[:, h * MLA_HEAD_PAD:h * MLA_HEAD_PAD + LANES] = nope
        qm_ref[:, h * MLA_HEAD_PAD + LANES:(h + 1) * MLA_HEAD_PAD] = jnp.where(shift_lane, norm, rot)

    ckvn = _rms_norm(_dot(xb, wckv_ref[...]), kvnorm_ref[...]).astype(BF16)
    kv = _dot(ckvn, wukv_ref[...])
    krope = _dot(xb, wkr_ref[...]) * cos2 + _dot(xb, wkrs_ref[...]) * sin2
    lane = lax.broadcasted_iota(jnp.int32, krope.shape, 1)
    krope = jnp.where(lane == MLA_SHIFT_LANE - QK_NOPE, -1.0, krope).astype(BF16)
    for h in range(MLA_HEADS):
        km_ref[:, h * MLA_HEAD_PAD:h * MLA_HEAD_PAD + LANES] = kv[:, h * LANES:(h + 1) * LANES].astype(BF16)
        km_ref[:, h * MLA_HEAD_PAD + LANES:(h + 1) * MLA_HEAD_PAD] = krope
    vm_ref[...] = kv[:, MLA_HEADS * QK_NOPE:].astype(BF16)


def _head_sum_matrix():
    feat = np.arange(NA_WIDTH)[:, None] // NA_HEAD_DIM
    return jnp.asarray(feat == np.arange(LANES)[None, :], BF16)


def _pre_call(x2d, seq, cos2, sin2, w):
    n = x2d.shape[0]
    tm = PRE_TM
    per_seq = seq // tm
    weights = [w['wa'], w['a_ln_g'], w['a_ln_b'], w['ws'], w['bs'], w['wqkv'], w['wcq'], w['q_norm'],
               w['wqn'], w['wqr'], w['wqs'], w['wckv'], w['kv_norm'], w['wukv'], w['wkr'], w['wkrs'],
               _head_sum_matrix()]
    tok = lambda width: pl.BlockSpec((tm, width), lambda i: (i, 0))
    pos = pl.BlockSpec((tm, LANES), lambda i: (i % per_seq, 0))
    out_widths = [A_WIDTH, NA_WIDTH, NA_WIDTH, NA_WIDTH, MLA_HEADS * MLA_HEAD_PAD,
                  MLA_HEADS * MLA_HEAD_PAD, MLA_WIDTH]
    return pl.pallas_call(
        _pre_kernel,
        grid=(n // tm,),
        in_specs=[tok(D_MODEL), pos, pos] + [_const_spec(t.shape) for t in weights],
        out_specs=[tok(wd) for wd in out_widths] + [tok(LANES)],
        out_shape=[jax.ShapeDtypeStruct((n, wd), BF16) for wd in out_widths]
                  + [jax.ShapeDtypeStruct((n, LANES), F32)],
        compiler_params=_params("parallel"),
        name="pre",
    )(x2d, cos2, sin2, *weights)


def _natten_kernel(q_ref, kp_ref, kc_ref, kn_ref, vp_ref, vc_ref, vn_ref, np_ref, nc_ref, nn_ref,
                   bias_ref, bmax_ref, o_ref, kcat, vcat, *, rows):
    j = pl.program_id(1)
    nt = NA_TOK_BLOCK
    kcat[0:nt] = kp_ref[...]
    kcat[nt:2 * nt] = kc_ref[...]
    kcat[2 * nt:3 * nt] = kn_ref[...]
    vcat[0:nt] = vp_ref[...]
    vcat[nt:2 * nt] = vc_ref[...]
    vcat[2 * nt:3 * nt] = vn_ref[...]
    r0 = j * NA_ROW_BLOCK
    lane_lo = lax.broadcasted_iota(jnp.int32, (GRID_W, LANES), 1) < NA_HEAD_DIM

    kmax2 = jnp.maximum(jnp.maximum(jnp.max(np_ref[...], 0, keepdims=True),
                                    jnp.max(nc_ref[...], 0, keepdims=True)),
                        jnp.max(nn_ref[...], 0, keepdims=True))
    kmax = jnp.broadcast_to(jnp.sqrt(kmax2) * NA_BOUND_SLACK, (2 * GRID_W, LANES))
    row_lo = lax.broadcasted_iota(jnp.int32, (2 * GRID_W, LANES), 0) < GRID_W
    lane = lax.broadcasted_iota(jnp.int32, (2 * GRID_W, LANES), 1)
    kcols = [jnp.sum(jnp.where(lane == jnp.where(row_lo, 2 * hp, 2 * hp + 1), kmax, 0.0), -1, keepdims=True)
             for hp in range(NA_HEADS // 2)]

    def attend(exact):
        bad = jnp.zeros((2 * GRID_W, 1), F32)
        for i in range(NA_ROW_BLOCK):
            r = r0 + i
            start = jnp.clip(r - NA_WIN_H // 2, 0, rows - NA_WIN_H)
            pat = start - r + (NA_WIN_H - 1)
            koff = pl.multiple_of((start - r0 + NA_ROW_BLOCK) * GRID_W, GRID_W)
            qs = slice(i * GRID_W, (i + 1) * GRID_W)
            for hp in range(NA_HEADS // 2):
                cs = slice(hp * LANES, (hp + 1) * LANES)
                qp = q_ref[qs, cs]
                zero = jnp.zeros_like(qp)
                q2 = jnp.concatenate([jnp.where(lane_lo, qp, zero), jnp.where(lane_lo, zero, qp)], axis=0)
                kk = kcat[pl.ds(koff, NA_WIN_H * GRID_W), cs]
                vv = vcat[pl.ds(koff, NA_WIN_H * GRID_W), cs]
                s = _dot_nt(q2, kk) + bias_ref[pat, hp]
                if exact:
                    m = jnp.max(s, -1, keepdims=True)
                else:
                    qf = q2.astype(F32)
                    m = jnp.sqrt(jnp.sum(qf * qf, -1, keepdims=True)) * kcols[hp] + bmax_ref[hp][:, 0:1]
                e = jnp.exp(s - m)
                l = jnp.sum(e, -1, keepdims=True)
                if not exact:
                    bad = bad + jnp.where((l >= NA_L_MIN) & (l <= NA_L_MAX), 0.0, 1.0)
                o2 = _dot(e.astype(BF16), vv) / l
                o_ref[qs, cs] = jnp.where(lane_lo, o2[:GRID_W], o2[GRID_W:]).astype(BF16)
        return bad

    n_bad = jnp.sum(attend(exact=False))

    @pl.when(n_bad > 0.0)
    def _():
        attend(exact=True)


def _natten_call(q, k, v, knorm, bias, bmax, bsz, seq):
    rows = seq // GRID_W
    nrb = rows // NA_ROW_BLOCK
    nt = NA_TOK_BLOCK
    cur = lambda width: pl.BlockSpec((nt, width), lambda b, j: (b * nrb + j, 0))
    prev = lambda width: pl.BlockSpec((nt, width), lambda b, j: (b * nrb + jnp.maximum(j - 1, 0), 0))
    nxt = lambda width: pl.BlockSpec((nt, width), lambda b, j: (b * nrb + jnp.minimum(j + 1, nrb - 1), 0))
    w = NA_WIDTH
    return pl.pallas_call(
        functools.partial(_natten_kernel, rows=rows),
        grid=(bsz, nrb),
        in_specs=[cur(w), prev(w), cur(w), nxt(w), prev(w), cur(w), nxt(w),
                  prev(LANES), cur(LANES), nxt(LANES), _const_spec(bias.shape), _const_spec(bmax.shape)],
        out_specs=cur(w),
        out_shape=jax.ShapeDtypeStruct((bsz * seq, NA_WIDTH), BF16),
        scratch_shapes=[pltpu.VMEM((3 * nt, NA_WIDTH), BF16), pltpu.VMEM((3 * nt, NA_WIDTH), BF16)],
        compiler_params=_params("parallel", "parallel"),
        name="natten",
    )(q, k, k, k, v, v, v, knorm, knorm, knorm, bias, bmax)


def _natten_bias_max(rpb):
    m = jnp.max(rpb.astype(F32), axis=(1, 2)).reshape(NA_HEADS // 2, 2, 1, 1)
    return jnp.broadcast_to(m, (NA_HEADS // 2, 2, GRID_W, LANES)).reshape(NA_HEADS // 2, 2 * GRID_W, LANES)


def _natten_bias(rpb):
    c = jnp.arange(GRID_W)
    col_start = jnp.clip(c - NA_WIN_W // 2, 0, GRID_W - NA_WIN_W)
    in_win = (c[None, :] >= col_start[:, None]) & (c[None, :] < col_start[:, None] + NA_WIN_W)
    dc = jnp.clip(c[None, :] - c[:, None] + (NA_WIN_W - 1), 0, 2 * NA_WIN_W - 2)
    onehot = (dc[:, :, None] == jnp.arange(2 * NA_WIN_W - 1)).astype(F32)
    t = jnp.einsum('hrd,qkd->hrqk', rpb.astype(F32), onehot, precision=lax.Precision.HIGHEST)
    t = jnp.where(in_win[None, None], t, NEG_INF)
    b = jnp.stack([t[:, p:p + NA_WIN_H] for p in range(NA_WIN_H)])
    b = b.transpose(0, 1, 3, 2, 4)
    return b.reshape(NA_WIN_H, NA_HEADS // 2, 2 * GRID_W, NA_WIN_H * GRID_W)


def _mla_online_softmax(q, k_ref, v_ref, tk):
    tq = q.shape[0]

    def body(j, carry):
        m, l, acc = carry
        ks = pl.ds(pl.multiple_of(j * tk, tk), tk)
        s = _dot_nt(q, k_ref[ks, :])
        m_new = jnp.maximum(m, jnp.max(s, -1, keepdims=True))
        alpha = jnp.exp2(m - m_new)
        p = jnp.exp2(s - m_new)
        l = alpha * l + jnp.sum(p, -1, keepdims=True)
        acc = alpha * acc + _dot(p.astype(BF16), v_ref[ks, :])
        return m_new, l, acc

    init = (jnp.full((tq, 1), NEG_INF, F32), jnp.zeros((tq, 1), F32), jnp.zeros((tq, V_DIM), F32))
    _, l, acc = lax.fori_loop(0, k_ref.shape[0] // tk, body, init)
    return acc, l


def _mla_kernel(q_ref, k_ref, v_ref, o_ref, kmax_ref, *, tk, unroll):
    q = q_ref[...]
    tq = q.shape[0]
    seq = k_ref.shape[0]

    @pl.when(pl.program_id(2) == 0)
    def _():
        def body(c, mx):
            kc = k_ref[pl.ds(pl.multiple_of(c * tk, tk), tk), :].astype(F32)
            n2 = jnp.sum(kc * kc, -1, keepdims=True)
            return jnp.maximum(mx, jnp.max(n2, 0, keepdims=True))

        mx = lax.fori_loop(0, seq // tk, body, jnp.zeros((1, 1), F32))
        kmax = jnp.broadcast_to(jnp.sqrt(mx) * MLA_BOUND_SLACK, kmax_ref.shape)
        lane = lax.broadcasted_iota(jnp.int32, kmax_ref.shape, 1)
        kmax_ref[...] = jnp.where(lane == MLA_SHIFT_LANE, kmax, 1.0).astype(BF16)

    q_shift = q * kmax_ref[0:1, :]
    ones = jnp.ones((tk, LANES), BF16)

    def body(j, acc):
        ks = pl.ds(pl.multiple_of(j * tk, tk), tk)
        p = jnp.exp2(_dot_nt(q_shift, k_ref[ks, :])).astype(BF16)
        return acc + _dot(p, jnp.concatenate([v_ref[ks, :], ones], axis=1))

    nk = seq // tk
    acc = lax.fori_loop(0, nk - 1, body, jnp.zeros((tq, V_DIM + LANES), F32), unroll=unroll)
    l_part = acc[:, V_DIM:]
    usable = (l_part >= MLA_L_MIN) & (l_part <= MLA_L_MAX)
    n_bad = jnp.sum(jnp.where(usable, 0.0, 1.0))
    acc = body(nk - 1, acc)
    l = acc[:, V_DIM:]
    o_ref[...] = (acc[:, :V_DIM] / l).astype(BF16)

    @pl.when(n_bad > 0.0)
    def _():
        acc, l = _mla_online_softmax(q, k_ref, v_ref, tk)
        o_ref[...] = (acc / l).astype(BF16)


def _mla_call(qm, km, vm, bsz, seq):
    tq = min(MLA_TQ, seq)
    nq = seq // tq
    tk = min(MLA_TK, seq // 2)
    looped = seq // tk - 1
    unroll = max(u for u in range(1, MLA_MAX_UNROLL + 1) if looped % u == 0)
    return pl.pallas_call(
        functools.partial(_mla_kernel, tk=tk, unroll=unroll),
        grid=(bsz, MLA_HEADS, nq),
        in_specs=[pl.BlockSpec((tq, MLA_HEAD_PAD), lambda b, h, i: (b * nq + i, h)),
                  pl.BlockSpec((seq, MLA_HEAD_PAD), lambda b, h, i: (b, h)),
                  pl.BlockSpec((seq, V_DIM), lambda b, h, i: (b, h))],
        out_specs=pl.BlockSpec((tq, V_DIM), lambda b, h, i: (b * nq + i, h)),
        out_shape=jax.ShapeDtypeStruct((bsz * seq, MLA_WIDTH), BF16),
        scratch_shapes=[pltpu.VMEM((16, MLA_HEAD_PAD), BF16)],
        compiler_params=_params("parallel", "parallel", "arbitrary"),
        name="mla",
    )(qm, km, vm)


def _first_argmax(vals):
    best, idx = vals[0], jnp.zeros(vals[0].shape, jnp.int32)
    for i in range(1, len(vals)):
        better = vals[i] > best
        idx = jnp.where(better, i, idx)
        best = jnp.where(better, vals[i], best)
    return best, idx


def _select(idx, vals):
    out = vals[0]
    for i in range(1, len(vals)):
        out = jnp.where(idx == i, vals[i], out)
    return out


def _merge_kernel(x_ref, oa_ref, ob_ref, oc_ref, wg_ref, wbr_ref, wo_ref, lng_ref, lnb_ref,
                  wrh_ref, wrl_ref, rb_ref, x1_ref, xpk_ref, cls_ref):
    x = x_ref[...]
    xb = x.astype(BF16)
    merged = None
    for i, o_ref in enumerate((oa_ref, ob_ref, oc_ref)):
        g = _sigmoid(_dot(xb, wg_ref[:, i * D_MODEL:(i + 1) * D_MODEL]))
        term = g * _dot(o_ref[...], wbr_ref[i])
        merged = term if merged is None else merged + term
    m = _dot(merged.astype(BF16), wo_ref[...])
    x1 = _layer_norm(DEEPNORM_ALPHA * x + m, lng_ref[...], lnb_ref[...])
    x1_ref[...] = x1

    x1h = x1.astype(BF16)
    x1l = (x1 - x1h.astype(F32)).astype(BF16)
    logits = _dot_nt(wrh_ref[...], x1h) + (_dot_nt(wrl_ref[...], x1h) + _dot_nt(wrh_ref[...], x1l))
    scores = _sigmoid(logits)
    biased = scores + rb_ref[...]
    sc = [scores[e:e + 1, :] for e in range(N_EXPERTS)]
    bi = [biased[e:e + 1, :] for e in range(N_EXPERTS)]
    grp_scores = []
    for g in range(N_GROUPS):
        v = bi[g * EXPERTS_PER_GROUP:(g + 1) * EXPERTS_PER_GROUP]
        top2 = None
        for i in range(EXPERTS_PER_GROUP):
            for k in range(i + 1, EXPERTS_PER_GROUP):
                top2 = v[i] + v[k] if top2 is None else jnp.maximum(top2, v[i] + v[k])
        grp_scores.append(top2)
    _, grp = _first_argmax(grp_scores)
    cand = [_select(grp, [bi[g * EXPERTS_PER_GROUP + i] for g in range(N_GROUPS)])
            for i in range(EXPERTS_PER_GROUP)]
    cand_sc = [_select(grp, [sc[g * EXPERTS_PER_GROUP + i] for g in range(N_GROUPS)])
               for i in range(EXPERTS_PER_GROUP)]
    _, i1 = _first_argmax(cand)
    _, i2 = _first_argmax([jnp.where(i1 == i, -jnp.inf, cand[i]) for i in range(EXPERTS_PER_GROUP)])
    s1 = _select(i1, cand_sc)
    s2 = _select(i2, cand_sc)
    tot = s1 + s2
    g1 = s1 / tot
    g2 = s2 / tot
    first_lower = i1 < i2
    a = jnp.where(first_lower, i1, i2)
    b = jnp.where(first_lower, i2, i1)
    cls_ref[...] = grp * PAIRS_PER_GROUP + (((a * (7 - a)) >> 1) + (b - a - 1))
    xpk_ref[:, :D_MODEL // 2] = _pack_bf16_pairs(x1)
    tm = x.shape[0]
    sub = lax.broadcasted_iota(jnp.int32, (LANES, tm), 0)
    ga = jnp.broadcast_to(jnp.where(first_lower, g1, g2), (LANES, tm))
    gb = jnp.broadcast_to(jnp.where(first_lower, g2, g1), (LANES, tm))
    gmat = jnp.where(sub == 0, ga, jnp.where(sub == 1, gb, 0.0))
    xpk_ref[:, D_MODEL // 2:] = lax.bitcast_convert_type(gmat.T, jnp.int32)


def _merge_call(x2d, oa, ob, oc, w, shared):
    n = x2d.shape[0]
    tm = MERGE_TM
    weights = [w['wg'], w['wbr'], w['wo'], w['ln1_g'], w['ln1_b'], shared['wrh'], shared['wrl'], shared['rb']]
    tok = lambda width: pl.BlockSpec((tm, width), lambda i: (i, 0))
    return pl.pallas_call(
        _merge_kernel,
        grid=(n // tm,),
        in_specs=[tok(D_MODEL), tok(A_WIDTH), tok(NA_WIDTH), tok(MLA_WIDTH)]
                 + [_const_spec(t.shape) for t in weights],
        out_specs=[tok(D_MODEL), tok(ROW_W), pl.BlockSpec((1, tm), lambda i: (0, i))],
        out_shape=[jax.ShapeDtypeStruct((n, D_MODEL), F32),
                   jax.ShapeDtypeStruct((n, ROW_W), jnp.int32),
                   jax.ShapeDtypeStruct((1, n), jnp.int32)],
        compiler_params=_params("parallel"),
        name="merge",
    )(x2d, oa, ob, oc, *weights)


def _row_copy(src, src_row, dst, dst_row, sem):
    return pltpu.make_async_copy(src.at[pl.ds(src_row, 1)], dst.at[pl.ds(dst_row, 1)], sem)


def _dispatch_kernel(pstart_ref, pend_ref, nused_ref, dest_ref, x_ref, spare_hbm, xs_hbm, st_ref, zbuf, zsem, sem):
    i = pl.program_id(0)
    tm = dest_ref.shape[2]
    n_slots = xs_hbm.shape[0]
    n_blocks = n_slots // MOE_BLOCK

    @pl.when(i == 0)
    def _():
        no_owner = pltpu.make_async_copy(spare_hbm, st_ref, zsem)
        no_owner.start()
        no_owner.wait()
        zbuf[...] = jnp.zeros_like(zbuf)

        def zero_block(first_slot):
            dst = xs_hbm.at[pl.ds(pl.multiple_of(first_slot, MOE_BLOCK), MOE_BLOCK)]
            return pltpu.make_async_copy(zbuf, dst, zsem)

        def for_each_zero_block(fn):
            for g in range(N_CLASSES):
                @pl.when(pend_ref[g] > pstart_ref[g])
                def _():
                    fn(zero_block(pend_ref[g] - MOE_BLOCK))

                @pl.when(nused_ref[0] + g < n_blocks)
                def _():
                    fn(zero_block((nused_ref[0] + g) * MOE_BLOCK))

        for_each_zero_block(lambda cp: cp.start())
        for_each_zero_block(lambda cp: cp.wait())

    def wait(j, c):
        _row_copy(x_ref, 0, xs_hbm, 0, sem).wait()
        return c

    for j in range(tm):
        d = dest_ref[0, 0, j]
        _row_copy(x_ref, j, xs_hbm, d, sem).start(priority=j % 2)
        st_ref[d] = i * tm + j
    lax.fori_loop(0, tm, wait, 0, unroll=8)


def _dispatch_call(xpk, dest3, pad_start, pad_end, n_used, n_slots):
    n = xpk.shape[0]
    tm = dest3.shape[2]
    grid_spec = pltpu.PrefetchScalarGridSpec(
        num_scalar_prefetch=3,
        grid=(n // tm,),
        in_specs=[pl.BlockSpec((1, 1, tm), lambda i, ps, pe, nu: (i, 0, 0), memory_space=pltpu.SMEM),
                  pl.BlockSpec((tm, ROW_W), lambda i, ps, pe, nu: (i, 0)),
                  pl.BlockSpec(memory_space=pl.ANY)],
        out_specs=[pl.BlockSpec(memory_space=pl.ANY), pl.BlockSpec(memory_space=pltpu.SMEM)],
        scratch_shapes=[pltpu.VMEM((MOE_BLOCK, ROW_W), jnp.int32), pltpu.SemaphoreType.DMA(()),
                        pltpu.SemaphoreType.DMA(())],
    )
    return pl.pallas_call(
        _dispatch_kernel,
        grid_spec=grid_spec,
        out_shape=[jax.ShapeDtypeStruct((n_slots, ROW_W), jnp.int32),
                   jax.ShapeDtypeStruct((n_slots,), jnp.int32)],
        compiler_params=_params("arbitrary"),
        name="dispatch",
    )(pad_start, pad_end, n_used, dest3, xpk, n + jnp.arange(n_slots, dtype=jnp.int32) % MOE_BLOCK)


def _experts_kernel(ba_ref, bb_ref, nused_ref, x_ref, st_prev_ref, st_cur_ref, wgu_a_ref, wgu_b_ref,
                    wd_a_ref, wd_b_ref, f_hbm, ybuf, sems, zsem):
    b = pl.program_id(0)
    last = pl.num_programs(0) - 1

    def scatter(st_ref, slot):
        for j in range(MOE_BLOCK):
            _row_copy(ybuf.at[slot], j, f_hbm, st_ref[0, 0, j], sems.at[slot]).start(priority=j % 2)

    def drain(slot):
        def wait(j, c):
            _row_copy(ybuf.at[slot], 0, f_hbm, 0, sems.at[slot]).wait()
            return c
        lax.fori_loop(0, MOE_BLOCK, wait, 0, unroll=8)

    def block_result():
        lo, hi = _unpack_bf16_pairs(x_ref[:, :D_MODEL // 2])
        lo, hi = lo.astype(BF16), hi.astype(BF16)
        gates = lax.bitcast_convert_type(x_ref[:, D_MODEL // 2:], F32)
        y = None
        for lane, (wgu_ref, wd_ref) in enumerate(((wgu_a_ref, wd_a_ref), (wgu_b_ref, wd_b_ref))):
            gu = _dot(lo, wgu_ref[0, :D_MODEL // 2, :]) + _dot(hi, wgu_ref[0, D_MODEL // 2:, :])
            gt = gu[:, :D_EXPERT]
            hid = gt * _sigmoid(gt) * gu[:, D_EXPERT:] * gates[:, lane:lane + 1]
            term = _dot(hid.astype(BF16), wd_ref[0])
            y = term if y is None else y + term
        return _pack_bf16_pairs(y)

    def step(slot):
        @pl.when(b == 0)
        def _():
            ybuf[1 - slot] = jnp.zeros(ybuf.shape[1:], ybuf.dtype)
            spare = f_hbm.at[pl.ds(f_hbm.shape[0] - MOE_BLOCK, MOE_BLOCK)]
            zero_spare = pltpu.make_async_copy(ybuf.at[1 - slot], spare, zsem)
            zero_spare.start()
            zero_spare.wait()
            ybuf[slot] = block_result()

        @pl.when(b >= 2)
        def _():
            drain(slot)

        @pl.when((b >= 1) & (b < nused_ref[0]))
        def _():
            scatter(st_prev_ref, 1 - slot)
            ybuf[slot] = block_result()

        @pl.when((b >= 1) & (b >= nused_ref[0]))
        def _():
            scatter(st_prev_ref, 1 - slot)
            ybuf[slot] = jnp.zeros(ybuf.shape[1:], ybuf.dtype)

        @pl.when(b == last)
        def _():
            @pl.when(b >= 1)
            def _():
                drain(1 - slot)

            scatter(st_cur_ref, slot)
            drain(slot)

    for slot in range(2):
        @pl.when(b % 2 == slot)
        def _():
            step(slot)


def _experts_call(xs, slot_tok, block_a, block_b, n_used, n_tok, w):
    n_blocks = block_a.shape[0]
    wgu_block = (1, D_MODEL, 2 * D_EXPERT)
    wd_block = (1, D_EXPERT, D_MODEL)
    st3 = slot_tok.reshape(n_blocks, 1, MOE_BLOCK)
    grid_spec = pltpu.PrefetchScalarGridSpec(
        num_scalar_prefetch=3,
        grid=(n_blocks,),
        in_specs=[pl.BlockSpec((MOE_BLOCK, ROW_W), lambda b, ba, bb, bv: (b, 0)),
                  pl.BlockSpec((1, 1, MOE_BLOCK), lambda b, ba, bb, bv: (jnp.maximum(b - 1, 0), 0, 0),
                               memory_space=pltpu.SMEM),
                  pl.BlockSpec((1, 1, MOE_BLOCK), lambda b, ba, bb, bv: (b, 0, 0), memory_space=pltpu.SMEM),
                  pl.BlockSpec(wgu_block, lambda b, ba, bb, bv: (ba[b], 0, 0)),
                  pl.BlockSpec(wgu_block, lambda b, ba, bb, bv: (bb[b], 0, 0)),
                  pl.BlockSpec(wd_block, lambda b, ba, bb, bv: (ba[b], 0, 0)),
                  pl.BlockSpec(wd_block, lambda b, ba, bb, bv: (bb[b], 0, 0))],
        out_specs=pl.BlockSpec(memory_space=pl.ANY),
        scratch_shapes=[pltpu.VMEM((2, MOE_BLOCK, D_MODEL // 2), jnp.int32), pltpu.SemaphoreType.DMA((2,)),
                        pltpu.SemaphoreType.DMA(())],
    )
    return pl.pallas_call(
        _experts_kernel,
        grid_spec=grid_spec,
        out_shape=jax.ShapeDtypeStruct((n_tok + MOE_BLOCK, D_MODEL // 2), jnp.int32),
        compiler_params=_params("arbitrary"),
        name="experts",
    )(block_a, block_b, n_used, xs, st3, st3, w['wgu'], w['wgu'], w['wd'], w['wd'])


def _combine_kernel(x1_ref, f_ref, lng_ref, lnb_ref, o_ref):
    f = jnp.concatenate(_unpack_bf16_pairs(f_ref[...]), axis=1)
    o_ref[...] = _layer_norm(DEEPNORM_ALPHA * x1_ref[...] + f, lng_ref[...], lnb_ref[...])


def _combine_call(x1, f, w):
    n = x1.shape[0]
    tm = COMBINE_TM
    return pl.pallas_call(
        _combine_kernel,
        grid=(n // tm,),
        in_specs=[pl.BlockSpec((tm, D_MODEL), lambda i: (i, 0)),
                  pl.BlockSpec((tm, D_MODEL // 2), lambda i: (i, 0)),
                  _const_spec(w['ln2_g'].shape), _const_spec(w['ln2_b'].shape)],
        out_specs=pl.BlockSpec((tm, D_MODEL), lambda i: (i, 0)),
        out_shape=jax.ShapeDtypeStruct((n, D_MODEL), F32),
        compiler_params=_params("parallel"),
        name="combine",
    )(x1, f, w['ln2_g'], w['ln2_b'])


def _class_experts():
    pairs = [(a, b) for a in range(EXPERTS_PER_GROUP) for b in range(a + 1, EXPERTS_PER_GROUP)]
    ea = [g * EXPERTS_PER_GROUP + a for g in range(N_GROUPS) for a, _ in pairs]
    eb = [g * EXPERTS_PER_GROUP + b for g in range(N_GROUPS) for _, b in pairs]
    return jnp.asarray(ea, jnp.int32), jnp.asarray(eb, jnp.int32)


def _dispatch_plan(cls):
    n = cls.shape[1]
    n_blocks = -(-n // MOE_BLOCK) + N_CLASSES
    onehot = (cls[0][:, None] == jnp.arange(N_CLASSES, dtype=jnp.int32)).astype(jnp.int32)
    incl = jnp.cumsum(onehot, axis=0)
    counts = incl[-1]
    padded = (counts + MOE_BLOCK - 1) // MOE_BLOCK * MOE_BLOCK
    pad_end = jnp.cumsum(padded).astype(jnp.int32)
    pad_start = pad_end - padded
    slot = pad_start[None, :] + (incl - onehot)
    dest = jnp.sum(onehot * slot, axis=1).astype(jnp.int32)
    dest3 = dest.reshape(n // ROUTE_TM, 1, ROUTE_TM)
    first_slot = jnp.arange(n_blocks, dtype=jnp.int32) * MOE_BLOCK
    block_c = jnp.minimum(jnp.sum(pad_end[None, :] <= first_slot[:, None], axis=1), N_CLASSES - 1)
    ea, eb = _class_experts()
    n_used = pad_end[-1:] // MOE_BLOCK
    return dest3, pad_start, pad_end, ea[block_c], eb[block_c], n_used, n_blocks * MOE_BLOCK


def _rope_tables(seq):
    inv = 1.0 / (ROPE_BASE ** (jnp.arange(0, QK_ROPE, 2, dtype=F32) / QK_ROPE))
    ang = jnp.arange(seq, dtype=F32)[:, None] * inv[None, :]
    pad = jnp.zeros((seq, LANES - QK_ROPE), F32)
    cos, sin = jnp.cos(ang), jnp.sin(ang)
    return jnp.concatenate([cos, cos, pad], -1), jnp.concatenate([sin, sin, pad], -1)


def _pad_cols(t, width):
    return jnp.pad(t, ((0, 0), (0, width - t.shape[1])))


def _rope_weight_pair(w_rope):
    half = QK_ROPE // 2
    swapped = jnp.concatenate([-w_rope[:, half:], w_rope[:, :half]], axis=1)
    return _pad_cols(w_rope, LANES), _pad_cols(swapped, LANES)


def _layer_weights(l, w_in, a_ln_g, a_ln_b, a_ws, a_bs, na_rpb, mla_q_norm, mla_kv_norm, mla_w_uq,
                   mla_w_ukv, w_br_a, w_br_b, w_br_c, w_o, ln1_g, ln1_b, ln2_g, ln2_b, w_gate, w_up,
                   w_down):
    sizes = (2 * A_WIDTH, NA_WIDTH, NA_WIDTH, NA_WIDTH, Q_LORA, KV_LORA, QK_ROPE, N_BRANCH * D_MODEL)
    cuts = np.cumsum(sizes)[:-1].tolist()
    wa, wq, wk, wv, wcq, wckv, wkr, wg = jnp.split(w_in[l], cuts, axis=-1)
    row = lambda t: t.reshape(1, -1).astype(F32)
    uq = mla_w_uq[l].reshape(Q_LORA, MLA_HEADS, QK_NOPE + QK_ROPE)
    wqn = uq[:, :, :QK_NOPE].reshape(Q_LORA, MLA_HEADS * QK_NOPE)
    rope_pairs = [_rope_weight_pair(uq[:, h, QK_NOPE:]) for h in range(MLA_HEADS)]
    wqr = jnp.concatenate([p[0] for p in rope_pairs], axis=1)
    wqs = jnp.concatenate([p[1] for p in rope_pairs], axis=1)
    ukv = mla_w_ukv[l].reshape(KV_LORA, MLA_HEADS, QK_NOPE + V_DIM)
    wukv = jnp.concatenate([ukv[:, :, :QK_NOPE].reshape(KV_LORA, -1), ukv[:, :, QK_NOPE:].reshape(KV_LORA, -1)], 1)
    wkr_p, wkrs_p = _rope_weight_pair(wkr)
    return {
        'wa': wa.astype(BF16), 'a_ln_g': row(a_ln_g[l]), 'a_ln_b': row(a_ln_b[l]),
        'ws': a_ws[l].astype(BF16),
        'bs': jnp.broadcast_to(a_bs[l][:, :, None], (A_GROUPS, CHUNK, CHUNK)).astype(F32),
        'wqkv': jnp.concatenate([wq * NA_SCALE, wk, wv], axis=1).astype(BF16),
        'wcq': wcq.astype(BF16), 'q_norm': row(mla_q_norm[l]),
        'wqn': wqn.astype(BF16), 'wqr': wqr.astype(BF16), 'wqs': wqs.astype(BF16),
        'wckv': wckv.astype(BF16), 'kv_norm': row(mla_kv_norm[l]), 'wukv': wukv.astype(BF16),
        'wkr': wkr_p.astype(BF16), 'wkrs': wkrs_p.astype(BF16),
        'na_bias': _natten_bias(na_rpb[l]), 'na_bmax': _natten_bias_max(na_rpb[l]),
        'wg': wg.astype(BF16),
        'wbr': jnp.stack([w_br_a[l], w_br_b[l], w_br_c[l]]).astype(BF16),
        'wo': w_o[l].astype(BF16), 'ln1_g': row(ln1_g[l]), 'ln1_b': row(ln1_b[l]),
        'ln2_g': row(ln2_g[l]), 'ln2_b': row(ln2_b[l]),
        'wgu': jnp.concatenate([w_gate[l], w_up[l]], axis=-1).astype(BF16),
        'wd': w_down[l].astype(BF16),
    }


def _shared_weights(w_router, router_bias):
    wr_t = w_router.T.astype(F32)
    wrh = wr_t.astype(BF16)
    wrl = (wr_t - wrh.astype(F32)).astype(BF16)
    rb = jnp.broadcast_to(router_bias.astype(F32)[:, None], (N_EXPERTS, MERGE_TM))
    return {'wrh': wrh, 'wrl': wrl, 'rb': rb}


def _trunk(x, layers, shared):
    bsz, seq, _ = x.shape
    cos2, sin2 = _rope_tables(seq)
    x2d = x.reshape(bsz * seq, D_MODEL)
    for w in layers:
        oa, q, k, v, qm, km, vm, kn2 = _pre_call(x2d, seq, cos2, sin2, w)
        ob = _natten_call(q, k, v, kn2, w['na_bias'], w['na_bmax'], bsz, seq)
        oc = _mla_call(qm, km, vm, bsz, seq)
        x1, xpk, cls = _merge_call(x2d, oa, ob, oc, w, shared)
        dest3, pad_start, pad_end, block_a, block_b, n_used, n_slots = _dispatch_plan(cls)
        xs, slot_tok = _dispatch_call(xpk, dest3, pad_start, pad_end, n_used, n_slots)
        f = _experts_call(xs, slot_tok, block_a, block_b, n_used, x1.shape[0], w)
        x2d = _combine_call(x1, f, w)
    return x2d.reshape(bsz, seq, D_MODEL)


def kernel(x_prompt, x_sample, w_in, a_ln_g, a_ln_b, a_ws, a_bs, na_rpb, mla_q_norm, mla_kv_norm, mla_w_uq, mla_w_ukv, w_br_a, w_br_b, w_br_c, w_o, ln1_g, ln1_b, ln2_g, ln2_b, w_router, router_bias, w_gate, w_up, w_down):
    layers = [_layer_weights(l, w_in, a_ln_g, a_ln_b, a_ws, a_bs, na_rpb, mla_q_norm, mla_kv_norm,
                             mla_w_uq, mla_w_ukv, w_br_a, w_br_b, w_br_c, w_o, ln1_g, ln1_b, ln2_g,
                             ln2_b, w_gate, w_up, w_down) for l in range(DEPTH)]
    shared = _shared_weights(w_router, router_bias)
    return (_trunk(x_prompt, layers, shared), _trunk(x_sample, layers, shared))
```

```python
import functools
import math

import jax
import jax.numpy as jnp
import numpy as np
from jax import lax
from jax.experimental import pallas as pl
from jax.experimental.pallas import tpu as pltpu

F32 = jnp.float32
BF16 = jnp.bfloat16

D_MODEL = 1024
DEPTH = 2
GRID_W = 64
CHUNK = 128
A_WIDTH = 512
A_GROUPS = 4
NA_HEADS = 8
NA_HEAD_DIM = 64
NA_WIN_H = 8
NA_WIN_W = 16
NA_WIDTH = NA_HEADS * NA_HEAD_DIM
NA_SCALE = NA_HEAD_DIM ** -0.5
MLA_HEADS = 4
Q_LORA = 384
KV_LORA = 256
QK_NOPE = 128
QK_ROPE = 64
V_DIM = 128
MLA_WIDTH = MLA_HEADS * V_DIM
MLA_SCALE = (QK_NOPE + QK_ROPE) ** -0.5
ROPE_BASE = 10000.0
N_BRANCH = 3
N_EXPERTS = 16
N_GROUPS = 4
EXPERTS_PER_GROUP = N_EXPERTS // N_GROUPS
TOP_K = 2
D_EXPERT = 256
MOE_BLOCK = 256
DEEPNORM_ALPHA = (2 * DEPTH) ** 0.25
LN_EPS = 1e-5
RMS_EPS = 1e-6
NEG_INF = -1e30
LOG2E = math.log2(math.e)

VMEM_LIMIT_BYTES = 56 * 1024 * 1024
LANES = 128
MLA_HEAD_PAD = 2 * LANES
NA_ROW_BLOCK = 8
NA_TOK_BLOCK = NA_ROW_BLOCK * GRID_W

ROW_W = D_MODEL // 2 + LANES
PAIRS_PER_GROUP = EXPERTS_PER_GROUP * (EXPERTS_PER_GROUP - 1) // 2
N_CLASSES = N_GROUPS * PAIRS_PER_GROUP
HIGH_HALF = -65536
PRE_TM = 512
MERGE_TM = 512
ROUTE_TM = 256
COMBINE_TM = 512
MLA_TQ = 1024
MLA_TK = 1024
MLA_MAX_UNROLL = 5
MLA_SHIFT_LANE = QK_NOPE + QK_ROPE
MLA_L_MIN = 2.0 ** -60
MLA_L_MAX = 2.0 ** 100
MLA_BOUND_SLACK = 1.0 + 2.0 ** -7
NA_BOUND_SLACK = 1.0 + 2.0 ** -6
NA_L_MIN = 2.0 ** -60
NA_L_MAX = 2.0 ** 100


def _params(*sem):
    return pltpu.CompilerParams(dimension_semantics=sem, vmem_limit_bytes=VMEM_LIMIT_BYTES)


def _const_spec(shape):
    nd = len(shape)
    return pl.BlockSpec(shape, lambda *_: (0,) * nd)


def _layer_norm(y, g, b):
    mu = jnp.mean(y, -1, keepdims=True)
    yc = y - mu
    var = jnp.mean(yc * yc, -1, keepdims=True)
    return yc * lax.rsqrt(var + LN_EPS) * g + b


def _rms_norm(y, g):
    return y * lax.rsqrt(jnp.mean(y * y, -1, keepdims=True) + RMS_EPS) * g


def _gelu_tanh(x):
    return 0.5 * x * (1.0 + jnp.tanh(math.sqrt(2.0 / math.pi) * (x + 0.044715 * (x * x * x))))


def _sigmoid(x):
    return 1.0 / (1.0 + jnp.exp(-x))


def _pack_bf16_pairs(y):
    w = y.shape[1] // 2
    lo = lax.bitcast_convert_type(y[:, :w].astype(BF16).astype(F32), jnp.int32)
    hi = lax.bitcast_convert_type(y[:, w:].astype(BF16).astype(F32), jnp.int32)
    return lax.shift_right_logical(lo, 16) | (hi & HIGH_HALF)


def _unpack_bf16_pairs(p):
    lo = lax.bitcast_convert_type(lax.shift_left(p, 16), F32)
    hi = lax.bitcast_convert_type(p & HIGH_HALF, F32)
    return lo, hi


def _dot(a, b):
    return jnp.dot(a, b, preferred_element_type=F32)


def _dot_nt(a, b):
    return lax.dot_general(a, b, (((1,), (1,)), ((), ())), preferred_element_type=F32)


def _pre_kernel(*refs, after_moe):
    if after_moe:
        x1_ref, f_ref, ln2g_ref, ln2b_ref = refs[:4]
        refs, x2_ref = refs[4:-1], refs[-1]
        f = jnp.concatenate(_unpack_bf16_pairs(f_ref[...]), axis=1)
        x = _layer_norm(DEEPNORM_ALPHA * x1_ref[...] + f, ln2g_ref[...], ln2b_ref[...])
        x2_ref[...] = x
    else:
        x, refs = refs[0][...], refs[1:]
    (cos_ref, sin_ref, wa_ref, lng_ref, lnb_ref, ws_ref, bs_ref, wqkv_ref, wcq_ref, qnorm_ref, wqn_ref,
     wqr_ref, wqs_ref, wckv_ref, kvnorm_ref, wukv_ref, wkr_ref, wkrs_ref, hsum_ref,
     oa_ref, q_ref, k_ref, v_ref, qm_ref, km_ref, vm_ref, kn2_ref) = refs
    tm = x.shape[0]
    xb = x.astype(BF16)
    cos2 = cos_ref[...]
    sin2 = sin_ref[...]

    a = _gelu_tanh(_dot(xb, wa_ref[...]))
    u = a[:, :A_WIDTH]
    vn = _layer_norm(a[:, A_WIDTH:], lng_ref[...], lnb_ref[...]).astype(BF16)
    for c in range(tm // CHUNK):
        rs = slice(c * CHUNK, (c + 1) * CHUNK)
        for g in range(A_GROUPS):
            cs = slice(g * CHUNK, (g + 1) * CHUNK)
            sp = _dot(ws_ref[g], vn[rs, cs]) + bs_ref[g]
            oa_ref[rs, cs] = (u[rs, cs] * sp).astype(BF16)

    qkv = _dot(xb, wqkv_ref[...])
    q_ref[...] = qkv[:, :NA_WIDTH].astype(BF16)
    kb = qkv[:, NA_WIDTH:2 * NA_WIDTH].astype(BF16)
    k_ref[...] = kb
    v_ref[...] = qkv[:, 2 * NA_WIDTH:].astype(BF16)
    kf = kb.astype(F32)
    kn2_ref[...] = _dot((kf * kf).astype(BF16), hsum_ref[...])

    cqn = _rms_norm(_dot(xb, wcq_ref[...]), qnorm_ref[...]).astype(BF16)
    qn = _dot(cqn, wqn_ref[...])
    qr = _dot(cqn, wqr_ref[...])
    qs = _dot(cqn, wqs_ref[...])
    qscale = MLA_SCALE * LOG2E
    shift_lane = lax.broadcasted_iota(jnp.int32, (tm, LANES), 1) == MLA_SHIFT_LANE - QK_NOPE
    for h in range(MLA_HEADS):
        hs = slice(h * LANES, (h + 1) * LANES)
        nope = (qn[:, hs] * qscale).astype(BF16)
        rot = ((qr[:, hs] * cos2 + qs[:, hs] * sin2) * qscale).astype(BF16)
        nf, rf = nope.astype(F32), rot.astype(F32)
        norm = jnp.sqrt(jnp.sum(nf * nf, -1, keepdims=True) + jnp.sum(rf * rf, -1, keepdims=True))
        norm = jnp.broadcast_to(norm * MLA_BOUND_SLACK, rot.shape).astype(BF16)
        qm_ref[:, h * MLA_HEAD_PAD:h * MLA_HEAD_PAD + LANES] = nope
        qm_ref[:, h * MLA_HEAD_PAD + LANES:(h + 1) * MLA_HEAD_PAD] = jnp.where(shift_lane, norm, rot)

    ckvn = _rms_norm(_dot(xb, wckv_ref[...]), kvnorm_ref[...]).astype(BF16)
    kv = _dot(ckvn, wukv_ref[...])
    krope = _dot(xb, wkr_ref[...]) * cos2 + _dot(xb, wkrs_ref[...]) * sin2
    lane = lax.broadcasted_iota(jnp.int32, krope.shape, 1)
    krope = jnp.where(lane == MLA_SHIFT_LANE - QK_NOPE, -1.0, krope).astype(BF16)
    for h in range(MLA_HEADS):
        km_ref[:, h * MLA_HEAD_PAD:h * MLA_HEAD_PAD + LANES] = kv[:, h * LANES:(h + 1) * LANES].astype(BF16)
        km_ref[:, h * MLA_HEAD_PAD + LANES:(h + 1) * MLA_HEAD_PAD] = krope
    vm_ref[...] = kv[:, MLA_HEADS * QK_NOPE:].astype(BF16)


def _head_sum_matrix():
    feat = np.arange(NA_WIDTH)[:, None] // NA_HEAD_DIM
    return jnp.asarray(feat == np.arange(LANES)[None, :], BF16)


def _pre_call(x2d, seq, cos2, sin2, w, moe=None):
    n = x2d.shape[0] if moe is None else moe[0].shape[0]
    tm = PRE_TM
    per_seq = seq // tm
    weights = [w['wa'], w['a_ln_g'], w['a_ln_b'], w['ws'], w['bs'], w['wqkv'], w['wcq'], w['q_norm'],
               w['wqn'], w['wqr'], w['wqs'], w['wckv'], w['kv_norm'], w['wukv'], w['wkr'], w['wkrs'],
               _head_sum_matrix()]
    tok = lambda width: pl.BlockSpec((tm, width), lambda i: (i, 0))
    pos = pl.BlockSpec((tm, LANES), lambda i: (i % per_seq, 0))
    out_widths = [A_WIDTH, NA_WIDTH, NA_WIDTH, NA_WIDTH, MLA_HEADS * MLA_HEAD_PAD,
                  MLA_HEADS * MLA_HEAD_PAD, MLA_WIDTH]
    out_specs = [tok(wd) for wd in out_widths] + [tok(LANES)]
    out_shape = ([jax.ShapeDtypeStruct((n, wd), BF16) for wd in out_widths]
                 + [jax.ShapeDtypeStruct((n, LANES), F32)])
    if moe is None:
        stream, stream_specs = [x2d], [tok(D_MODEL)]
    else:
        x1, f, prev = moe
        stream = [x1, f, prev['ln2_g'], prev['ln2_b']]
        stream_specs = [tok(D_MODEL), tok(D_MODEL // 2), _const_spec(prev['ln2_g'].shape),
                        _const_spec(prev['ln2_b'].shape)]
        out_specs.append(tok(D_MODEL))
        out_shape.append(jax.ShapeDtypeStruct((n, D_MODEL), F32))
    return pl.pallas_call(
        functools.partial(_pre_kernel, after_moe=moe is not None),
        grid=(n // tm,),
        in_specs=stream_specs + [pos, pos] + [_const_spec(t.shape) for t in weights],
        out_specs=out_specs,
        out_shape=out_shape,
        compiler_params=_params("parallel"),
        name="pre",
    )(*stream, cos2, sin2, *weights)


def _natten_kernel(q_ref, kp_ref, kc_ref, kn_ref, vp_ref, vc_ref, vn_ref, np_ref, nc_ref, nn_ref,
                   bias_ref, bmax_ref, o_ref, kcat, vcat, *, rows):
    j = pl.program_id(1)
    nt = NA_TOK_BLOCK
    kcat[0:nt] = kp_ref[...]
    kcat[nt:2 * nt] = kc_ref[...]
    kcat[2 * nt:3 * nt] = kn_ref[...]
    vcat[0:nt] = vp_ref[...]
    vcat[nt:2 * nt] = vc_ref[...]
    vcat[2 * nt:3 * nt] = vn_ref[...]
    r0 = j * NA_ROW_BLOCK
    lane_lo = lax.broadcasted_iota(jnp.int32, (GRID_W, LANES), 1) < NA_HEAD_DIM

    kmax2 = jnp.maximum(jnp.maximum(jnp.max(np_ref[...], 0, keepdims=True),
                                    jnp.max(nc_ref[...], 0, keepdims=True)),
                        jnp.max(nn_ref[...], 0, keepdims=True))
    kmax = jnp.broadcast_to(jnp.sqrt(kmax2) * NA_BOUND_SLACK, (2 * GRID_W, LANES))
    row_lo = lax.broadcasted_iota(jnp.int32, (2 * GRID_W, LANES), 0) < GRID_W
    lane = lax.broadcasted_iota(jnp.int32, (2 * GRID_W, LANES), 1)
    kcols = [jnp.sum(jnp.where(lane == jnp.where(row_lo, 2 * hp, 2 * hp + 1), kmax, 0.0), -1, keepdims=True)
             for hp in range(NA_HEADS // 2)]

    def attend(exact):
        bad = jnp.zeros((2 * GRID_W, 1), F32)
        for i in range(NA_ROW_BLOCK):
            r = r0 + i
            start = jnp.clip(r - NA_WIN_H // 2, 0, rows - NA_WIN_H)
            pat = start - r + (NA_WIN_H - 1)
            koff = pl.multiple_of((start - r0 + NA_ROW_BLOCK) * GRID_W, GRID_W)
            qs = slice(i * GRID_W, (i + 1) * GRID_W)
            for hp in range(NA_HEADS // 2):
                cs = slice(hp * LANES, (hp + 1) * LANES)
                qp = q_ref[qs, cs]
                zero = jnp.zeros_like(qp)
                q2 = jnp.concatenate([jnp.where(lane_lo, qp, zero), jnp.where(lane_lo, zero, qp)], axis=0)
                kk = kcat[pl.ds(koff, NA_WIN_H * GRID_W), cs]
                vv = vcat[pl.ds(koff, NA_WIN_H * GRID_W), cs]
                s = _dot_nt(q2, kk) + bias_ref[pat, hp]
                if exact:
                    m = jnp.max(s, -1, keepdims=True)
                else:
                    qf = q2.astype(F32)
                    m = jnp.sqrt(jnp.sum(qf * qf, -1, keepdims=True)) * kcols[hp] + bmax_ref[hp][:, 0:1]
                e = jnp.exp(s - m)
                l = jnp.sum(e, -1, keepdims=True)
                if not exact:
                    bad = bad + jnp.where((l >= NA_L_MIN) & (l <= NA_L_MAX), 0.0, 1.0)
                o2 = _dot(e.astype(BF16), vv) / l
                o_ref[qs, cs] = jnp.where(lane_lo, o2[:GRID_W], o2[GRID_W:]).astype(BF16)
        return bad

    n_bad = jnp.sum(attend(exact=False))

    @pl.when(n_bad > 0.0)
    def _():
        attend(exact=True)


def _natten_call(q, k, v, knorm, bias, bmax, bsz, seq):
    rows = seq // GRID_W
    nrb = rows // NA_ROW_BLOCK
    nt = NA_TOK_BLOCK
    cur = lambda width: pl.BlockSpec((nt, width), lambda b, j: (b * nrb + j, 0))
    prev = lambda width: pl.BlockSpec((nt, width), lambda b, j: (b * nrb + jnp.maximum(j - 1, 0), 0))
    nxt = lambda width: pl.BlockSpec((nt, width), lambda b, j: (b * nrb + jnp.minimum(j + 1, nrb - 1), 0))
    w = NA_WIDTH
    return pl.pallas_call(
        functools.partial(_natten_kernel, rows=rows),
        grid=(bsz, nrb),
        in_specs=[cur(w), prev(w), cur(w), nxt(w), prev(w), cur(w), nxt(w),
                  prev(LANES), cur(LANES), nxt(LANES), _const_spec(bias.shape), _const_spec(bmax.shape)],
        out_specs=cur(w),
        out_shape=jax.ShapeDtypeStruct((bsz * seq, NA_WIDTH), BF16),
        scratch_shapes=[pltpu.VMEM((3 * nt, NA_WIDTH), BF16), pltpu.VMEM((3 * nt, NA_WIDTH), BF16)],
        compiler_params=_params("parallel", "parallel"),
        name="natten",
    )(q, k, k, k, v, v, v, knorm, knorm, knorm, bias, bmax)


def _natten_bias_max(rpb):
    m = jnp.max(rpb.astype(F32), axis=(1, 2)).reshape(NA_HEADS // 2, 2, 1, 1)
    return jnp.broadcast_to(m, (NA_HEADS // 2, 2, GRID_W, LANES)).reshape(NA_HEADS // 2, 2 * GRID_W, LANES)


def _natten_bias(rpb):
    c = jnp.arange(GRID_W)
    col_start = jnp.clip(c - NA_WIN_W // 2, 0, GRID_W - NA_WIN_W)
    in_win = (c[None, :] >= col_start[:, None]) & (c[None, :] < col_start[:, None] + NA_WIN_W)
    dc = jnp.clip(c[None, :] - c[:, None] + (NA_WIN_W - 1), 0, 2 * NA_WIN_W - 2)
    onehot = (dc[:, :, None] == jnp.arange(2 * NA_WIN_W - 1)).astype(F32)
    t = jnp.einsum('hrd,qkd->hrqk', rpb.astype(F32), onehot, precision=lax.Precision.HIGHEST)
    t = jnp.where(in_win[None, None], t, NEG_INF)
    b = jnp.stack([t[:, p:p + NA_WIN_H] for p in range(NA_WIN_H)])
    b = b.transpose(0, 1, 3, 2, 4)
    return b.reshape(NA_WIN_H, NA_HEADS // 2, 2 * GRID_W, NA_WIN_H * GRID_W)


def _mla_online_softmax(q, k_ref, v_ref, tk):
    tq = q.shape[0]

    def body(j, carry):
        m, l, acc = carry
        ks = pl.ds(pl.multiple_of(j * tk, tk), tk)
        s = _dot_nt(q, k_ref[ks, :])
        m_new = jnp.maximum(m, jnp.max(s, -1, keepdims=True))
        alpha = jnp.exp2(m - m_new)
        p = jnp.exp2(s - m_new)
        l = alpha * l + jnp.sum(p, -1, keepdims=True)
        acc = alpha * acc + _dot(p.astype(BF16), v_ref[ks, :])
        return m_new, l, acc

    init = (jnp.full((tq, 1), NEG_INF, F32), jnp.zeros((tq, 1), F32), jnp.zeros((tq, V_DIM), F32))
    _, l, acc = lax.fori_loop(0, k_ref.shape[0] // tk, body, init)
    return acc, l


def _mla_kernel(q_ref, k_ref, v_ref, o_ref, kmax_ref, *, tk, unroll):
    q = q_ref[...]
    tq = q.shape[0]
    seq = k_ref.shape[0]

    @pl.when(pl.program_id(2) == 0)
    def _():
        def body(c, mx):
            kc = k_ref[pl.ds(pl.multiple_of(c * tk, tk), tk), :].astype(F32)
            n2 = jnp.sum(kc * kc, -1, keepdims=True)
            return jnp.maximum(mx, jnp.max(n2, 0, keepdims=True))

        mx = lax.fori_loop(0, seq // tk, body, jnp.zeros((1, 1), F32))
        kmax = jnp.broadcast_to(jnp.sqrt(mx) * MLA_BOUND_SLACK, kmax_ref.shape)
        lane = lax.broadcasted_iota(jnp.int32, kmax_ref.shape, 1)
        kmax_ref[...] = jnp.where(lane == MLA_SHIFT_LANE, kmax, 1.0).astype(BF16)

    q_shift = q * kmax_ref[0:1, :]
    ones = jnp.ones((tk, LANES), BF16)

    def body(j, acc):
        ks = pl.ds(pl.multiple_of(j * tk, tk), tk)
        p = jnp.exp2(_dot_nt(q_shift, k_ref[ks, :])).astype(BF16)
        return acc + _dot(p, jnp.concatenate([v_ref[ks, :], ones], axis=1))

    nk = seq // tk
    acc = lax.fori_loop(0, nk - 1, body, jnp.zeros((tq, V_DIM + LANES), F32), unroll=unroll)
    l_part = acc[:, V_DIM:]
    usable = (l_part >= MLA_L_MIN) & (l_part <= MLA_L_MAX)
    n_bad = jnp.sum(jnp.where(usable, 0.0, 1.0))
    acc = body(nk - 1, acc)
    l = acc[:, V_DIM:]
    o_ref[...] = (acc[:, :V_DIM] / l).astype(BF16)

    @pl.when(n_bad > 0.0)
    def _():
        acc, l = _mla_online_softmax(q, k_ref, v_ref, tk)
        o_ref[...] = (acc / l).astype(BF16)


def _mla_call(qm, km, vm, bsz, seq):
    tq = min(MLA_TQ, seq)
    nq = seq // tq
    tk = min(MLA_TK, seq // 2)
    looped = seq // tk - 1
    unroll = max(u for u in range(1, MLA_MAX_UNROLL + 1) if looped % u == 0)
    return pl.pallas_call(
        functools.partial(_mla_kernel, tk=tk, unroll=unroll),
        grid=(bsz, MLA_HEADS, nq),
        in_specs=[pl.BlockSpec((tq, MLA_HEAD_PAD), lambda b, h, i: (b * nq + i, h)),
                  pl.BlockSpec((seq, MLA_HEAD_PAD), lambda b, h, i: (b, h)),
                  pl.BlockSpec((seq, V_DIM), lambda b, h, i: (b, h))],
        out_specs=pl.BlockSpec((tq, V_DIM), lambda b, h, i: (b * nq + i, h)),
        out_shape=jax.ShapeDtypeStruct((bsz * seq, MLA_WIDTH), BF16),
        scratch_shapes=[pltpu.VMEM((16, MLA_HEAD_PAD), BF16)],
        compiler_params=_params("parallel", "parallel", "arbitrary"),
        name="mla",
    )(qm, km, vm)


def _first_argmax(vals):
    best, idx = vals[0], jnp.zeros(vals[0].shape, jnp.int32)
    for i in range(1, len(vals)):
        better = vals[i] > best
        idx = jnp.where(better, i, idx)
        best = jnp.where(better, vals[i], best)
    return best, idx


def _select(idx, vals):
    out = vals[0]
    for i in range(1, len(vals)):
        out = jnp.where(idx == i, vals[i], out)
    return out


def _merge_kernel(x_ref, oa_ref, ob_ref, oc_ref, wg_ref, wbr_ref, wo_ref, lng_ref, lnb_ref,
                  wrh_ref, wrl_ref, rb_ref, x1_ref, xpk_ref, cls_ref):
    x = x_ref[...]
    xb = x.astype(BF16)
    merged = None
    for i, o_ref in enumerate((oa_ref, ob_ref, oc_ref)):
        g = _sigmoid(_dot(xb, wg_ref[:, i * D_MODEL:(i + 1) * D_MODEL]))
        term = g * _dot(o_ref[...], wbr_ref[i])
        merged = term if merged is None else merged + term
    m = _dot(merged.astype(BF16), wo_ref[...])
    x1 = _layer_norm(DEEPNORM_ALPHA * x + m, lng_ref[...], lnb_ref[...])
    x1_ref[...] = x1

    x1h = x1.astype(BF16)
    x1l = (x1 - x1h.astype(F32)).astype(BF16)
    logits = _dot_nt(wrh_ref[...], x1h) + (_dot_nt(wrl_ref[...], x1h) + _dot_nt(wrh_ref[...], x1l))
    scores = _sigmoid(logits)
    biased = scores + rb_ref[...]
    sc = [scores[e:e + 1, :] for e in range(N_EXPERTS)]
    bi = [biased[e:e + 1, :] for e in range(N_EXPERTS)]
    grp_scores = []
    for g in range(N_GROUPS):
        v = bi[g * EXPERTS_PER_GROUP:(g + 1) * EXPERTS_PER_GROUP]
        top2 = None
        for i in range(EXPERTS_PER_GROUP):
            for k in range(i + 1, EXPERTS_PER_GROUP):
                top2 = v[i] + v[k] if top2 is None else jnp.maximum(top2, v[i] + v[k])
        grp_scores.append(top2)
    _, grp = _first_argmax(grp_scores)
    cand = [_select(grp, [bi[g * EXPERTS_PER_GROUP + i] for g in range(N_GROUPS)])
            for i in range(EXPERTS_PER_GROUP)]
    cand_sc = [_select(grp, [sc[g * EXPERTS_PER_GROUP + i] for g in range(N_GROUPS)])
               for i in range(EXPERTS_PER_GROUP)]
    _, i1 = _first_argmax(cand)
    _, i2 = _first_argmax([jnp.where(i1 == i, -jnp.inf, cand[i]) for i in range(EXPERTS_PER_GROUP)])
    s1 = _select(i1, cand_sc)
    s2 = _select(i2, cand_sc)
    tot = s1 + s2
    g1 = s1 / tot
    g2 = s2 / tot
    first_lower = i1 < i2
    a = jnp.where(first_lower, i1, i2)
    b = jnp.where(first_lower, i2, i1)
    cls_ref[...] = grp * PAIRS_PER_GROUP + (((a * (7 - a)) >> 1) + (b - a - 1))
    xpk_ref[:, :D_MODEL // 2] = _pack_bf16_pairs(x1)
    tm = x.shape[0]
    sub = lax.broadcasted_iota(jnp.int32, (LANES, tm), 0)
    ga = jnp.broadcast_to(jnp.where(first_lower, g1, g2), (LANES, tm))
    gb = jnp.broadcast_to(jnp.where(first_lower, g2, g1), (LANES, tm))
    gmat = jnp.where(sub == 0, ga, jnp.where(sub == 1, gb, 0.0))
    xpk_ref[:, D_MODEL // 2:] = lax.bitcast_convert_type(gmat.T, jnp.int32)


def _merge_call(x2d, oa, ob, oc, w, shared):
    n = x2d.shape[0]
    tm = MERGE_TM
    weights = [w['wg'], w['wbr'], w['wo'], w['ln1_g'], w['ln1_b'], shared['wrh'], shared['wrl'], shared['rb']]
    tok = lambda width: pl.BlockSpec((tm, width), lambda i: (i, 0))
    return pl.pallas_call(
        _merge_kernel,
        grid=(n // tm,),
        in_specs=[tok(D_MODEL), tok(A_WIDTH), tok(NA_WIDTH), tok(MLA_WIDTH)]
                 + [_const_spec(t.shape) for t in weights],
        out_specs=[tok(D_MODEL), tok(ROW_W), pl.BlockSpec((1, tm), lambda i: (0, i))],
        out_shape=[jax.ShapeDtypeStruct((n, D_MODEL), F32),
                   jax.ShapeDtypeStruct((n, ROW_W), jnp.int32),
                   jax.ShapeDtypeStruct((1, n), jnp.int32)],
        compiler_params=_params("parallel"),
        name="merge",
    )(x2d, oa, ob, oc, *weights)


def _row_copy(src, src_row, dst, dst_row, sem):
    return pltpu.make_async_copy(src.at[pl.ds(src_row, 1)], dst.at[pl.ds(dst_row, 1)], sem)


def _dispatch_kernel(pstart_ref, pend_ref, nused_ref, dest_ref, x_ref, spare_hbm, xs_hbm, st_ref, zbuf, zsem, sem):
    i = pl.program_id(0)
    tm = dest_ref.shape[2]
    n_slots = xs_hbm.shape[0]
    n_blocks = n_slots // MOE_BLOCK

    @pl.when(i == 0)
    def _():
        no_owner = pltpu.make_async_copy(spare_hbm, st_ref, zsem)
        no_owner.start()
        no_owner.wait()
        zbuf[...] = jnp.zeros_like(zbuf)

        def zero_block(first_slot):
            dst = xs_hbm.at[pl.ds(pl.multiple_of(first_slot, MOE_BLOCK), MOE_BLOCK)]
            return pltpu.make_async_copy(zbuf, dst, zsem)

        def for_each_zero_block(fn):
            for g in range(N_CLASSES):
                @pl.when(pend_ref[g] > pstart_ref[g])
                def _():
                    fn(zero_block(pend_ref[g] - MOE_BLOCK))

                @pl.when(nused_ref[0] + g < n_blocks)
                def _():
                    fn(zero_block((nused_ref[0] + g) * MOE_BLOCK))

        for_each_zero_block(lambda cp: cp.start())
        for_each_zero_block(lambda cp: cp.wait())

    def wait(j, c):
        _row_copy(x_ref, 0, xs_hbm, 0, sem).wait()
        return c

    for j in range(tm):
        d = dest_ref[0, 0, j]
        _row_copy(x_ref, j, xs_hbm, d, sem).start(priority=j % 2)
        st_ref[d] = i * tm + j
    lax.fori_loop(0, tm, wait, 0, unroll=8)


def _dispatch_call(xpk, dest3, pad_start, pad_end, n_used, n_slots):
    n = xpk.shape[0]
    tm = dest3.shape[2]
    grid_spec = pltpu.PrefetchScalarGridSpec(
        num_scalar_prefetch=3,
        grid=(n // tm,),
        in_specs=[pl.BlockSpec((1, 1, tm), lambda i, ps, pe, nu: (i, 0, 0), memory_space=pltpu.SMEM),
                  pl.BlockSpec((tm, ROW_W), lambda i, ps, pe, nu: (i, 0)),
                  pl.BlockSpec(memory_space=pl.ANY)],
        out_specs=[pl.BlockSpec(memory_space=pl.ANY), pl.BlockSpec(memory_space=pltpu.SMEM)],
        scratch_shapes=[pltpu.VMEM((MOE_BLOCK, ROW_W), jnp.int32), pltpu.SemaphoreType.DMA(()),
                        pltpu.SemaphoreType.DMA(())],
    )
    return pl.pallas_call(
        _dispatch_kernel,
        grid_spec=grid_spec,
        out_shape=[jax.ShapeDtypeStruct((n_slots, ROW_W), jnp.int32),
                   jax.ShapeDtypeStruct((n_slots,), jnp.int32)],
        compiler_params=_params("arbitrary"),
        name="dispatch",
    )(pad_start, pad_end, n_used, dest3, xpk, n + jnp.arange(n_slots, dtype=jnp.int32) % MOE_BLOCK)


def _experts_kernel(ba_ref, bb_ref, nused_ref, x_ref, st_prev_ref, st_cur_ref, wgu_a_ref, wgu_b_ref,
                    wd_a_ref, wd_b_ref, f_hbm, ybuf, sems, zsem):
    b = pl.program_id(0)
    last = pl.num_programs(0) - 1

    def scatter(st_ref, slot):
        for j in range(MOE_BLOCK):
            _row_copy(ybuf.at[slot], j, f_hbm, st_ref[0, 0, j], sems.at[slot]).start(priority=j % 2)

    def drain(slot):
        def wait(j, c):
            _row_copy(ybuf.at[slot], 0, f_hbm, 0, sems.at[slot]).wait()
            return c
        lax.fori_loop(0, MOE_BLOCK, wait, 0, unroll=8)

    def block_result():
        lo, hi = _unpack_bf16_pairs(x_ref[:, :D_MODEL // 2])
        lo, hi = lo.astype(BF16), hi.astype(BF16)
        gates = lax.bitcast_convert_type(x_ref[:, D_MODEL // 2:], F32)
        y = None
        for lane, (wgu_ref, wd_ref) in enumerate(((wgu_a_ref, wd_a_ref), (wgu_b_ref, wd_b_ref))):
            gu = _dot(lo, wgu_ref[0, :D_MODEL // 2, :]) + _dot(hi, wgu_ref[0, D_MODEL // 2:, :])
            gt = gu[:, :D_EXPERT]
            hid = gt * _sigmoid(gt) * gu[:, D_EXPERT:] * gates[:, lane:lane + 1]
            term = _dot(hid.astype(BF16), wd_ref[0])
            y = term if y is None else y + term
        return _pack_bf16_pairs(y)

    def step(slot):
        @pl.when(b == 0)
        def _():
            ybuf[1 - slot] = jnp.zeros(ybuf.shape[1:], ybuf.dtype)
            spare = f_hbm.at[pl.ds(f_hbm.shape[0] - MOE_BLOCK, MOE_BLOCK)]
            zero_spare = pltpu.make_async_copy(ybuf.at[1 - slot], spare, zsem)
            zero_spare.start()
            zero_spare.wait()
            ybuf[slot] = block_result()

        @pl.when(b >= 2)
        def _():
            drain(slot)

        @pl.when((b >= 1) & (b < nused_ref[0]))
        def _():
            scatter(st_prev_ref, 1 - slot)
            ybuf[slot] = block_result()

        @pl.when((b >= 1) & (b >= nused_ref[0]))
        def _():
            scatter(st_prev_ref, 1 - slot)
            ybuf[slot] = jnp.zeros(ybuf.shape[1:], ybuf.dtype)

        @pl.when(b == last)
        def _():
            @pl.when(b >= 1)
            def _():
                drain(1 - slot)

            scatter(st_cur_ref, slot)
            drain(slot)

    for slot in range(2):
        @pl.when(b % 2 == slot)
        def _():
            step(slot)


def _experts_call(xs, slot_tok, block_a, block_b, n_used, n_tok, w):
    n_blocks = block_a.shape[0]
    wgu_block = (1, D_MODEL, 2 * D_EXPERT)
    wd_block = (1, D_EXPERT, D_MODEL)
    st3 = slot_tok.reshape(n_blocks, 1, MOE_BLOCK)
    grid_spec = pltpu.PrefetchScalarGridSpec(
        num_scalar_prefetch=3,
        grid=(n_blocks,),
        in_specs=[pl.BlockSpec((MOE_BLOCK, ROW_W), lambda b, ba, bb, bv: (b, 0)),
                  pl.BlockSpec((1, 1, MOE_BLOCK), lambda b, ba, bb, bv: (jnp.maximum(b - 1, 0), 0, 0),
                               memory_space=pltpu.SMEM),
                  pl.BlockSpec((1, 1, MOE_BLOCK), lambda b, ba, bb, bv: (b, 0, 0), memory_space=pltpu.SMEM),
                  pl.BlockSpec(wgu_block, lambda b, ba, bb, bv: (ba[b], 0, 0)),
                  pl.BlockSpec(wgu_block, lambda b, ba, bb, bv: (bb[b], 0, 0)),
                  pl.BlockSpec(wd_block, lambda b, ba, bb, bv: (ba[b], 0, 0)),
                  pl.BlockSpec(wd_block, lambda b, ba, bb, bv: (bb[b], 0, 0))],
        out_specs=pl.BlockSpec(memory_space=pl.ANY),
        scratch_shapes=[pltpu.VMEM((2, MOE_BLOCK, D_MODEL // 2), jnp.int32), pltpu.SemaphoreType.DMA((2,)),
                        pltpu.SemaphoreType.DMA(())],
    )
    return pl.pallas_call(
        _experts_kernel,
        grid_spec=grid_spec,
        out_shape=jax.ShapeDtypeStruct((n_tok + MOE_BLOCK, D_MODEL // 2), jnp.int32),
        compiler_params=_params("arbitrary"),
        name="experts",
    )(block_a, block_b, n_used, xs, st3, st3, w['wgu'], w['wgu'], w['wd'], w['wd'])


def _combine_kernel(x1_ref, f_ref, lng_ref, lnb_ref, o_ref):
    f = jnp.concatenate(_unpack_bf16_pairs(f_ref[...]), axis=1)
    o_ref[...] = _layer_norm(DEEPNORM_ALPHA * x1_ref[...] + f, lng_ref[...], lnb_ref[...])


def _combine_call(x1, f, w):
    n = x1.shape[0]
    tm = COMBINE_TM
    return pl.pallas_call(
        _combine_kernel,
        grid=(n // tm,),
        in_specs=[pl.BlockSpec((tm, D_MODEL), lambda i: (i, 0)),
                  pl.BlockSpec((tm, D_MODEL // 2), lambda i: (i, 0)),
                  _const_spec(w['ln2_g'].shape), _const_spec(w['ln2_b'].shape)],
        out_specs=pl.BlockSpec((tm, D_MODEL), lambda i: (i, 0)),
        out_shape=jax.ShapeDtypeStruct((n, D_MODEL), F32),
        compiler_params=_params("parallel"),
        name="combine",
    )(x1, f, w['ln2_g'], w['ln2_b'])


def _class_experts():
    pairs = [(a, b) for a in range(EXPERTS_PER_GROUP) for b in range(a + 1, EXPERTS_PER_GROUP)]
    ea = [g * EXPERTS_PER_GROUP + a for g in range(N_GROUPS) for a, _ in pairs]
    eb = [g * EXPERTS_PER_GROUP + b for g in range(N_GROUPS) for _, b in pairs]
    return jnp.asarray(ea, jnp.int32), jnp.asarray(eb, jnp.int32)


def _dispatch_plan(cls):
    n = cls.shape[1]
    n_blocks = -(-n // MOE_BLOCK) + N_CLASSES
    onehot = (cls[0][:, None] == jnp.arange(N_CLASSES, dtype=jnp.int32)).astype(jnp.int32)
    incl = jnp.cumsum(onehot, axis=0)
    counts = incl[-1]
    padded = (counts + MOE_BLOCK - 1) // MOE_BLOCK * MOE_BLOCK
    pad_end = jnp.cumsum(padded).astype(jnp.int32)
    pad_start = pad_end - padded
    slot = pad_start[None, :] + (incl - onehot)
    dest = jnp.sum(onehot * slot, axis=1).astype(jnp.int32)
    dest3 = dest.reshape(n // ROUTE_TM, 1, ROUTE_TM)
    first_slot = jnp.arange(n_blocks, dtype=jnp.int32) * MOE_BLOCK
    block_c = jnp.minimum(jnp.sum(pad_end[None, :] <= first_slot[:, None], axis=1), N_CLASSES - 1)
    ea, eb = _class_experts()
    n_used = pad_end[-1:] // MOE_BLOCK
    return dest3, pad_start, pad_end, ea[block_c], eb[block_c], n_used, n_blocks * MOE_BLOCK


def _rope_tables(seq):
    inv = 1.0 / (ROPE_BASE ** (jnp.arange(0, QK_ROPE, 2, dtype=F32) / QK_ROPE))
    ang = jnp.arange(seq, dtype=F32)[:, None] * inv[None, :]
    pad = jnp.zeros((seq, LANES - QK_ROPE), F32)
    cos, sin = jnp.cos(ang), jnp.sin(ang)
    return jnp.concatenate([cos, cos, pad], -1), jnp.concatenate([sin, sin, pad], -1)


def _pad_cols(t, width):
    return jnp.pad(t, ((0, 0), (0, width - t.shape[1])))


def _rope_weight_pair(w_rope):
    half = QK_ROPE // 2
    swapped = jnp.concatenate([-w_rope[:, half:], w_rope[:, :half]], axis=1)
    return _pad_cols(w_rope, LANES), _pad_cols(swapped, LANES)


def _layer_weights(l, w_in, a_ln_g, a_ln_b, a_ws, a_bs, na_rpb, mla_q_norm, mla_kv_norm, mla_w_uq,
                   mla_w_ukv, w_br_a, w_br_b, w_br_c, w_o, ln1_g, ln1_b, ln2_g, ln2_b, w_gate, w_up,
                   w_down):
    sizes = (2 * A_WIDTH, NA_WIDTH, NA_WIDTH, NA_WIDTH, Q_LORA, KV_LORA, QK_ROPE, N_BRANCH * D_MODEL)
    cuts = np.cumsum(sizes)[:-1].tolist()
    wa, wq, wk, wv, wcq, wckv, wkr, wg = jnp.split(w_in[l], cuts, axis=-1)
    row = lambda t: t.reshape(1, -1).astype(F32)
    uq = mla_w_uq[l].reshape(Q_LORA, MLA_HEADS, QK_NOPE + QK_ROPE)
    wqn = uq[:, :, :QK_NOPE].reshape(Q_LORA, MLA_HEADS * QK_NOPE)
    rope_pairs = [_rope_weight_pair(uq[:, h, QK_NOPE:]) for h in range(MLA_HEADS)]
    wqr = jnp.concatenate([p[0] for p in rope_pairs], axis=1)
    wqs = jnp.concatenate([p[1] for p in rope_pairs], axis=1)
    ukv = mla_w_ukv[l].reshape(KV_LORA, MLA_HEADS, QK_NOPE + V_DIM)
    wukv = jnp.concatenate([ukv[:, :, :QK_NOPE].reshape(KV_LORA, -1), ukv[:, :, QK_NOPE:].reshape(KV_LORA, -1)], 1)
    wkr_p, wkrs_p = _rope_weight_pair(wkr)
    return {
        'wa': wa.astype(BF16), 'a_ln_g': row(a_ln_g[l]), 'a_ln_b': row(a_ln_b[l]),
        'ws': a_ws[l].astype(BF16),
        'bs': jnp.broadcast_to(a_bs[l][:, :, None], (A_GROUPS, CHUNK, CHUNK)).astype(F32),
        'wqkv': jnp.concatenate([wq * NA_SCALE, wk, wv], axis=1).astype(BF16),
        'wcq': wcq.astype(BF16), 'q_norm': row(mla_q_norm[l]),
        'wqn': wqn.astype(BF16), 'wqr': wqr.astype(BF16), 'wqs': wqs.astype(BF16),
        'wckv': wckv.astype(BF16), 'kv_norm': row(mla_kv_norm[l]), 'wukv': wukv.astype(BF16),
        'wkr': wkr_p.astype(BF16), 'wkrs': wkrs_p.astype(BF16),
        'na_bias': _natten_bias(na_rpb[l]), 'na_bmax': _natten_bias_max(na_rpb[l]),
        'wg': wg.astype(BF16),
        'wbr': jnp.stack([w_br_a[l], w_br_b[l], w_br_c[l]]).astype(BF16),
        'wo': w_o[l].astype(BF16), 'ln1_g': row(ln1_g[l]), 'ln1_b': row(ln1_b[l]),
        'ln2_g': row(ln2_g[l]), 'ln2_b': row(ln2_b[l]),
        'wgu': jnp.concatenate([w_gate[l], w_up[l]], axis=-1).astype(BF16),
        'wd': w_down[l].astype(BF16),
    }


def _shared_weights(w_router, router_bias):
    wr_t = w_router.T.astype(F32)
    wrh = wr_t.astype(BF16)
    wrl = (wr_t - wrh.astype(F32)).astype(BF16)
    rb = jnp.broadcast_to(router_bias.astype(F32)[:, None], (N_EXPERTS, MERGE_TM))
    return {'wrh': wrh, 'wrl': wrl, 'rb': rb}


def _trunk(x, layers, shared):
    bsz, seq, _ = x.shape
    cos2, sin2 = _rope_tables(seq)
    x2d = x.reshape(bsz * seq, D_MODEL)
    moe = None
    for w in layers:
        if moe is None:
            oa, q, k, v, qm, km, vm, kn2 = _pre_call(x2d, seq, cos2, sin2, w)
        else:
            oa, q, k, v, qm, km, vm, kn2, x2d = _pre_call(None, seq, cos2, sin2, w, moe)
        ob = _natten_call(q, k, v, kn2, w['na_bias'], w['na_bmax'], bsz, seq)
        oc = _mla_call(qm, km, vm, bsz, seq)
        x1, xpk, cls = _merge_call(x2d, oa, ob, oc, w, shared)
        dest3, pad_start, pad_end, block_a, block_b, n_used, n_slots = _dispatch_plan(cls)
        xs, slot_tok = _dispatch_call(xpk, dest3, pad_start, pad_end, n_used, n_slots)
        f = _experts_call(xs, slot_tok, block_a, block_b, n_used, x1.shape[0], w)
        moe = (x1, f, w)
    return _combine_call(x1, f, w).reshape(bsz, seq, D_MODEL)


def kernel(x_prompt, x_sample, w_in, a_ln_g, a_ln_b, a_ws, a_bs, na_rpb, mla_q_norm, mla_kv_norm, mla_w_uq, mla_w_ukv, w_br_a, w_br_b, w_br_c, w_o, ln1_g, ln1_b, ln2_g, ln2_b, w_router, router_bias, w_gate, w_up, w_down):
    layers = [_layer_weights(l, w_in, a_ln_g, a_ln_b, a_ws, a_bs, na_rpb, mla_q_norm, mla_kv_norm,
                             mla_w_uq, mla_w_ukv, w_br_a, w_br_b, w_br_c, w_o, ln1_g, ln1_b, ln2_g,
                             ln2_b, w_gate, w_up, w_down) for l in range(DEPTH)]
    shared = _shared_weights(w_router, router_bias)
    return (_trunk(x_prompt, layers, shared), _trunk(x_sample, layers, shared))
```

```python
import functools
import math

import jax
import jax.numpy as jnp
import numpy as np
from jax import lax
from jax.experimental import pallas as pl
from jax.experimental.pallas import tpu as pltpu

F32 = jnp.float32
BF16 = jnp.bfloat16

D_MODEL = 1024
DEPTH = 2
GRID_W = 64
CHUNK = 128
A_WIDTH = 512
A_GROUPS = 4
NA_HEADS = 8
NA_HEAD_DIM = 64
NA_WIN_H = 8
NA_WIN_W = 16
NA_WIDTH = NA_HEADS * NA_HEAD_DIM
NA_SCALE = NA_HEAD_DIM ** -0.5
MLA_HEADS = 4
Q_LORA = 384
KV_LORA = 256
QK_NOPE = 128
QK_ROPE = 64
V_DIM = 128
MLA_WIDTH = MLA_HEADS * V_DIM
MLA_SCALE = (QK_NOPE + QK_ROPE) ** -0.5
ROPE_BASE = 10000.0
N_BRANCH = 3
N_EXPERTS = 16
N_GROUPS = 4
EXPERTS_PER_GROUP = N_EXPERTS // N_GROUPS
TOP_K = 2
D_EXPERT = 256
MOE_BLOCK = 256
DEEPNORM_ALPHA = (2 * DEPTH) ** 0.25
LN_EPS = 1e-5
RMS_EPS = 1e-6
NEG_INF = -1e30
LOG2E = math.log2(math.e)

VMEM_LIMIT_BYTES = 56 * 1024 * 1024
LANES = 128
MLA_HEAD_PAD = 2 * LANES
NA_ROW_BLOCK = 8
NA_TOK_BLOCK = NA_ROW_BLOCK * GRID_W

ROW_W = D_MODEL // 2 + LANES
PAIRS_PER_GROUP = EXPERTS_PER_GROUP * (EXPERTS_PER_GROUP - 1) // 2
N_CLASSES = N_GROUPS * PAIRS_PER_GROUP
HIGH_HALF = -65536
PRE_TM = 512
MERGE_TM = 1024
ROUTE_TM = 256
COMBINE_TM = 512
MLA_TQ = 2048
MLA_TK = 1024
MLA_MAX_UNROLL = 5
MLA_SHIFT_LANE = QK_NOPE + QK_ROPE
MLA_L_MIN = 2.0 ** -60
MLA_L_MAX = 2.0 ** 100
MLA_BOUND_SLACK = 1.0 + 2.0 ** -7
NA_BOUND_SLACK = 1.0 + 2.0 ** -6
NA_L_MIN = 2.0 ** -60
NA_L_MAX = 2.0 ** 100


def _params(*sem):
    return pltpu.CompilerParams(dimension_semantics=sem, vmem_limit_bytes=VMEM_LIMIT_BYTES)


def _const_spec(shape):
    nd = len(shape)
    return pl.BlockSpec(shape, lambda *_: (0,) * nd, pipeline_mode=pl.Buffered(1))


def _layer_norm(y, g, b):
    mu = jnp.mean(y, -1, keepdims=True)
    yc = y - mu
    var = jnp.mean(yc * yc, -1, keepdims=True)
    return yc * lax.rsqrt(var + LN_EPS) * g + b


def _rms_norm(y, g):
    return y * lax.rsqrt(jnp.mean(y * y, -1, keepdims=True) + RMS_EPS) * g


def _gelu_tanh(x):
    return 0.5 * x * (1.0 + jnp.tanh(math.sqrt(2.0 / math.pi) * (x + 0.044715 * (x * x * x))))


def _sigmoid(x):
    return 1.0 / (1.0 + jnp.exp(-x))


def _pack_bf16_pairs(y):
    w = y.shape[1] // 2
    lo = lax.bitcast_convert_type(y[:, :w].astype(BF16).astype(F32), jnp.int32)
    hi = lax.bitcast_convert_type(y[:, w:].astype(BF16).astype(F32), jnp.int32)
    return lax.shift_right_logical(lo, 16) | (hi & HIGH_HALF)


def _unpack_bf16_pairs(p):
    lo = lax.bitcast_convert_type(lax.shift_left(p, 16), F32)
    hi = lax.bitcast_convert_type(p & HIGH_HALF, F32)
    return lo, hi


def _dot(a, b):
    return jnp.dot(a, b, preferred_element_type=F32)


def _dot_nt(a, b):
    return lax.dot_general(a, b, (((1,), (1,)), ((), ())), preferred_element_type=F32)


def _pre_kernel(*refs, after_moe):
    if after_moe:
        x1_ref, f_ref, ln2g_ref, ln2b_ref = refs[:4]
        refs, x2_ref = refs[4:-1], refs[-1]
        f = jnp.concatenate(_unpack_bf16_pairs(f_ref[...]), axis=1)
        x = _layer_norm(DEEPNORM_ALPHA * x1_ref[...] + f, ln2g_ref[...], ln2b_ref[...])
        x2_ref[...] = x
    else:
        x, refs = refs[0][...], refs[1:]
    (cos_ref, sin_ref, wa_ref, lng_ref, lnb_ref, ws_ref, bs_ref, wqkv_ref, wcq_ref, qnorm_ref, wqn_ref,
     wqr_ref, wqs_ref, wckv_ref, kvnorm_ref, wukv_ref, wkr_ref, wkrs_ref, hsum_ref,
     oa_ref, q_ref, k_ref, v_ref, qm_ref, km_ref, vm_ref, kn2_ref) = refs
    tm = x.shape[0]
    xb = x.astype(BF16)
    cos2 = cos_ref[...]
    sin2 = sin_ref[...]

    a = _gelu_tanh(_dot(xb, wa_ref[...]))
    u = a[:, :A_WIDTH]
    vn = _layer_norm(a[:, A_WIDTH:], lng_ref[...], lnb_ref[...]).astype(BF16)
    for c in range(tm // CHUNK):
        rs = slice(c * CHUNK, (c + 1) * CHUNK)
        for g in range(A_GROUPS):
            cs = slice(g * CHUNK, (g + 1) * CHUNK)
            sp = _dot(ws_ref[g], vn[rs, cs]) + bs_ref[g]
            oa_ref[rs, cs] = (u[rs, cs] * sp).astype(BF16)

    qkv = _dot(xb, wqkv_ref[...])
    q_ref[...] = qkv[:, :NA_WIDTH].astype(BF16)
    kb = qkv[:, NA_WIDTH:2 * NA_WIDTH].astype(BF16)
    k_ref[...] = kb
    v_ref[...] = qkv[:, 2 * NA_WIDTH:].astype(BF16)
    kf = kb.astype(F32)
    kn2_ref[...] = _dot((kf * kf).astype(BF16), hsum_ref[...])

    cqn = _rms_norm(_dot(xb, wcq_ref[...]), qnorm_ref[...]).astype(BF16)
    qn = _dot(cqn, wqn_ref[...])
    qr = _dot(cqn, wqr_ref[...])
    qs = _dot(cqn, wqs_ref[...])
    qscale = MLA_SCALE * LOG2E
    shift_lane = lax.broadcasted_iota(jnp.int32, (tm, LANES), 1) == MLA_SHIFT_LANE - QK_NOPE
    for h in range(MLA_HEADS):
        hs = slice(h * LANES, (h + 1) * LANES)
        nope = (qn[:, hs] * qscale).astype(BF16)
        rot = ((qr[:, hs] * cos2 + qs[:, hs] * sin2) * qscale).astype(BF16)
        nf, rf = nope.astype(F32), rot.astype(F32)
        norm = jnp.sqrt(jnp.sum(nf * nf, -1, keepdims=True) + jnp.sum(rf * rf, -1, keepdims=True))
        norm = jnp.broadcast_to(norm * MLA_BOUND_SLACK, rot.shape).astype(BF16)
        qm_ref[:, h * MLA_HEAD_PAD:h * MLA_HEAD_PAD + LANES] = nope
        qm_ref[:, h * MLA_HEAD_PAD + LANES:(h + 1) * MLA_HEAD_PAD] = jnp.where(shift_lane, norm, rot)

    ckvn = _rms_norm(_dot(xb, wckv_ref[...]), kvnorm_ref[...]).astype(BF16)
    kv = _dot(ckvn, wukv_ref[...])
    krope = _dot(xb, wkr_ref[...]) * cos2 + _dot(xb, wkrs_ref[...]) * sin2
    lane = lax.broadcasted_iota(jnp.int32, krope.shape, 1)
    krope = jnp.where(lane == MLA_SHIFT_LANE - QK_NOPE, -1.0, krope).astype(BF16)
    for h in range(MLA_HEADS):
        km_ref[:, h * MLA_HEAD_PAD:h * MLA_HEAD_PAD + LANES] = kv[:, h * LANES:(h + 1) * LANES].astype(BF16)
        km_ref[:, h * MLA_HEAD_PAD + LANES:(h + 1) * MLA_HEAD_PAD] = krope
    vm_ref[...] = kv[:, MLA_HEADS * QK_NOPE:].astype(BF16)


def _head_sum_matrix():
    feat = np.arange(NA_WIDTH)[:, None] // NA_HEAD_DIM
    return jnp.asarray(feat == np.arange(LANES)[None, :], BF16)


def _pre_call(x2d, seq, cos2, sin2, w, moe=None):
    n = x2d.shape[0] if moe is None else moe[0].shape[0]
    tm = PRE_TM
    per_seq = seq // tm
    weights = [w['wa'], w['a_ln_g'], w['a_ln_b'], w['ws'], w['bs'], w['wqkv'], w['wcq'], w['q_norm'],
               w['wqn'], w['wqr'], w['wqs'], w['wckv'], w['kv_norm'], w['wukv'], w['wkr'], w['wkrs'],
               _head_sum_matrix()]
    tok = lambda width: pl.BlockSpec((tm, width), lambda i: (i, 0))
    pos = pl.BlockSpec((tm, LANES), lambda i: (i % per_seq, 0))
    out_widths = [A_WIDTH, NA_WIDTH, NA_WIDTH, NA_WIDTH, MLA_HEADS * MLA_HEAD_PAD,
                  MLA_HEADS * MLA_HEAD_PAD, MLA_WIDTH]
    out_specs = [tok(wd) for wd in out_widths] + [tok(LANES)]
    out_shape = ([jax.ShapeDtypeStruct((n, wd), BF16) for wd in out_widths]
                 + [jax.ShapeDtypeStruct((n, LANES), F32)])
    if moe is None:
        stream, stream_specs = [x2d], [tok(D_MODEL)]
    else:
        x1, f, prev = moe
        stream = [x1, f, prev['ln2_g'], prev['ln2_b']]
        stream_specs = [tok(D_MODEL), tok(D_MODEL // 2), _const_spec(prev['ln2_g'].shape),
                        _const_spec(prev['ln2_b'].shape)]
        out_specs.append(tok(D_MODEL))
        out_shape.append(jax.ShapeDtypeStruct((n, D_MODEL), F32))
    return pl.pallas_call(
        functools.partial(_pre_kernel, after_moe=moe is not None),
        grid=(n // tm,),
        in_specs=stream_specs + [pos, pos] + [_const_spec(t.shape) for t in weights],
        out_specs=out_specs,
        out_shape=out_shape,
        compiler_params=_params("parallel"),
        name="pre",
    )(*stream, cos2, sin2, *weights)


def _natten_kernel(q_ref, kp_ref, kc_ref, kn_ref, vp_ref, vc_ref, vn_ref, np_ref, nc_ref, nn_ref,
                   bias_ref, bmax_ref, o_ref, kcat, vcat, *, rows):
    j = pl.program_id(1)
    nt = NA_TOK_BLOCK
    kcat[0:nt] = kp_ref[...]
    kcat[nt:2 * nt] = kc_ref[...]
    kcat[2 * nt:3 * nt] = kn_ref[...]
    vcat[0:nt] = vp_ref[...]
    vcat[nt:2 * nt] = vc_ref[...]
    vcat[2 * nt:3 * nt] = vn_ref[...]
    r0 = j * NA_ROW_BLOCK
    lane_lo = lax.broadcasted_iota(jnp.int32, (GRID_W, LANES), 1) < NA_HEAD_DIM

    kmax2 = jnp.maximum(jnp.maximum(jnp.max(np_ref[...], 0, keepdims=True),
                                    jnp.max(nc_ref[...], 0, keepdims=True)),
                        jnp.max(nn_ref[...], 0, keepdims=True))
    kmax = jnp.broadcast_to(jnp.sqrt(kmax2) * NA_BOUND_SLACK, (2 * GRID_W, LANES))
    row_lo = lax.broadcasted_iota(jnp.int32, (2 * GRID_W, LANES), 0) < GRID_W
    lane = lax.broadcasted_iota(jnp.int32, (2 * GRID_W, LANES), 1)
    kcols = [jnp.sum(jnp.where(lane == jnp.where(row_lo, 2 * hp, 2 * hp + 1), kmax, 0.0), -1, keepdims=True)
             for hp in range(NA_HEADS // 2)]

    def attend(exact):
        bad = jnp.zeros((2 * GRID_W, 1), F32)
        for i in range(NA_ROW_BLOCK):
            r = r0 + i
            start = jnp.clip(r - NA_WIN_H // 2, 0, rows - NA_WIN_H)
            pat = start - r + (NA_WIN_H - 1)
            koff = pl.multiple_of((start - r0 + NA_ROW_BLOCK) * GRID_W, GRID_W)
            qs = slice(i * GRID_W, (i + 1) * GRID_W)
            for hp in range(NA_HEADS // 2):
                cs = slice(hp * LANES, (hp + 1) * LANES)
                qp = q_ref[qs, cs]
                zero = jnp.zeros_like(qp)
                q2 = jnp.concatenate([jnp.where(lane_lo, qp, zero), jnp.where(lane_lo, zero, qp)], axis=0)
                kk = kcat[pl.ds(koff, NA_WIN_H * GRID_W), cs]
                vv = vcat[pl.ds(koff, NA_WIN_H * GRID_W), cs]
                s = _dot_nt(q2, kk) + bias_ref[pat, hp]
                if exact:
                    m = jnp.max(s, -1, keepdims=True)
                else:
                    qf = q2.astype(F32)
                    m = jnp.sqrt(jnp.sum(qf * qf, -1, keepdims=True)) * kcols[hp] + bmax_ref[hp][:, 0:1]
                e = jnp.exp(s - m)
                l = jnp.sum(e, -1, keepdims=True)
                if not exact:
                    bad = bad + jnp.where((l >= NA_L_MIN) & (l <= NA_L_MAX), 0.0, 1.0)
                o2 = _dot(e.astype(BF16), vv) / l
                o_ref[qs, cs] = jnp.where(lane_lo, o2[:GRID_W], o2[GRID_W:]).astype(BF16)
        return bad

    n_bad = jnp.sum(attend(exact=False))

    @pl.when(n_bad > 0.0)
    def _():
        attend(exact=True)


def _natten_call(q, k, v, knorm, bias, bmax, bsz, seq):
    rows = seq // GRID_W
    nrb = rows // NA_ROW_BLOCK
    nt = NA_TOK_BLOCK
    cur = lambda width: pl.BlockSpec((nt, width), lambda b, j: (b * nrb + j, 0))
    prev = lambda width: pl.BlockSpec((nt, width), lambda b, j: (b * nrb + jnp.maximum(j - 1, 0), 0))
    nxt = lambda width: pl.BlockSpec((nt, width), lambda b, j: (b * nrb + jnp.minimum(j + 1, nrb - 1), 0))
    w = NA_WIDTH
    return pl.pallas_call(
        functools.partial(_natten_kernel, rows=rows),
        grid=(bsz, nrb),
        in_specs=[cur(w), prev(w), cur(w), nxt(w), prev(w), cur(w), nxt(w),
                  prev(LANES), cur(LANES), nxt(LANES), _const_spec(bias.shape), _const_spec(bmax.shape)],
        out_specs=cur(w),
        out_shape=jax.ShapeDtypeStruct((bsz * seq, NA_WIDTH), BF16),
        scratch_shapes=[pltpu.VMEM((3 * nt, NA_WIDTH), BF16), pltpu.VMEM((3 * nt, NA_WIDTH), BF16)],
        compiler_params=_params("parallel", "parallel"),
        name="natten",
    )(q, k, k, k, v, v, v, knorm, knorm, knorm, bias, bmax)


def _natten_bias_max(rpb):
    m = jnp.max(rpb.astype(F32), axis=(1, 2)).reshape(NA_HEADS // 2, 2, 1, 1)
    return jnp.broadcast_to(m, (NA_HEADS // 2, 2, GRID_W, LANES)).reshape(NA_HEADS // 2, 2 * GRID_W, LANES)


def _natten_bias(rpb):
    c = jnp.arange(GRID_W)
    col_start = jnp.clip(c - NA_WIN_W // 2, 0, GRID_W - NA_WIN_W)
    in_win = (c[None, :] >= col_start[:, None]) & (c[None, :] < col_start[:, None] + NA_WIN_W)
    dc = jnp.clip(c[None, :] - c[:, None] + (NA_WIN_W - 1), 0, 2 * NA_WIN_W - 2)
    onehot = (dc[:, :, None] == jnp.arange(2 * NA_WIN_W - 1)).astype(F32)
    t = jnp.einsum('hrd,qkd->hrqk', rpb.astype(F32), onehot, precision=lax.Precision.HIGHEST)
    t = jnp.where(in_win[None, None], t, NEG_INF)
    b = jnp.stack([t[:, p:p + NA_WIN_H] for p in range(NA_WIN_H)])
    b = b.transpose(0, 1, 3, 2, 4)
    return b.reshape(NA_WIN_H, NA_HEADS // 2, 2 * GRID_W, NA_WIN_H * GRID_W)


def _mla_online_softmax(q, k_ref, v_ref, tk):
    tq = q.shape[0]

    def body(j, carry):
        m, l, acc = carry
        ks = pl.ds(pl.multiple_of(j * tk, tk), tk)
        s = _dot_nt(q, k_ref[ks, :])
        m_new = jnp.maximum(m, jnp.max(s, -1, keepdims=True))
        alpha = jnp.exp2(m - m_new)
        p = jnp.exp2(s - m_new)
        l = alpha * l + jnp.sum(p, -1, keepdims=True)
        acc = alpha * acc + _dot(p.astype(BF16), v_ref[ks, :])
        return m_new, l, acc

    init = (jnp.full((tq, 1), NEG_INF, F32), jnp.zeros((tq, 1), F32), jnp.zeros((tq, V_DIM), F32))
    _, l, acc = lax.fori_loop(0, k_ref.shape[0] // tk, body, init)
    return acc, l


def _mla_kernel(q_ref, k_ref, v_ref, o_ref, kmax_ref, *, tk, unroll):
    q = q_ref[...]
    tq = q.shape[0]
    seq = k_ref.shape[0]

    @pl.when(pl.program_id(2) == 0)
    def _():
        def body(c, mx):
            kc = k_ref[pl.ds(pl.multiple_of(c * tk, tk), tk), :].astype(F32)
            n2 = jnp.sum(kc * kc, -1, keepdims=True)
            return jnp.maximum(mx, jnp.max(n2, 0, keepdims=True))

        mx = lax.fori_loop(0, seq // tk, body, jnp.zeros((1, 1), F32))
        kmax = jnp.broadcast_to(jnp.sqrt(mx) * MLA_BOUND_SLACK, kmax_ref.shape)
        lane = lax.broadcasted_iota(jnp.int32, kmax_ref.shape, 1)
        kmax_ref[...] = jnp.where(lane == MLA_SHIFT_LANE, kmax, 1.0).astype(BF16)

    q_shift = q * kmax_ref[0:1, :]
    ones = jnp.ones((tk, LANES), BF16)

    def body(j, acc):
        ks = pl.ds(pl.multiple_of(j * tk, tk), tk)
        p = jnp.exp2(_dot_nt(q_shift, k_ref[ks, :])).astype(BF16)
        return acc + _dot(p, jnp.concatenate([v_ref[ks, :], ones], axis=1))

    nk = seq // tk
    acc = lax.fori_loop(0, nk - 1, body, jnp.zeros((tq, V_DIM + LANES), F32), unroll=unroll)
    l_part = acc[:, V_DIM:]
    usable = (l_part >= MLA_L_MIN) & (l_part <= MLA_L_MAX)
    n_bad = jnp.sum(jnp.where(usable, 0.0, 1.0))
    acc = body(nk - 1, acc)
    l = acc[:, V_DIM:]
    o_ref[...] = (acc[:, :V_DIM] / l).astype(BF16)

    @pl.when(n_bad > 0.0)
    def _():
        acc, l = _mla_online_softmax(q, k_ref, v_ref, tk)
        o_ref[...] = (acc / l).astype(BF16)


def _mla_call(qm, km, vm, bsz, seq):
    tq = min(MLA_TQ, seq)
    nq = seq // tq
    tk = min(MLA_TK, seq // 2)
    looped = seq // tk - 1
    unroll = max(u for u in range(1, MLA_MAX_UNROLL + 1) if looped % u == 0)
    return pl.pallas_call(
        functools.partial(_mla_kernel, tk=tk, unroll=unroll),
        grid=(bsz, MLA_HEADS, nq),
        in_specs=[pl.BlockSpec((tq, MLA_HEAD_PAD), lambda b, h, i: (b * nq + i, h)),
                  pl.BlockSpec((seq, MLA_HEAD_PAD), lambda b, h, i: (b, h)),
                  pl.BlockSpec((seq, V_DIM), lambda b, h, i: (b, h))],
        out_specs=pl.BlockSpec((tq, V_DIM), lambda b, h, i: (b * nq + i, h)),
        out_shape=jax.ShapeDtypeStruct((bsz * seq, MLA_WIDTH), BF16),
        scratch_shapes=[pltpu.VMEM((16, MLA_HEAD_PAD), BF16)],
        compiler_params=_params("parallel", "parallel", "arbitrary"),
        name="mla",
    )(qm, km, vm)


def _first_argmax(vals):
    best, idx = vals[0], jnp.zeros(vals[0].shape, jnp.int32)
    for i in range(1, len(vals)):
        better = vals[i] > best
        idx = jnp.where(better, i, idx)
        best = jnp.where(better, vals[i], best)
    return best, idx


def _select(idx, vals):
    out = vals[0]
    for i in range(1, len(vals)):
        out = jnp.where(idx == i, vals[i], out)
    return out


def _merge_kernel(x_ref, oa_ref, ob_ref, oc_ref, wg_ref, wbr_ref, wo_ref, lng_ref, lnb_ref,
                  wrh_ref, wrl_ref, rb_ref, x1_ref, xpk_ref, cls_ref):
    x = x_ref[...]
    xb = x.astype(BF16)
    merged = None
    for i, o_ref in enumerate((oa_ref, ob_ref, oc_ref)):
        g = _sigmoid(_dot(xb, wg_ref[:, i * D_MODEL:(i + 1) * D_MODEL]))
        term = g * _dot(o_ref[...], wbr_ref[i])
        merged = term if merged is None else merged + term
    m = _dot(merged.astype(BF16), wo_ref[...])
    x1 = _layer_norm(DEEPNORM_ALPHA * x + m, lng_ref[...], lnb_ref[...])
    x1_ref[...] = x1

    x1h = x1.astype(BF16)
    x1l = (x1 - x1h.astype(F32)).astype(BF16)
    logits = _dot_nt(wrh_ref[...], x1h) + (_dot_nt(wrl_ref[...], x1h) + _dot_nt(wrh_ref[...], x1l))
    scores = _sigmoid(logits)
    biased = scores + rb_ref[...]
    sc = [scores[e:e + 1, :] for e in range(N_EXPERTS)]
    bi = [biased[e:e + 1, :] for e in range(N_EXPERTS)]
    grp_scores = []
    for g in range(N_GROUPS):
        v = bi[g * EXPERTS_PER_GROUP:(g + 1) * EXPERTS_PER_GROUP]
        top2 = None
        for i in range(EXPERTS_PER_GROUP):
            for k in range(i + 1, EXPERTS_PER_GROUP):
                top2 = v[i] + v[k] if top2 is None else jnp.maximum(top2, v[i] + v[k])
        grp_scores.append(top2)
    _, grp = _first_argmax(grp_scores)
    cand = [_select(grp, [bi[g * EXPERTS_PER_GROUP + i] for g in range(N_GROUPS)])
            for i in range(EXPERTS_PER_GROUP)]
    cand_sc = [_select(grp, [sc[g * EXPERTS_PER_GROUP + i] for g in range(N_GROUPS)])
               for i in range(EXPERTS_PER_GROUP)]
    _, i1 = _first_argmax(cand)
    _, i2 = _first_argmax([jnp.where(i1 == i, -jnp.inf, cand[i]) for i in range(EXPERTS_PER_GROUP)])
    s1 = _select(i1, cand_sc)
    s2 = _select(i2, cand_sc)
    tot = s1 + s2
    g1 = s1 / tot
    g2 = s2 / tot
    first_lower = i1 < i2
    a = jnp.where(first_lower, i1, i2)
    b = jnp.where(first_lower, i2, i1)
    cls_ref[...] = grp * PAIRS_PER_GROUP + (((a * (7 - a)) >> 1) + (b - a - 1))
    xpk_ref[:, :D_MODEL // 2] = _pack_bf16_pairs(x1)
    tm = x.shape[0]
    sub = lax.broadcasted_iota(jnp.int32, (LANES, tm), 0)
    ga = jnp.broadcast_to(jnp.where(first_lower, g1, g2), (LANES, tm))
    gb = jnp.broadcast_to(jnp.where(first_lower, g2, g1), (LANES, tm))
    gmat = jnp.where(sub == 0, ga, jnp.where(sub == 1, gb, 0.0))
    xpk_ref[:, D_MODEL // 2:] = lax.bitcast_convert_type(gmat.T, jnp.int32)


def _merge_call(x2d, oa, ob, oc, w, shared):
    n = x2d.shape[0]
    tm = MERGE_TM
    weights = [w['wg'], w['wbr'], w['wo'], w['ln1_g'], w['ln1_b'], shared['wrh'], shared['wrl'], shared['rb']]
    tok = lambda width: pl.BlockSpec((tm, width), lambda i: (i, 0))
    return pl.pallas_call(
        _merge_kernel,
        grid=(n // tm,),
        in_specs=[tok(D_MODEL), tok(A_WIDTH), tok(NA_WIDTH), tok(MLA_WIDTH)]
                 + [_const_spec(t.shape) for t in weights],
        out_specs=[tok(D_MODEL), tok(ROW_W), pl.BlockSpec((1, tm), lambda i: (0, i))],
        out_shape=[jax.ShapeDtypeStruct((n, D_MODEL), F32),
                   jax.ShapeDtypeStruct((n, ROW_W), jnp.int32),
                   jax.ShapeDtypeStruct((1, n), jnp.int32)],
        compiler_params=_params("parallel"),
        name="merge",
    )(x2d, oa, ob, oc, *weights)


def _row_copy(src, src_row, dst, dst_row, sem):
    return pltpu.make_async_copy(src.at[pl.ds(src_row, 1)], dst.at[pl.ds(dst_row, 1)], sem)


def _dispatch_kernel(pstart_ref, pend_ref, nused_ref, dest_ref, x_ref, spare_hbm, xs_hbm, st_ref, zbuf, zsem, sem):
    i = pl.program_id(0)
    tm = dest_ref.shape[2]
    n_slots = xs_hbm.shape[0]
    n_blocks = n_slots // MOE_BLOCK

    @pl.when(i == 0)
    def _():
        no_owner = pltpu.make_async_copy(spare_hbm, st_ref, zsem)
        no_owner.start()
        no_owner.wait()
        zbuf[...] = jnp.zeros_like(zbuf)

        def zero_block(first_slot):
            dst = xs_hbm.at[pl.ds(pl.multiple_of(first_slot, MOE_BLOCK), MOE_BLOCK)]
            return pltpu.make_async_copy(zbuf, dst, zsem)

        def for_each_zero_block(fn):
            for g in range(N_CLASSES):
                @pl.when(pend_ref[g] > pstart_ref[g])
                def _():
                    fn(zero_block(pend_ref[g] - MOE_BLOCK))

                @pl.when(nused_ref[0] + g < n_blocks)
                def _():
                    fn(zero_block((nused_ref[0] + g) * MOE_BLOCK))

        for_each_zero_block(lambda cp: cp.start())
        for_each_zero_block(lambda cp: cp.wait())

    def wait(j, c):
        _row_copy(x_ref, 0, xs_hbm, 0, sem).wait()
        return c

    for j in range(tm):
        d = dest_ref[0, 0, j]
        _row_copy(x_ref, j, xs_hbm, d, sem).start(priority=j % 2)
        st_ref[d] = i * tm + j
    lax.fori_loop(0, tm, wait, 0, unroll=8)


def _dispatch_call(xpk, dest3, pad_start, pad_end, n_used, n_slots):
    n = xpk.shape[0]
    tm = dest3.shape[2]
    grid_spec = pltpu.PrefetchScalarGridSpec(
        num_scalar_prefetch=3,
        grid=(n // tm,),
        in_specs=[pl.BlockSpec((1, 1, tm), lambda i, ps, pe, nu: (i, 0, 0), memory_space=pltpu.SMEM),
                  pl.BlockSpec((tm, ROW_W), lambda i, ps, pe, nu: (i, 0)),
                  pl.BlockSpec(memory_space=pl.ANY)],
        out_specs=[pl.BlockSpec(memory_space=pl.ANY), pl.BlockSpec(memory_space=pltpu.SMEM)],
        scratch_shapes=[pltpu.VMEM((MOE_BLOCK, ROW_W), jnp.int32), pltpu.SemaphoreType.DMA(()),
                        pltpu.SemaphoreType.DMA(())],
    )
    return pl.pallas_call(
        _dispatch_kernel,
        grid_spec=grid_spec,
        out_shape=[jax.ShapeDtypeStruct((n_slots, ROW_W), jnp.int32),
                   jax.ShapeDtypeStruct((n_slots,), jnp.int32)],
        compiler_params=_params("arbitrary"),
        name="dispatch",
    )(pad_start, pad_end, n_used, dest3, xpk, n + jnp.arange(n_slots, dtype=jnp.int32) % MOE_BLOCK)


def _experts_kernel(ba_ref, bb_ref, nused_ref, x_ref, st_prev_ref, st_cur_ref, wgu_a_ref, wgu_b_ref,
                    wd_a_ref, wd_b_ref, f_hbm, ybuf, sems, zsem):
    b = pl.program_id(0)
    last = pl.num_programs(0) - 1

    def scatter(st_ref, slot):
        for j in range(MOE_BLOCK):
            _row_copy(ybuf.at[slot], j, f_hbm, st_ref[0, 0, j], sems.at[slot]).start(priority=j % 2)

    def drain(slot):
        def wait(j, c):
            _row_copy(ybuf.at[slot], 0, f_hbm, 0, sems.at[slot]).wait()
            return c
        lax.fori_loop(0, MOE_BLOCK, wait, 0, unroll=8)

    def block_result():
        lo, hi = _unpack_bf16_pairs(x_ref[:, :D_MODEL // 2])
        lo, hi = lo.astype(BF16), hi.astype(BF16)
        gates = lax.bitcast_convert_type(x_ref[:, D_MODEL // 2:], F32)
        y = None
        for lane, (wgu_ref, wd_ref) in enumerate(((wgu_a_ref, wd_a_ref), (wgu_b_ref, wd_b_ref))):
            gu = _dot(lo, wgu_ref[0, :D_MODEL // 2, :]) + _dot(hi, wgu_ref[0, D_MODEL // 2:, :])
            gt = gu[:, :D_EXPERT]
            hid = gt * _sigmoid(gt) * gu[:, D_EXPERT:] * gates[:, lane:lane + 1]
            term = _dot(hid.astype(BF16), wd_ref[0])
            y = term if y is None else y + term
        return _pack_bf16_pairs(y)

    def step(slot):
        @pl.when(b == 0)
        def _():
            ybuf[1 - slot] = jnp.zeros(ybuf.shape[1:], ybuf.dtype)
            spare = f_hbm.at[pl.ds(f_hbm.shape[0] - MOE_BLOCK, MOE_BLOCK)]
            zero_spare = pltpu.make_async_copy(ybuf.at[1 - slot], spare, zsem)
            zero_spare.start()
            zero_spare.wait()
            ybuf[slot] = block_result()

        @pl.when(b >= 2)
        def _():
            drain(slot)

        @pl.when((b >= 1) & (b < nused_ref[0]))
        def _():
            scatter(st_prev_ref, 1 - slot)
            ybuf[slot] = block_result()

        @pl.when((b >= 1) & (b >= nused_ref[0]))
        def _():
            scatter(st_prev_ref, 1 - slot)
            ybuf[slot] = jnp.zeros(ybuf.shape[1:], ybuf.dtype)

        @pl.when(b == last)
        def _():
            @pl.when(b >= 1)
            def _():
                drain(1 - slot)

            scatter(st_cur_ref, slot)
            drain(slot)

    for slot in range(2):
        @pl.when(b % 2 == slot)
        def _():
            step(slot)


def _experts_call(xs, slot_tok, block_a, block_b, n_used, n_tok, w):
    n_blocks = block_a.shape[0]
    wgu_block = (1, D_MODEL, 2 * D_EXPERT)
    wd_block = (1, D_EXPERT, D_MODEL)
    st3 = slot_tok.reshape(n_blocks, 1, MOE_BLOCK)
    grid_spec = pltpu.PrefetchScalarGridSpec(
        num_scalar_prefetch=3,
        grid=(n_blocks,),
        in_specs=[pl.BlockSpec((MOE_BLOCK, ROW_W), lambda b, ba, bb, bv: (b, 0)),
                  pl.BlockSpec((1, 1, MOE_BLOCK), lambda b, ba, bb, bv: (jnp.maximum(b - 1, 0), 0, 0),
                               memory_space=pltpu.SMEM),
                  pl.BlockSpec((1, 1, MOE_BLOCK), lambda b, ba, bb, bv: (b, 0, 0), memory_space=pltpu.SMEM),
                  pl.BlockSpec(wgu_block, lambda b, ba, bb, bv: (ba[b], 0, 0)),
                  pl.BlockSpec(wgu_block, lambda b, ba, bb, bv: (bb[b], 0, 0)),
                  pl.BlockSpec(wd_block, lambda b, ba, bb, bv: (ba[b], 0, 0)),
                  pl.BlockSpec(wd_block, lambda b, ba, bb, bv: (bb[b], 0, 0))],
        out_specs=pl.BlockSpec(memory_space=pl.ANY),
        scratch_shapes=[pltpu.VMEM((2, MOE_BLOCK, D_MODEL // 2), jnp.int32), pltpu.SemaphoreType.DMA((2,)),
                        pltpu.SemaphoreType.DMA(())],
    )
    return pl.pallas_call(
        _experts_kernel,
        grid_spec=grid_spec,
        out_shape=jax.ShapeDtypeStruct((n_tok + MOE_BLOCK, D_MODEL // 2), jnp.int32),
        compiler_params=_params("arbitrary"),
        name="experts",
    )(block_a, block_b, n_used, xs, st3, st3, w['wgu'], w['wgu'], w['wd'], w['wd'])


def _combine_kernel(x1_ref, f_ref, lng_ref, lnb_ref, o_ref):
    f = jnp.concatenate(_unpack_bf16_pairs(f_ref[...]), axis=1)
    o_ref[...] = _layer_norm(DEEPNORM_ALPHA * x1_ref[...] + f, lng_ref[...], lnb_ref[...])


def _combine_call(x1, f, w):
    n = x1.shape[0]
    tm = COMBINE_TM
    return pl.pallas_call(
        _combine_kernel,
        grid=(n // tm,),
        in_specs=[pl.BlockSpec((tm, D_MODEL), lambda i: (i, 0)),
                  pl.BlockSpec((tm, D_MODEL // 2), lambda i: (i, 0)),
                  _const_spec(w['ln2_g'].shape), _const_spec(w['ln2_b'].shape)],
        out_specs=pl.BlockSpec((tm, D_MODEL), lambda i: (i, 0)),
        out_shape=jax.ShapeDtypeStruct((n, D_MODEL), F32),
        compiler_params=_params("parallel"),
        name="combine",
    )(x1, f, w['ln2_g'], w['ln2_b'])


def _class_experts():
    pairs = [(a, b) for a in range(EXPERTS_PER_GROUP) for b in range(a + 1, EXPERTS_PER_GROUP)]
    ea = [g * EXPERTS_PER_GROUP + a for g in range(N_GROUPS) for a, _ in pairs]
    eb = [g * EXPERTS_PER_GROUP + b for g in range(N_GROUPS) for _, b in pairs]
    return jnp.asarray(ea, jnp.int32), jnp.asarray(eb, jnp.int32)


def _dispatch_plan(cls):
    n = cls.shape[1]
    n_blocks = -(-n // MOE_BLOCK) + N_CLASSES
    onehot = (cls[0][:, None] == jnp.arange(N_CLASSES, dtype=jnp.int32)).astype(jnp.int32)
    incl = jnp.cumsum(onehot, axis=0)
    counts = incl[-1]
    padded = (counts + MOE_BLOCK - 1) // MOE_BLOCK * MOE_BLOCK
    pad_end = jnp.cumsum(padded).astype(jnp.int32)
    pad_start = pad_end - padded
    slot = pad_start[None, :] + (incl - onehot)
    dest = jnp.sum(onehot * slot, axis=1).astype(jnp.int32)
    dest3 = dest.reshape(n // ROUTE_TM, 1, ROUTE_TM)
    first_slot = jnp.arange(n_blocks, dtype=jnp.int32) * MOE_BLOCK
    block_c = jnp.minimum(jnp.sum(pad_end[None, :] <= first_slot[:, None], axis=1), N_CLASSES - 1)
    ea, eb = _class_experts()
    n_used = pad_end[-1:] // MOE_BLOCK
    return dest3, pad_start, pad_end, ea[block_c], eb[block_c], n_used, n_blocks * MOE_BLOCK


def _rope_tables(seq):
    inv = 1.0 / (ROPE_BASE ** (jnp.arange(0, QK_ROPE, 2, dtype=F32) / QK_ROPE))
    ang = jnp.arange(seq, dtype=F32)[:, None] * inv[None, :]
    pad = jnp.zeros((seq, LANES - QK_ROPE), F32)
    cos, sin = jnp.cos(ang), jnp.sin(ang)
    return jnp.concatenate([cos, cos, pad], -1), jnp.concatenate([sin, sin, pad], -1)


def _pad_cols(t, width):
    return jnp.pad(t, ((0, 0), (0, width - t.shape[1])))


def _rope_weight_pair(w_rope):
    half = QK_ROPE // 2
    swapped = jnp.concatenate([-w_rope[:, half:], w_rope[:, :half]], axis=1)
    return _pad_cols(w_rope, LANES), _pad_cols(swapped, LANES)


def _layer_weights(l, w_in, a_ln_g, a_ln_b, a_ws, a_bs, na_rpb, mla_q_norm, mla_kv_norm, mla_w_uq,
                   mla_w_ukv, w_br_a, w_br_b, w_br_c, w_o, ln1_g, ln1_b, ln2_g, ln2_b, w_gate, w_up,
                   w_down):
    sizes = (2 * A_WIDTH, NA_WIDTH, NA_WIDTH, NA_WIDTH, Q_LORA, KV_LORA, QK_ROPE, N_BRANCH * D_MODEL)
    cuts = np.cumsum(sizes)[:-1].tolist()
    wa, wq, wk, wv, wcq, wckv, wkr, wg = jnp.split(w_in[l], cuts, axis=-1)
    row = lambda t: t.reshape(1, -1).astype(F32)
    uq = mla_w_uq[l].reshape(Q_LORA, MLA_HEADS, QK_NOPE + QK_ROPE)
    wqn = uq[:, :, :QK_NOPE].reshape(Q_LORA, MLA_HEADS * QK_NOPE)
    rope_pairs = [_rope_weight_pair(uq[:, h, QK_NOPE:]) for h in range(MLA_HEADS)]
    wqr = jnp.concatenate([p[0] for p in rope_pairs], axis=1)
    wqs = jnp.concatenate([p[1] for p in rope_pairs], axis=1)
    ukv = mla_w_ukv[l].reshape(KV_LORA, MLA_HEADS, QK_NOPE + V_DIM)
    wukv = jnp.concatenate([ukv[:, :, :QK_NOPE].reshape(KV_LORA, -1), ukv[:, :, QK_NOPE:].reshape(KV_LORA, -1)], 1)
    wkr_p, wkrs_p = _rope_weight_pair(wkr)
    return {
        'wa': wa.astype(BF16), 'a_ln_g': row(a_ln_g[l]), 'a_ln_b': row(a_ln_b[l]),
        'ws': a_ws[l].astype(BF16),
        'bs': jnp.broadcast_to(a_bs[l][:, :, None], (A_GROUPS, CHUNK, CHUNK)).astype(F32),
        'wqkv': jnp.concatenate([wq * NA_SCALE, wk, wv], axis=1).astype(BF16),
        'wcq': wcq.astype(BF16), 'q_norm': row(mla_q_norm[l]),
        'wqn': wqn.astype(BF16), 'wqr': wqr.astype(BF16), 'wqs': wqs.astype(BF16),
        'wckv': wckv.astype(BF16), 'kv_norm': row(mla_kv_norm[l]), 'wukv': wukv.astype(BF16),
        'wkr': wkr_p.astype(BF16), 'wkrs': wkrs_p.astype(BF16),
        'na_bias': _natten_bias(na_rpb[l]), 'na_bmax': _natten_bias_max(na_rpb[l]),
        'wg': wg.astype(BF16),
        'wbr': jnp.stack([w_br_a[l], w_br_b[l], w_br_c[l]]).astype(BF16),
        'wo': w_o[l].astype(BF16), 'ln1_g': row(ln1_g[l]), 'ln1_b': row(ln1_b[l]),
        'ln2_g': row(ln2_g[l]), 'ln2_b': row(ln2_b[l]),
        'wgu': jnp.concatenate([w_gate[l], w_up[l]], axis=-1).astype(BF16),
        'wd': w_down[l].astype(BF16),
    }


def _shared_weights(w_router, router_bias):
    wr_t = w_router.T.astype(F32)
    wrh = wr_t.astype(BF16)
    wrl = (wr_t - wrh.astype(F32)).astype(BF16)
    rb = jnp.broadcast_to(router_bias.astype(F32)[:, None], (N_EXPERTS, MERGE_TM))
    return {'wrh': wrh, 'wrl': wrl, 'rb': rb}


def _trunk(x, layers, shared):
    bsz, seq, _ = x.shape
    cos2, sin2 = _rope_tables(seq)
    x2d = x.reshape(bsz * seq, D_MODEL)
    moe = None
    for w in layers:
        if moe is None:
            oa, q, k, v, qm, km, vm, kn2 = _pre_call(x2d, seq, cos2, sin2, w)
        else:
            oa, q, k, v, qm, km, vm, kn2, x2d = _pre_call(None, seq, cos2, sin2, w, moe)
        ob = _natten_call(q, k, v, kn2, w['na_bias'], w['na_bmax'], bsz, seq)
        oc = _mla_call(qm, km, vm, bsz, seq)
        x1, xpk, cls = _merge_call(x2d, oa, ob, oc, w, shared)
        dest3, pad_start, pad_end, block_a, block_b, n_used, n_slots = _dispatch_plan(cls)
        xs, slot_tok = _dispatch_call(xpk, dest3, pad_start, pad_end, n_used, n_slots)
        f = _experts_call(xs, slot_tok, block_a, block_b, n_used, x1.shape[0], w)
        moe = (x1, f, w)
    return _combine_call(x1, f, w).reshape(bsz, seq, D_MODEL)


def kernel(x_prompt, x_sample, w_in, a_ln_g, a_ln_b, a_ws, a_bs, na_rpb, mla_q_norm, mla_kv_norm, mla_w_uq, mla_w_ukv, w_br_a, w_br_b, w_br_c, w_o, ln1_g, ln1_b, ln2_g, ln2_b, w_router, router_bias, w_gate, w_up, w_down):
    layers = [_layer_weights(l, w_in, a_ln_g, a_ln_b, a_ws, a_bs, na_rpb, mla_q_norm, mla_kv_norm,
                             mla_w_uq, mla_w_ukv, w_br_a, w_br_b, w_br_c, w_o, ln1_g, ln1_b, ln2_g,
                             ln2_b, w_gate, w_up, w_down) for l in range(DEPTH)]
    shared = _shared_weights(w_router, router_bias)
    return (_trunk(x_prompt, layers, shared), _trunk(x_sample, layers, shared))
```

```python
import functools
import math

import jax
import jax.numpy as jnp
import numpy as np
from jax import lax
from jax.experimental import pallas as pl
from jax.experimental.pallas import tpu as pltpu

F32 = jnp.float32
BF16 = jnp.bfloat16

D_MODEL = 1024
DEPTH = 2
GRID_W = 64
CHUNK = 128
A_WIDTH = 512
A_GROUPS = 4
NA_HEADS = 8
NA_HEAD_DIM = 64
NA_WIN_H = 8
NA_WIN_W = 16
NA_WIDTH = NA_HEADS * NA_HEAD_DIM
NA_SCALE = NA_HEAD_DIM ** -0.5
MLA_HEADS = 4
Q_LORA = 384
KV_LORA = 256
QK_NOPE = 128
QK_ROPE = 64
V_DIM = 128
MLA_WIDTH = MLA_HEADS * V_DIM
MLA_SCALE = (QK_NOPE + QK_ROPE) ** -0.5
ROPE_BASE = 10000.0
N_BRANCH = 3
N_EXPERTS = 16
N_GROUPS = 4
EXPERTS_PER_GROUP = N_EXPERTS // N_GROUPS
TOP_K = 2
D_EXPERT = 256
MOE_BLOCK = 256
DEEPNORM_ALPHA = (2 * DEPTH) ** 0.25
LN_EPS = 1e-5
RMS_EPS = 1e-6
NEG_INF = -1e30
LOG2E = math.log2(math.e)

VMEM_LIMIT_BYTES = 60 * 1024 * 1024
LANES = 128
MLA_HEAD_PAD = 2 * LANES
NA_ROW_BLOCK = 8
NA_TOK_BLOCK = NA_ROW_BLOCK * GRID_W

ROW_W = D_MODEL // 2 + LANES
PAIRS_PER_GROUP = EXPERTS_PER_GROUP * (EXPERTS_PER_GROUP - 1) // 2
N_CLASSES = N_GROUPS * PAIRS_PER_GROUP
HIGH_HALF = -65536
PRE_TM = 1024
MERGE_TM = 1024
ROUTE_TM = 256
COMBINE_TM = 512
MLA_TQ = 2048
MLA_TK = 1024
MLA_MAX_UNROLL = 5
MLA_SHIFT_LANE = QK_NOPE + QK_ROPE
MLA_L_MIN = 2.0 ** -60
MLA_L_MAX = 2.0 ** 100
MLA_BOUND_SLACK = 1.0 + 2.0 ** -7
NA_BOUND_SLACK = 1.0 + 2.0 ** -6
NA_L_MIN = 2.0 ** -60
NA_L_MAX = 2.0 ** 100


def _params(*sem):
    return pltpu.CompilerParams(dimension_semantics=sem, vmem_limit_bytes=VMEM_LIMIT_BYTES)


def _const_spec(shape):
    nd = len(shape)
    return pl.BlockSpec(shape, lambda *_: (0,) * nd, pipeline_mode=pl.Buffered(1))


def _layer_norm(y, g, b):
    mu = jnp.mean(y, -1, keepdims=True)
    yc = y - mu
    var = jnp.mean(yc * yc, -1, keepdims=True)
    return yc * lax.rsqrt(var + LN_EPS) * g + b


def _rms_norm(y, g):
    return y * lax.rsqrt(jnp.mean(y * y, -1, keepdims=True) + RMS_EPS) * g


def _gelu_tanh(x):
    return 0.5 * x * (1.0 + jnp.tanh(math.sqrt(2.0 / math.pi) * (x + 0.044715 * (x * x * x))))


def _sigmoid(x):
    return 1.0 / (1.0 + jnp.exp(-x))


def _pack_bf16_pairs(y):
    w = y.shape[1] // 2
    lo = lax.bitcast_convert_type(y[:, :w].astype(BF16).astype(F32), jnp.int32)
    hi = lax.bitcast_convert_type(y[:, w:].astype(BF16).astype(F32), jnp.int32)
    return lax.shift_right_logical(lo, 16) | (hi & HIGH_HALF)


def _unpack_bf16_pairs(p):
    lo = lax.bitcast_convert_type(lax.shift_left(p, 16), F32)
    hi = lax.bitcast_convert_type(p & HIGH_HALF, F32)
    return lo, hi


def _dot(a, b):
    return jnp.dot(a, b, preferred_element_type=F32)


def _dot_nt(a, b):
    return lax.dot_general(a, b, (((1,), (1,)), ((), ())), preferred_element_type=F32)


def _pre_kernel(*refs, after_moe):
    if after_moe:
        x1_ref, f_ref, ln2g_ref, ln2b_ref = refs[:4]
        refs, x2_ref = refs[4:-1], refs[-1]
        f = jnp.concatenate(_unpack_bf16_pairs(f_ref[...]), axis=1)
        x = _layer_norm(DEEPNORM_ALPHA * x1_ref[...] + f, ln2g_ref[...], ln2b_ref[...])
        x2_ref[...] = x
    else:
        x, refs = refs[0][...], refs[1:]
    (cos_ref, sin_ref, wa_ref, lng_ref, lnb_ref, ws_ref, bs_ref, wqkv_ref, wcq_ref, qnorm_ref, wqn_ref,
     wqr_ref, wqs_ref, wckv_ref, kvnorm_ref, wukv_ref, wkr_ref, wkrs_ref, hsum_ref,
     oa_ref, q_ref, k_ref, v_ref, qm_ref, km_ref, vm_ref, kn2_ref) = refs
    tm = x.shape[0]
    xb = x.astype(BF16)
    cos2 = cos_ref[...]
    sin2 = sin_ref[...]

    a = _gelu_tanh(_dot(xb, wa_ref[...]))
    u = a[:, :A_WIDTH]
    vn = _layer_norm(a[:, A_WIDTH:], lng_ref[...], lnb_ref[...]).astype(BF16)
    for c in range(tm // CHUNK):
        rs = slice(c * CHUNK, (c + 1) * CHUNK)
        for g in range(A_GROUPS):
            cs = slice(g * CHUNK, (g + 1) * CHUNK)
            sp = _dot(ws_ref[g], vn[rs, cs]) + bs_ref[g]
            oa_ref[rs, cs] = (u[rs, cs] * sp).astype(BF16)

    qkv = _dot(xb, wqkv_ref[...])
    q_ref[...] = qkv[:, :NA_WIDTH].astype(BF16)
    kb = qkv[:, NA_WIDTH:2 * NA_WIDTH].astype(BF16)
    k_ref[...] = kb
    v_ref[...] = qkv[:, 2 * NA_WIDTH:].astype(BF16)
    kf = kb.astype(F32)
    kn2_ref[...] = _dot((kf * kf).astype(BF16), hsum_ref[...])

    cqn = _rms_norm(_dot(xb, wcq_ref[...]), qnorm_ref[...]).astype(BF16)
    qn = _dot(cqn, wqn_ref[...])
    qr = _dot(cqn, wqr_ref[...])
    qs = _dot(cqn, wqs_ref[...])
    qscale = MLA_SCALE * LOG2E
    shift_lane = lax.broadcasted_iota(jnp.int32, (tm, LANES), 1) == MLA_SHIFT_LANE - QK_NOPE
    for h in range(MLA_HEADS):
        hs = slice(h * LANES, (h + 1) * LANES)
        nope = (qn[:, hs] * qscale).astype(BF16)
        rot = ((qr[:, hs] * cos2 + qs[:, hs] * sin2) * qscale).astype(BF16)
        nf, rf = nope.astype(F32), rot.astype(F32)
        norm = jnp.sqrt(jnp.sum(nf * nf, -1, keepdims=True) + jnp.sum(rf * rf, -1, keepdims=True))
        norm = jnp.broadcast_to(norm * MLA_BOUND_SLACK, rot.shape).astype(BF16)
        qm_ref[:, h * MLA_HEAD_PAD:h * MLA_HEAD_PAD + LANES] = nope
        qm_ref[:, h * MLA_HEAD_PAD + LANES:(h + 1) * MLA_HEAD_PAD] = jnp.where(shift_lane, norm, rot)

    ckvn = _rms_norm(_dot(xb, wckv_ref[...]), kvnorm_ref[...]).astype(BF16)
    kv = _dot(ckvn, wukv_ref[...])
    krope = _dot(xb, wkr_ref[...]) * cos2 + _dot(xb, wkrs_ref[...]) * sin2
    lane = lax.broadcasted_iota(jnp.int32, krope.shape, 1)
    krope = jnp.where(lane == MLA_SHIFT_LANE - QK_NOPE, -1.0, krope).astype(BF16)
    for h in range(MLA_HEADS):
        km_ref[:, h * MLA_HEAD_PAD:h * MLA_HEAD_PAD + LANES] = kv[:, h * LANES:(h + 1) * LANES].astype(BF16)
        km_ref[:, h * MLA_HEAD_PAD + LANES:(h + 1) * MLA_HEAD_PAD] = krope
    vm_ref[...] = kv[:, MLA_HEADS * QK_NOPE:].astype(BF16)


def _head_sum_matrix():
    feat = np.arange(NA_WIDTH)[:, None] // NA_HEAD_DIM
    return jnp.asarray(feat == np.arange(LANES)[None, :], BF16)


def _pre_call(x2d, seq, cos2, sin2, w, moe=None):
    n = x2d.shape[0] if moe is None else moe[0].shape[0]
    tm = PRE_TM
    per_seq = seq // tm
    weights = [w['wa'], w['a_ln_g'], w['a_ln_b'], w['ws'], w['bs'], w['wqkv'], w['wcq'], w['q_norm'],
               w['wqn'], w['wqr'], w['wqs'], w['wckv'], w['kv_norm'], w['wukv'], w['wkr'], w['wkrs'],
               _head_sum_matrix()]
    tok = lambda width: pl.BlockSpec((tm, width), lambda i: (i, 0))
    pos = pl.BlockSpec((tm, LANES), lambda i: (i % per_seq, 0))
    out_widths = [A_WIDTH, NA_WIDTH, NA_WIDTH, NA_WIDTH, MLA_HEADS * MLA_HEAD_PAD,
                  MLA_HEADS * MLA_HEAD_PAD, MLA_WIDTH]
    out_specs = [tok(wd) for wd in out_widths] + [tok(LANES)]
    out_shape = ([jax.ShapeDtypeStruct((n, wd), BF16) for wd in out_widths]
                 + [jax.ShapeDtypeStruct((n, LANES), F32)])
    if moe is None:
        stream, stream_specs = [x2d], [tok(D_MODEL)]
    else:
        x1, f, prev = moe
        stream = [x1, f, prev['ln2_g'], prev['ln2_b']]
        stream_specs = [tok(D_MODEL), tok(D_MODEL // 2), _const_spec(prev['ln2_g'].shape),
                        _const_spec(prev['ln2_b'].shape)]
        out_specs.append(tok(D_MODEL))
        out_shape.append(jax.ShapeDtypeStruct((n, D_MODEL), F32))
    return pl.pallas_call(
        functools.partial(_pre_kernel, after_moe=moe is not None),
        grid=(n // tm,),
        in_specs=stream_specs + [pos, pos] + [_const_spec(t.shape) for t in weights],
        out_specs=out_specs,
        out_shape=out_shape,
        compiler_params=_params("parallel"),
        name="pre",
    )(*stream, cos2, sin2, *weights)


def _natten_kernel(q_ref, kp_ref, kc_ref, kn_ref, vp_ref, vc_ref, vn_ref, np_ref, nc_ref, nn_ref,
                   bias_ref, bmax_ref, o_ref, kcat, vcat, *, rows):
    j = pl.program_id(1)
    nt = NA_TOK_BLOCK
    kcat[0:nt] = kp_ref[...]
    kcat[nt:2 * nt] = kc_ref[...]
    kcat[2 * nt:3 * nt] = kn_ref[...]
    vcat[0:nt] = vp_ref[...]
    vcat[nt:2 * nt] = vc_ref[...]
    vcat[2 * nt:3 * nt] = vn_ref[...]
    r0 = j * NA_ROW_BLOCK
    lane_lo = lax.broadcasted_iota(jnp.int32, (GRID_W, LANES), 1) < NA_HEAD_DIM

    kmax2 = jnp.maximum(jnp.maximum(jnp.max(np_ref[...], 0, keepdims=True),
                                    jnp.max(nc_ref[...], 0, keepdims=True)),
                        jnp.max(nn_ref[...], 0, keepdims=True))
    kmax = jnp.broadcast_to(jnp.sqrt(kmax2) * NA_BOUND_SLACK, (2 * GRID_W, LANES))
    row_lo = lax.broadcasted_iota(jnp.int32, (2 * GRID_W, LANES), 0) < GRID_W
    lane = lax.broadcasted_iota(jnp.int32, (2 * GRID_W, LANES), 1)
    kcols = [jnp.sum(jnp.where(lane == jnp.where(row_lo, 2 * hp, 2 * hp + 1), kmax, 0.0), -1, keepdims=True)
             for hp in range(NA_HEADS // 2)]

    def attend(exact):
        bad = jnp.zeros((2 * GRID_W, 1), F32)
        for i in range(NA_ROW_BLOCK):
            r = r0 + i
            start = jnp.clip(r - NA_WIN_H // 2, 0, rows - NA_WIN_H)
            pat = start - r + (NA_WIN_H - 1)
            koff = pl.multiple_of((start - r0 + NA_ROW_BLOCK) * GRID_W, GRID_W)
            qs = slice(i * GRID_W, (i + 1) * GRID_W)
            for hp in range(NA_HEADS // 2):
                cs = slice(hp * LANES, (hp + 1) * LANES)
                qp = q_ref[qs, cs]
                zero = jnp.zeros_like(qp)
                q2 = jnp.concatenate([jnp.where(lane_lo, qp, zero), jnp.where(lane_lo, zero, qp)], axis=0)
                kk = kcat[pl.ds(koff, NA_WIN_H * GRID_W), cs]
                vv = vcat[pl.ds(koff, NA_WIN_H * GRID_W), cs]
                s = _dot_nt(q2, kk) + bias_ref[pat, hp]
                if exact:
                    m = jnp.max(s, -1, keepdims=True)
                else:
                    qf = q2.astype(F32)
                    m = jnp.sqrt(jnp.sum(qf * qf, -1, keepdims=True)) * kcols[hp] + bmax_ref[hp][:, 0:1]
                e = jnp.exp(s - m)
                l = jnp.sum(e, -1, keepdims=True)
                if not exact:
                    bad = bad + jnp.where((l >= NA_L_MIN) & (l <= NA_L_MAX), 0.0, 1.0)
                o2 = _dot(e.astype(BF16), vv) / l
                o_ref[qs, cs] = jnp.where(lane_lo, o2[:GRID_W], o2[GRID_W:]).astype(BF16)
        return bad

    n_bad = jnp.sum(attend(exact=False))

    @pl.when(n_bad > 0.0)
    def _():
        attend(exact=True)


def _natten_call(q, k, v, knorm, bias, bmax, bsz, seq):
    rows = seq // GRID_W
    nrb = rows // NA_ROW_BLOCK
    nt = NA_TOK_BLOCK
    cur = lambda width: pl.BlockSpec((nt, width), lambda b, j: (b * nrb + j, 0))
    prev = lambda width: pl.BlockSpec((nt, width), lambda b, j: (b * nrb + jnp.maximum(j - 1, 0), 0))
    nxt = lambda width: pl.BlockSpec((nt, width), lambda b, j: (b * nrb + jnp.minimum(j + 1, nrb - 1), 0))
    w = NA_WIDTH
    return pl.pallas_call(
        functools.partial(_natten_kernel, rows=rows),
        grid=(bsz, nrb),
        in_specs=[cur(w), prev(w), cur(w), nxt(w), prev(w), cur(w), nxt(w),
                  prev(LANES), cur(LANES), nxt(LANES), _const_spec(bias.shape), _const_spec(bmax.shape)],
        out_specs=cur(w),
        out_shape=jax.ShapeDtypeStruct((bsz * seq, NA_WIDTH), BF16),
        scratch_shapes=[pltpu.VMEM((3 * nt, NA_WIDTH), BF16), pltpu.VMEM((3 * nt, NA_WIDTH), BF16)],
        compiler_params=_params("parallel", "parallel"),
        name="natten",
    )(q, k, k, k, v, v, v, knorm, knorm, knorm, bias, bmax)


def _natten_bias_max(rpb):
    m = jnp.max(rpb.astype(F32), axis=(1, 2)).reshape(NA_HEADS // 2, 2, 1, 1)
    return jnp.broadcast_to(m, (NA_HEADS // 2, 2, GRID_W, LANES)).reshape(NA_HEADS // 2, 2 * GRID_W, LANES)


def _natten_bias(rpb):
    c = jnp.arange(GRID_W)
    col_start = jnp.clip(c - NA_WIN_W // 2, 0, GRID_W - NA_WIN_W)
    in_win = (c[None, :] >= col_start[:, None]) & (c[None, :] < col_start[:, None] + NA_WIN_W)
    dc = jnp.clip(c[None, :] - c[:, None] + (NA_WIN_W - 1), 0, 2 * NA_WIN_W - 2)
    onehot = (dc[:, :, None] == jnp.arange(2 * NA_WIN_W - 1)).astype(F32)
    t = jnp.einsum('hrd,qkd->hrqk', rpb.astype(F32), onehot, precision=lax.Precision.HIGHEST)
    t = jnp.where(in_win[None, None], t, NEG_INF)
    b = jnp.stack([t[:, p:p + NA_WIN_H] for p in range(NA_WIN_H)])
    b = b.transpose(0, 1, 3, 2, 4)
    return b.reshape(NA_WIN_H, NA_HEADS // 2, 2 * GRID_W, NA_WIN_H * GRID_W)


def _mla_online_softmax(q, k_ref, v_ref, tk):
    tq = q.shape[0]

    def body(j, carry):
        m, l, acc = carry
        ks = pl.ds(pl.multiple_of(j * tk, tk), tk)
        s = _dot_nt(q, k_ref[ks, :])
        m_new = jnp.maximum(m, jnp.max(s, -1, keepdims=True))
        alpha = jnp.exp2(m - m_new)
        p = jnp.exp2(s - m_new)
        l = alpha * l + jnp.sum(p, -1, keepdims=True)
        acc = alpha * acc + _dot(p.astype(BF16), v_ref[ks, :])
        return m_new, l, acc

    init = (jnp.full((tq, 1), NEG_INF, F32), jnp.zeros((tq, 1), F32), jnp.zeros((tq, V_DIM), F32))
    _, l, acc = lax.fori_loop(0, k_ref.shape[0] // tk, body, init)
    return acc, l


def _mla_kernel(q_ref, k_ref, v_ref, o_ref, kmax_ref, *, tk, unroll):
    q = q_ref[...]
    tq = q.shape[0]
    seq = k_ref.shape[0]

    @pl.when(pl.program_id(2) == 0)
    def _():
        def body(c, mx):
            kc = k_ref[pl.ds(pl.multiple_of(c * tk, tk), tk), :].astype(F32)
            n2 = jnp.sum(kc * kc, -1, keepdims=True)
            return jnp.maximum(mx, jnp.max(n2, 0, keepdims=True))

        mx = lax.fori_loop(0, seq // tk, body, jnp.zeros((1, 1), F32))
        kmax = jnp.broadcast_to(jnp.sqrt(mx) * MLA_BOUND_SLACK, kmax_ref.shape)
        lane = lax.broadcasted_iota(jnp.int32, kmax_ref.shape, 1)
        kmax_ref[...] = jnp.where(lane == MLA_SHIFT_LANE, kmax, 1.0).astype(BF16)

    q_shift = q * kmax_ref[0:1, :]
    ones = jnp.ones((tk, LANES), BF16)

    def body(j, acc):
        ks = pl.ds(pl.multiple_of(j * tk, tk), tk)
        p = jnp.exp2(_dot_nt(q_shift, k_ref[ks, :])).astype(BF16)
        return acc + _dot(p, jnp.concatenate([v_ref[ks, :], ones], axis=1))

    nk = seq // tk
    acc = lax.fori_loop(0, nk - 1, body, jnp.zeros((tq, V_DIM + LANES), F32), unroll=unroll)
    l_part = acc[:, V_DIM:]
    usable = (l_part >= MLA_L_MIN) & (l_part <= MLA_L_MAX)
    n_bad = jnp.sum(jnp.where(usable, 0.0, 1.0))
    acc = body(nk - 1, acc)
    l = acc[:, V_DIM:]
    o_ref[...] = (acc[:, :V_DIM] / l).astype(BF16)

    @pl.when(n_bad > 0.0)
    def _():
        acc, l = _mla_online_softmax(q, k_ref, v_ref, tk)
        o_ref[...] = (acc / l).astype(BF16)


def _mla_call(qm, km, vm, bsz, seq):
    tq = min(MLA_TQ, seq)
    nq = seq // tq
    tk = min(MLA_TK, seq // 2)
    looped = seq // tk - 1
    unroll = max(u for u in range(1, MLA_MAX_UNROLL + 1) if looped % u == 0)
    return pl.pallas_call(
        functools.partial(_mla_kernel, tk=tk, unroll=unroll),
        grid=(bsz, MLA_HEADS, nq),
        in_specs=[pl.BlockSpec((tq, MLA_HEAD_PAD), lambda b, h, i: (b * nq + i, h)),
                  pl.BlockSpec((seq, MLA_HEAD_PAD), lambda b, h, i: (b, h)),
                  pl.BlockSpec((seq, V_DIM), lambda b, h, i: (b, h))],
        out_specs=pl.BlockSpec((tq, V_DIM), lambda b, h, i: (b * nq + i, h)),
        out_shape=jax.ShapeDtypeStruct((bsz * seq, MLA_WIDTH), BF16),
        scratch_shapes=[pltpu.VMEM((16, MLA_HEAD_PAD), BF16)],
        compiler_params=_params("parallel", "parallel", "arbitrary"),
        name="mla",
    )(qm, km, vm)


def _first_argmax(vals):
    best, idx = vals[0], jnp.zeros(vals[0].shape, jnp.int32)
    for i in range(1, len(vals)):
        better = vals[i] > best
        idx = jnp.where(better, i, idx)
        best = jnp.where(better, vals[i], best)
    return best, idx


def _select(idx, vals):
    out = vals[0]
    for i in range(1, len(vals)):
        out = jnp.where(idx == i, vals[i], out)
    return out


def _merge_kernel(x_ref, oa_ref, ob_ref, oc_ref, wg_ref, wbr_ref, wo_ref, lng_ref, lnb_ref,
                  wrh_ref, wrl_ref, rb_ref, x1_ref, xpk_ref, cls_ref):
    x = x_ref[...]
    xb = x.astype(BF16)
    merged = None
    for i, o_ref in enumerate((oa_ref, ob_ref, oc_ref)):
        g = _sigmoid(_dot(xb, wg_ref[:, i * D_MODEL:(i + 1) * D_MODEL]))
        term = g * _dot(o_ref[...], wbr_ref[i])
        merged = term if merged is None else merged + term
    m = _dot(merged.astype(BF16), wo_ref[...])
    x1 = _layer_norm(DEEPNORM_ALPHA * x + m, lng_ref[...], lnb_ref[...])
    x1_ref[...] = x1

    x1h = x1.astype(BF16)
    x1l = (x1 - x1h.astype(F32)).astype(BF16)
    logits = _dot_nt(wrh_ref[...], x1h) + (_dot_nt(wrl_ref[...], x1h) + _dot_nt(wrh_ref[...], x1l))
    scores = _sigmoid(logits)
    biased = scores + rb_ref[...]
    sc = [scores[e:e + 1, :] for e in range(N_EXPERTS)]
    bi = [biased[e:e + 1, :] for e in range(N_EXPERTS)]
    grp_scores = []
    for g in range(N_GROUPS):
        v = bi[g * EXPERTS_PER_GROUP:(g + 1) * EXPERTS_PER_GROUP]
        top2 = None
        for i in range(EXPERTS_PER_GROUP):
            for k in range(i + 1, EXPERTS_PER_GROUP):
                top2 = v[i] + v[k] if top2 is None else jnp.maximum(top2, v[i] + v[k])
        grp_scores.append(top2)
    _, grp = _first_argmax(grp_scores)
    cand = [_select(grp, [bi[g * EXPERTS_PER_GROUP + i] for g in range(N_GROUPS)])
            for i in range(EXPERTS_PER_GROUP)]
    cand_sc = [_select(grp, [sc[g * EXPERTS_PER_GROUP + i] for g in range(N_GROUPS)])
               for i in range(EXPERTS_PER_GROUP)]
    _, i1 = _first_argmax(cand)
    _, i2 = _first_argmax([jnp.where(i1 == i, -jnp.inf, cand[i]) for i in range(EXPERTS_PER_GROUP)])
    s1 = _select(i1, cand_sc)
    s2 = _select(i2, cand_sc)
    tot = s1 + s2
    g1 = s1 / tot
    g2 = s2 / tot
    first_lower = i1 < i2
    a = jnp.where(first_lower, i1, i2)
    b = jnp.where(first_lower, i2, i1)
    cls_ref[...] = grp * PAIRS_PER_GROUP + (((a * (7 - a)) >> 1) + (b - a - 1))
    xpk_ref[:, :D_MODEL // 2] = _pack_bf16_pairs(x1)
    tm = x.shape[0]
    sub = lax.broadcasted_iota(jnp.int32, (LANES, tm), 0)
    ga = jnp.broadcast_to(jnp.where(first_lower, g1, g2), (LANES, tm))
    gb = jnp.broadcast_to(jnp.where(first_lower, g2, g1), (LANES, tm))
    gmat = jnp.where(sub == 0, ga, jnp.where(sub == 1, gb, 0.0))
    xpk_ref[:, D_MODEL // 2:] = lax.bitcast_convert_type(gmat.T, jnp.int32)


def _merge_call(x2d, oa, ob, oc, w, shared):
    n = x2d.shape[0]
    tm = MERGE_TM
    weights = [w['wg'], w['wbr'], w['wo'], w['ln1_g'], w['ln1_b'], shared['wrh'], shared['wrl'], shared['rb']]
    tok = lambda width: pl.BlockSpec((tm, width), lambda i: (i, 0))
    return pl.pallas_call(
        _merge_kernel,
        grid=(n // tm,),
        in_specs=[tok(D_MODEL), tok(A_WIDTH), tok(NA_WIDTH), tok(MLA_WIDTH)]
                 + [_const_spec(t.shape) for t in weights],
        out_specs=[tok(D_MODEL), tok(ROW_W), pl.BlockSpec((1, tm), lambda i: (0, i))],
        out_shape=[jax.ShapeDtypeStruct((n, D_MODEL), F32),
                   jax.ShapeDtypeStruct((n, ROW_W), jnp.int32),
                   jax.ShapeDtypeStruct((1, n), jnp.int32)],
        compiler_params=_params("parallel"),
        name="merge",
    )(x2d, oa, ob, oc, *weights)


def _row_copy(src, src_row, dst, dst_row, sem):
    return pltpu.make_async_copy(src.at[pl.ds(src_row, 1)], dst.at[pl.ds(dst_row, 1)], sem)


def _dispatch_kernel(pstart_ref, pend_ref, nused_ref, dest_ref, x_ref, spare_hbm, xs_hbm, st_ref, zbuf, zsem, sem):
    i = pl.program_id(0)
    tm = dest_ref.shape[2]
    n_slots = xs_hbm.shape[0]
    n_blocks = n_slots // MOE_BLOCK

    @pl.when(i == 0)
    def _():
        no_owner = pltpu.make_async_copy(spare_hbm, st_ref, zsem)
        no_owner.start()
        no_owner.wait()
        zbuf[...] = jnp.zeros_like(zbuf)

        def zero_block(first_slot):
            dst = xs_hbm.at[pl.ds(pl.multiple_of(first_slot, MOE_BLOCK), MOE_BLOCK)]
            return pltpu.make_async_copy(zbuf, dst, zsem)

        def for_each_zero_block(fn):
            for g in range(N_CLASSES):
                @pl.when(pend_ref[g] > pstart_ref[g])
                def _():
                    fn(zero_block(pend_ref[g] - MOE_BLOCK))

                @pl.when(nused_ref[0] + g < n_blocks)
                def _():
                    fn(zero_block((nused_ref[0] + g) * MOE_BLOCK))

        for_each_zero_block(lambda cp: cp.start())
        for_each_zero_block(lambda cp: cp.wait())

    def wait(j, c):
        _row_copy(x_ref, 0, xs_hbm, 0, sem).wait()
        return c

    for j in range(tm):
        d = dest_ref[0, 0, j]
        _row_copy(x_ref, j, xs_hbm, d, sem).start(priority=j % 2)
        st_ref[d] = i * tm + j
    lax.fori_loop(0, tm, wait, 0, unroll=8)


def _dispatch_call(xpk, dest3, pad_start, pad_end, n_used, n_slots):
    n = xpk.shape[0]
    tm = dest3.shape[2]
    grid_spec = pltpu.PrefetchScalarGridSpec(
        num_scalar_prefetch=3,
        grid=(n // tm,),
        in_specs=[pl.BlockSpec((1, 1, tm), lambda i, ps, pe, nu: (i, 0, 0), memory_space=pltpu.SMEM),
                  pl.BlockSpec((tm, ROW_W), lambda i, ps, pe, nu: (i, 0)),
                  pl.BlockSpec(memory_space=pl.ANY)],
        out_specs=[pl.BlockSpec(memory_space=pl.ANY), pl.BlockSpec(memory_space=pltpu.SMEM)],
        scratch_shapes=[pltpu.VMEM((MOE_BLOCK, ROW_W), jnp.int32), pltpu.SemaphoreType.DMA(()),
                        pltpu.SemaphoreType.DMA(())],
    )
    return pl.pallas_call(
        _dispatch_kernel,
        grid_spec=grid_spec,
        out_shape=[jax.ShapeDtypeStruct((n_slots, ROW_W), jnp.int32),
                   jax.ShapeDtypeStruct((n_slots,), jnp.int32)],
        compiler_params=_params("arbitrary"),
        name="dispatch",
    )(pad_start, pad_end, n_used, dest3, xpk, n + jnp.arange(n_slots, dtype=jnp.int32) % MOE_BLOCK)


def _experts_kernel(ba_ref, bb_ref, nused_ref, x_ref, st_prev_ref, st_cur_ref, wgu_a_ref, wgu_b_ref,
                    wd_a_ref, wd_b_ref, f_hbm, ybuf, sems, zsem):
    b = pl.program_id(0)
    last = pl.num_programs(0) - 1

    def scatter(st_ref, slot):
        for j in range(MOE_BLOCK):
            _row_copy(ybuf.at[slot], j, f_hbm, st_ref[0, 0, j], sems.at[slot]).start(priority=j % 2)

    def drain(slot):
        def wait(j, c):
            _row_copy(ybuf.at[slot], 0, f_hbm, 0, sems.at[slot]).wait()
            return c
        lax.fori_loop(0, MOE_BLOCK, wait, 0, unroll=8)

    def block_result():
        lo, hi = _unpack_bf16_pairs(x_ref[:, :D_MODEL // 2])
        lo, hi = lo.astype(BF16), hi.astype(BF16)
        gates = lax.bitcast_convert_type(x_ref[:, D_MODEL // 2:], F32)
        y = None
        for lane, (wgu_ref, wd_ref) in enumerate(((wgu_a_ref, wd_a_ref), (wgu_b_ref, wd_b_ref))):
            gu = _dot(lo, wgu_ref[0, :D_MODEL // 2, :]) + _dot(hi, wgu_ref[0, D_MODEL // 2:, :])
            gt = gu[:, :D_EXPERT]
            hid = gt * _sigmoid(gt) * gu[:, D_EXPERT:] * gates[:, lane:lane + 1]
            term = _dot(hid.astype(BF16), wd_ref[0])
            y = term if y is None else y + term
        return _pack_bf16_pairs(y)

    def step(slot):
        @pl.when(b == 0)
        def _():
            ybuf[1 - slot] = jnp.zeros(ybuf.shape[1:], ybuf.dtype)
            spare = f_hbm.at[pl.ds(f_hbm.shape[0] - MOE_BLOCK, MOE_BLOCK)]
            zero_spare = pltpu.make_async_copy(ybuf.at[1 - slot], spare, zsem)
            zero_spare.start()
            zero_spare.wait()
            ybuf[slot] = block_result()

        n_used = nused_ref[0]

        @pl.when((b >= 2) & (b - 2 < n_used))
        def _():
            drain(slot)

        @pl.when((b >= 1) & (b < n_used))
        def _():
            scatter(st_prev_ref, 1 - slot)
            ybuf[slot] = block_result()

        @pl.when((b >= 1) & (b == n_used))
        def _():
            scatter(st_prev_ref, 1 - slot)

        @pl.when(b == last)
        def _():
            @pl.when((b >= 1) & (b - 1 < n_used))
            def _():
                drain(1 - slot)

            @pl.when(b < n_used)
            def _():
                scatter(st_cur_ref, slot)
                drain(slot)

    for slot in range(2):
        @pl.when(b % 2 == slot)
        def _():
            step(slot)


def _experts_call(xs, slot_tok, block_a, block_b, n_used, n_tok, w):
    n_blocks = block_a.shape[0]
    wgu_block = (1, D_MODEL, 2 * D_EXPERT)
    wd_block = (1, D_EXPERT, D_MODEL)
    st3 = slot_tok.reshape(n_blocks, 1, MOE_BLOCK)
    grid_spec = pltpu.PrefetchScalarGridSpec(
        num_scalar_prefetch=3,
        grid=(n_blocks,),
        in_specs=[pl.BlockSpec((MOE_BLOCK, ROW_W), lambda b, ba, bb, bv: (b, 0)),
                  pl.BlockSpec((1, 1, MOE_BLOCK), lambda b, ba, bb, bv: (jnp.maximum(b - 1, 0), 0, 0),
                               memory_space=pltpu.SMEM),
                  pl.BlockSpec((1, 1, MOE_BLOCK), lambda b, ba, bb, bv: (b, 0, 0), memory_space=pltpu.SMEM),
                  pl.BlockSpec(wgu_block, lambda b, ba, bb, bv: (ba[b], 0, 0)),
                  pl.BlockSpec(wgu_block, lambda b, ba, bb, bv: (bb[b], 0, 0)),
                  pl.BlockSpec(wd_block, lambda b, ba, bb, bv: (ba[b], 0, 0)),
                  pl.BlockSpec(wd_block, lambda b, ba, bb, bv: (bb[b], 0, 0))],
        out_specs=pl.BlockSpec(memory_space=pl.ANY),
        scratch_shapes=[pltpu.VMEM((2, MOE_BLOCK, D_MODEL // 2), jnp.int32), pltpu.SemaphoreType.DMA((2,)),
                        pltpu.SemaphoreType.DMA(())],
    )
    return pl.pallas_call(
        _experts_kernel,
        grid_spec=grid_spec,
        out_shape=jax.ShapeDtypeStruct((n_tok + MOE_BLOCK, D_MODEL // 2), jnp.int32),
        compiler_params=_params("arbitrary"),
        name="experts",
    )(block_a, block_b, n_used, xs, st3, st3, w['wgu'], w['wgu'], w['wd'], w['wd'])


def _combine_kernel(x1_ref, f_ref, lng_ref, lnb_ref, o_ref):
    f = jnp.concatenate(_unpack_bf16_pairs(f_ref[...]), axis=1)
    o_ref[...] = _layer_norm(DEEPNORM_ALPHA * x1_ref[...] + f, lng_ref[...], lnb_ref[...])


def _combine_call(x1, f, w):
    n = x1.shape[0]
    tm = COMBINE_TM
    return pl.pallas_call(
        _combine_kernel,
        grid=(n // tm,),
        in_specs=[pl.BlockSpec((tm, D_MODEL), lambda i: (i, 0)),
                  pl.BlockSpec((tm, D_MODEL // 2), lambda i: (i, 0)),
                  _const_spec(w['ln2_g'].shape), _const_spec(w['ln2_b'].shape)],
        out_specs=pl.BlockSpec((tm, D_MODEL), lambda i: (i, 0)),
        out_shape=jax.ShapeDtypeStruct((n, D_MODEL), F32),
        compiler_params=_params("parallel"),
        name="combine",
    )(x1, f, w['ln2_g'], w['ln2_b'])


def _class_experts():
    pairs = [(a, b) for a in range(EXPERTS_PER_GROUP) for b in range(a + 1, EXPERTS_PER_GROUP)]
    ea = [g * EXPERTS_PER_GROUP + a for g in range(N_GROUPS) for a, _ in pairs]
    eb = [g * EXPERTS_PER_GROUP + b for g in range(N_GROUPS) for _, b in pairs]
    return jnp.asarray(ea, jnp.int32), jnp.asarray(eb, jnp.int32)


def _dispatch_plan(cls):
    n = cls.shape[1]
    n_blocks = -(-n // MOE_BLOCK) + N_CLASSES
    onehot = (cls[0][:, None] == jnp.arange(N_CLASSES, dtype=jnp.int32)).astype(jnp.int32)
    incl = jnp.cumsum(onehot, axis=0)
    counts = incl[-1]
    padded = (counts + MOE_BLOCK - 1) // MOE_BLOCK * MOE_BLOCK
    pad_end = jnp.cumsum(padded).astype(jnp.int32)
    pad_start = pad_end - padded
    slot = pad_start[None, :] + (incl - onehot)
    dest = jnp.sum(onehot * slot, axis=1).astype(jnp.int32)
    dest3 = dest.reshape(n // ROUTE_TM, 1, ROUTE_TM)
    first_slot = jnp.arange(n_blocks, dtype=jnp.int32) * MOE_BLOCK
    block_c = jnp.minimum(jnp.sum(pad_end[None, :] <= first_slot[:, None], axis=1), N_CLASSES - 1)
    ea, eb = _class_experts()
    n_used = pad_end[-1:] // MOE_BLOCK
    return dest3, pad_start, pad_end, ea[block_c], eb[block_c], n_used, n_blocks * MOE_BLOCK


def _rope_tables(seq):
    inv = 1.0 / (ROPE_BASE ** (jnp.arange(0, QK_ROPE, 2, dtype=F32) / QK_ROPE))
    ang = jnp.arange(seq, dtype=F32)[:, None] * inv[None, :]
    pad = jnp.zeros((seq, LANES - QK_ROPE), F32)
    cos, sin = jnp.cos(ang), jnp.sin(ang)
    return jnp.concatenate([cos, cos, pad], -1), jnp.concatenate([sin, sin, pad], -1)


def _pad_cols(t, width):
    return jnp.pad(t, ((0, 0), (0, width - t.shape[1])))


def _rope_weight_pair(w_rope):
    half = QK_ROPE // 2
    swapped = jnp.concatenate([-w_rope[:, half:], w_rope[:, :half]], axis=1)
    return _pad_cols(w_rope, LANES), _pad_cols(swapped, LANES)


def _layer_weights(l, w_in, a_ln_g, a_ln_b, a_ws, a_bs, na_rpb, mla_q_norm, mla_kv_norm, mla_w_uq,
                   mla_w_ukv, w_br_a, w_br_b, w_br_c, w_o, ln1_g, ln1_b, ln2_g, ln2_b, w_gate, w_up,
                   w_down):
    sizes = (2 * A_WIDTH, NA_WIDTH, NA_WIDTH, NA_WIDTH, Q_LORA, KV_LORA, QK_ROPE, N_BRANCH * D_MODEL)
    cuts = np.cumsum(sizes)[:-1].tolist()
    wa, wq, wk, wv, wcq, wckv, wkr, wg = jnp.split(w_in[l], cuts, axis=-1)
    row = lambda t: t.reshape(1, -1).astype(F32)
    uq = mla_w_uq[l].reshape(Q_LORA, MLA_HEADS, QK_NOPE + QK_ROPE)
    wqn = uq[:, :, :QK_NOPE].reshape(Q_LORA, MLA_HEADS * QK_NOPE)
    rope_pairs = [_rope_weight_pair(uq[:, h, QK_NOPE:]) for h in range(MLA_HEADS)]
    wqr = jnp.concatenate([p[0] for p in rope_pairs], axis=1)
    wqs = jnp.concatenate([p[1] for p in rope_pairs], axis=1)
    ukv = mla_w_ukv[l].reshape(KV_LORA, MLA_HEADS, QK_NOPE + V_DIM)
    wukv = jnp.concatenate([ukv[:, :, :QK_NOPE].reshape(KV_LORA, -1), ukv[:, :, QK_NOPE:].reshape(KV_LORA, -1)], 1)
    wkr_p, wkrs_p = _rope_weight_pair(wkr)
    return {
        'wa': wa.astype(BF16), 'a_ln_g': row(a_ln_g[l]), 'a_ln_b': row(a_ln_b[l]),
        'ws': a_ws[l].astype(BF16),
        'bs': jnp.broadcast_to(a_bs[l][:, :, None], (A_GROUPS, CHUNK, CHUNK)).astype(F32),
        'wqkv': jnp.concatenate([wq * NA_SCALE, wk, wv], axis=1).astype(BF16),
        'wcq': wcq.astype(BF16), 'q_norm': row(mla_q_norm[l]),
        'wqn': wqn.astype(BF16), 'wqr': wqr.astype(BF16), 'wqs': wqs.astype(BF16),
        'wckv': wckv.astype(BF16), 'kv_norm': row(mla_kv_norm[l]), 'wukv': wukv.astype(BF16),
        'wkr': wkr_p.astype(BF16), 'wkrs': wkrs_p.astype(BF16),
        'na_bias': _natten_bias(na_rpb[l]), 'na_bmax': _natten_bias_max(na_rpb[l]),
        'wg': wg.astype(BF16),
        'wbr': jnp.stack([w_br_a[l], w_br_b[l], w_br_c[l]]).astype(BF16),
        'wo': w_o[l].astype(BF16), 'ln1_g': row(ln1_g[l]), 'ln1_b': row(ln1_b[l]),
        'ln2_g': row(ln2_g[l]), 'ln2_b': row(ln2_b[l]),
        'wgu': jnp.concatenate([w_gate[l], w_up[l]], axis=-1).astype(BF16),
        'wd': w_down[l].astype(BF16),
    }


def _shared_weights(w_router, router_bias):
    wr_t = w_router.T.astype(F32)
    wrh = wr_t.astype(BF16)
    wrl = (wr_t - wrh.astype(F32)).astype(BF16)
    rb = jnp.broadcast_to(router_bias.astype(F32)[:, None], (N_EXPERTS, MERGE_TM))
    return {'wrh': wrh, 'wrl': wrl, 'rb': rb}


def _trunk(x, layers, shared):
    bsz, seq, _ = x.shape
    cos2, sin2 = _rope_tables(seq)
    x2d = x.reshape(bsz * seq, D_MODEL)
    moe = None
    for w in layers:
        if moe is None:
            oa, q, k, v, qm, km, vm, kn2 = _pre_call(x2d, seq, cos2, sin2, w)
        else:
            oa, q, k, v, qm, km, vm, kn2, x2d = _pre_call(None, seq, cos2, sin2, w, moe)
        ob = _natten_call(q, k, v, kn2, w['na_bias'], w['na_bmax'], bsz, seq)
        oc = _mla_call(qm, km, vm, bsz, seq)
        x1, xpk, cls = _merge_call(x2d, oa, ob, oc, w, shared)
        dest3, pad_start, pad_end, block_a, block_b, n_used, n_slots = _dispatch_plan(cls)
        xs, slot_tok = _dispatch_call(xpk, dest3, pad_start, pad_end, n_used, n_slots)
        f = _experts_call(xs, slot_tok, block_a, block_b, n_used, x1.shape[0], w)
        moe = (x1, f, w)
    return _combine_call(x1, f, w).reshape(bsz, seq, D_MODEL)


def kernel(x_prompt, x_sample, w_in, a_ln_g, a_ln_b, a_ws, a_bs, na_rpb, mla_q_norm, mla_kv_norm, mla_w_uq, mla_w_ukv, w_br_a, w_br_b, w_br_c, w_o, ln1_g, ln1_b, ln2_g, ln2_b, w_router, router_bias, w_gate, w_up, w_down):
    layers = [_layer_weights(l, w_in, a_ln_g, a_ln_b, a_ws, a_bs, na_rpb, mla_q_norm, mla_kv_norm,
                             mla_w_uq, mla_w_ukv, w_br_a, w_br_b, w_br_c, w_o, ln1_g, ln1_b, ln2_g,
                             ln2_b, w_gate, w_up, w_down) for l in range(DEPTH)]
    shared = _shared_weights(w_router, router_bias)
    return (_trunk(x_prompt, layers, shared), _trunk(x_sample, layers, shared))
```

```python
import functools
import math

import jax
import jax.numpy as jnp
import numpy as np
from jax import lax
from jax.experimental import pallas as pl
from jax.experimental.pallas import tpu as pltpu

F32 = jnp.float32
BF16 = jnp.bfloat16

D_MODEL = 1024
DEPTH = 2
GRID_W = 64
CHUNK = 128
A_WIDTH = 512
A_GROUPS = 4
NA_HEADS = 8
NA_HEAD_DIM = 64
NA_WIN_H = 8
NA_WIN_W = 16
NA_WIDTH = NA_HEADS * NA_HEAD_DIM
NA_SCALE = NA_HEAD_DIM ** -0.5
MLA_HEADS = 4
Q_LORA = 384
KV_LORA = 256
QK_NOPE = 128
QK_ROPE = 64
V_DIM = 128
MLA_WIDTH = MLA_HEADS * V_DIM
MLA_SCALE = (QK_NOPE + QK_ROPE) ** -0.5
ROPE_BASE = 10000.0
N_BRANCH = 3
N_EXPERTS = 16
N_GROUPS = 4
EXPERTS_PER_GROUP = N_EXPERTS // N_GROUPS
TOP_K = 2
D_EXPERT = 256
MOE_BLOCK = 256
DEEPNORM_ALPHA = (2 * DEPTH) ** 0.25
LN_EPS = 1e-5
RMS_EPS = 1e-6
NEG_INF = -1e30
LOG2E = math.log2(math.e)

VMEM_LIMIT_BYTES = 60 * 1024 * 1024
LANES = 128
MLA_HEAD_PAD = 2 * LANES
NA_ROW_BLOCK = 8
NA_TOK_BLOCK = NA_ROW_BLOCK * GRID_W

ROW_W = D_MODEL // 2 + LANES
PAIRS_PER_GROUP = EXPERTS_PER_GROUP * (EXPERTS_PER_GROUP - 1) // 2
N_CLASSES = N_GROUPS * PAIRS_PER_GROUP
HIGH_HALF = -65536
PRE_TM = 1024
MERGE_TM = 1024
ROUTE_TM = 256
COMBINE_TM = 512
MLA_TQ = 2048
MLA_TK = 1024
MLA_MAX_UNROLL = 5
MLA_SHIFT_LANE = QK_NOPE + QK_ROPE
MLA_L_MIN = 2.0 ** -60
MLA_L_MAX = 2.0 ** 100
MLA_BOUND_SLACK = 1.0 + 2.0 ** -7
NA_BOUND_SLACK = 1.0 + 2.0 ** -6
NA_L_MIN = 2.0 ** -60
NA_L_MAX = 2.0 ** 100


def _params(*sem):
    return pltpu.CompilerParams(dimension_semantics=sem, vmem_limit_bytes=VMEM_LIMIT_BYTES)


def _const_spec(shape):
    nd = len(shape)
    return pl.BlockSpec(shape, lambda *_: (0,) * nd, pipeline_mode=pl.Buffered(1))


def _layer_norm(y, g, b):
    mu = jnp.mean(y, -1, keepdims=True)
    yc = y - mu
    var = jnp.mean(yc * yc, -1, keepdims=True)
    return yc * lax.rsqrt(var + LN_EPS) * g + b


def _rms_norm(y, g):
    return y * lax.rsqrt(jnp.mean(y * y, -1, keepdims=True) + RMS_EPS) * g


def _gelu_tanh(x):
    return 0.5 * x * (1.0 + jnp.tanh(math.sqrt(2.0 / math.pi) * (x + 0.044715 * (x * x * x))))


def _sigmoid(x):
    return 1.0 / (1.0 + jnp.exp(-x))


def _pack_bf16_pairs(y):
    w = y.shape[1] // 2
    lo = lax.bitcast_convert_type(y[:, :w].astype(BF16).astype(F32), jnp.int32)
    hi = lax.bitcast_convert_type(y[:, w:].astype(BF16).astype(F32), jnp.int32)
    return lax.shift_right_logical(lo, 16) | (hi & HIGH_HALF)


def _unpack_bf16_pairs(p):
    lo = lax.bitcast_convert_type(lax.shift_left(p, 16), F32)
    hi = lax.bitcast_convert_type(p & HIGH_HALF, F32)
    return lo, hi


def _dot(a, b):
    return jnp.dot(a, b, preferred_element_type=F32)


def _dot_nt(a, b):
    return lax.dot_general(a, b, (((1,), (1,)), ((), ())), preferred_element_type=F32)


def _pre_kernel(*refs, after_moe):
    if after_moe:
        x1_ref, f_ref, ln2g_ref, ln2b_ref = refs[:4]
        refs, x2_ref = refs[4:-1], refs[-1]
        f = jnp.concatenate(_unpack_bf16_pairs(f_ref[...]), axis=1)
        x = _layer_norm(DEEPNORM_ALPHA * x1_ref[...] + f, ln2g_ref[...], ln2b_ref[...])
        x2_ref[...] = x
    else:
        x, refs = refs[0][...], refs[1:]
    (cos_ref, sin_ref, wa_ref, lng_ref, lnb_ref, ws_ref, bs_ref, wqkv_ref, wlat_ref, qnorm_ref, wqn_ref,
     wqr_ref, wqs_ref, kvnorm_ref, wukv_ref, hsum_ref,
     oa_ref, q_ref, k_ref, v_ref, qm_ref, km_ref, vm_ref, kn2_ref) = refs
    tm = x.shape[0]
    xb = x.astype(BF16)
    cos2 = cos_ref[...]
    sin2 = sin_ref[...]

    a = _gelu_tanh(_dot(xb, wa_ref[...]))
    u = a[:, :A_WIDTH]
    vn = _layer_norm(a[:, A_WIDTH:], lng_ref[...], lnb_ref[...]).astype(BF16)
    for c in range(tm // CHUNK):
        rs = slice(c * CHUNK, (c + 1) * CHUNK)
        for g in range(A_GROUPS):
            cs = slice(g * CHUNK, (g + 1) * CHUNK)
            sp = _dot(ws_ref[g], vn[rs, cs]) + bs_ref[g]
            oa_ref[rs, cs] = (u[rs, cs] * sp).astype(BF16)

    qkv = _dot(xb, wqkv_ref[...])
    q_ref[...] = qkv[:, :NA_WIDTH].astype(BF16)
    kb = qkv[:, NA_WIDTH:2 * NA_WIDTH].astype(BF16)
    k_ref[...] = kb
    v_ref[...] = qkv[:, 2 * NA_WIDTH:].astype(BF16)
    kf = kb.astype(F32)
    kn2_ref[...] = _dot((kf * kf).astype(BF16), hsum_ref[...])

    lat = _dot(xb, wlat_ref[...])
    c_q = lat[:, :Q_LORA]
    kr = lat[:, Q_LORA:Q_LORA + LANES]
    c_kv = lat[:, Q_LORA + LANES:Q_LORA + LANES + KV_LORA]
    krs = lat[:, Q_LORA + LANES + KV_LORA:]
    cqn = _rms_norm(c_q, qnorm_ref[...]).astype(BF16)
    qn = _dot(cqn, wqn_ref[...])
    qr = _dot(cqn, wqr_ref[...])
    qs = _dot(cqn, wqs_ref[...])
    qscale = MLA_SCALE * LOG2E
    shift_lane = lax.broadcasted_iota(jnp.int32, (tm, LANES), 1) == MLA_SHIFT_LANE - QK_NOPE
    for h in range(MLA_HEADS):
        hs = slice(h * LANES, (h + 1) * LANES)
        nope = (qn[:, hs] * qscale).astype(BF16)
        rot = ((qr[:, hs] * cos2 + qs[:, hs] * sin2) * qscale).astype(BF16)
        nf, rf = nope.astype(F32), rot.astype(F32)
        norm = jnp.sqrt(jnp.sum(nf * nf, -1, keepdims=True) + jnp.sum(rf * rf, -1, keepdims=True))
        norm = jnp.broadcast_to(norm * MLA_BOUND_SLACK, rot.shape).astype(BF16)
        qm_ref[:, h * MLA_HEAD_PAD:h * MLA_HEAD_PAD + LANES] = nope
        qm_ref[:, h * MLA_HEAD_PAD + LANES:(h + 1) * MLA_HEAD_PAD] = jnp.where(shift_lane, norm, rot)

    ckvn = _rms_norm(c_kv, kvnorm_ref[...]).astype(BF16)
    kv = _dot(ckvn, wukv_ref[...])
    krope = kr * cos2 + krs * sin2
    lane = lax.broadcasted_iota(jnp.int32, krope.shape, 1)
    krope = jnp.where(lane == MLA_SHIFT_LANE - QK_NOPE, -1.0, krope).astype(BF16)
    for h in range(MLA_HEADS):
        km_ref[:, h * MLA_HEAD_PAD:h * MLA_HEAD_PAD + LANES] = kv[:, h * LANES:(h + 1) * LANES].astype(BF16)
        km_ref[:, h * MLA_HEAD_PAD + LANES:(h + 1) * MLA_HEAD_PAD] = krope
    vm_ref[...] = kv[:, MLA_HEADS * QK_NOPE:].astype(BF16)


def _head_sum_matrix():
    feat = np.arange(NA_WIDTH)[:, None] // NA_HEAD_DIM
    return jnp.asarray(feat == np.arange(LANES)[None, :], BF16)


def _pre_call(x2d, seq, cos2, sin2, w, moe=None):
    n = x2d.shape[0] if moe is None else moe[0].shape[0]
    tm = PRE_TM
    per_seq = seq // tm
    weights = [w['wa'], w['a_ln_g'], w['a_ln_b'], w['ws'], w['bs'], w['wqkv'], w['wlat'], w['q_norm'],
               w['wqn'], w['wqr'], w['wqs'], w['kv_norm'], w['wukv'], _head_sum_matrix()]
    tok = lambda width: pl.BlockSpec((tm, width), lambda i: (i, 0))
    pos = pl.BlockSpec((tm, LANES), lambda i: (i % per_seq, 0))
    out_widths = [A_WIDTH, NA_WIDTH, NA_WIDTH, NA_WIDTH, MLA_HEADS * MLA_HEAD_PAD,
                  MLA_HEADS * MLA_HEAD_PAD, MLA_WIDTH]
    out_specs = [tok(wd) for wd in out_widths] + [tok(LANES)]
    out_shape = ([jax.ShapeDtypeStruct((n, wd), BF16) for wd in out_widths]
                 + [jax.ShapeDtypeStruct((n, LANES), F32)])
    if moe is None:
        stream, stream_specs = [x2d], [tok(D_MODEL)]
    else:
        x1, f, prev = moe
        stream = [x1, f, prev['ln2_g'], prev['ln2_b']]
        stream_specs = [tok(D_MODEL), tok(D_MODEL // 2), _const_spec(prev['ln2_g'].shape),
                        _const_spec(prev['ln2_b'].shape)]
        out_specs.append(tok(D_MODEL))
        out_shape.append(jax.ShapeDtypeStruct((n, D_MODEL), F32))
    return pl.pallas_call(
        functools.partial(_pre_kernel, after_moe=moe is not None),
        grid=(n // tm,),
        in_specs=stream_specs + [pos, pos] + [_const_spec(t.shape) for t in weights],
        out_specs=out_specs,
        out_shape=out_shape,
        compiler_params=_params("parallel"),
        name="pre",
    )(*stream, cos2, sin2, *weights)


def _natten_kernel(q_ref, kp_ref, kc_ref, kn_ref, vp_ref, vc_ref, vn_ref, np_ref, nc_ref, nn_ref,
                   bias_ref, bmax_ref, o_ref, kcat, vcat, *, rows):
    j = pl.program_id(1)
    nt = NA_TOK_BLOCK
    kcat[0:nt] = kp_ref[...]
    kcat[nt:2 * nt] = kc_ref[...]
    kcat[2 * nt:3 * nt] = kn_ref[...]
    vcat[0:nt] = vp_ref[...]
    vcat[nt:2 * nt] = vc_ref[...]
    vcat[2 * nt:3 * nt] = vn_ref[...]
    r0 = j * NA_ROW_BLOCK
    lane_lo = lax.broadcasted_iota(jnp.int32, (GRID_W, LANES), 1) < NA_HEAD_DIM

    kmax2 = jnp.maximum(jnp.maximum(jnp.max(np_ref[...], 0, keepdims=True),
                                    jnp.max(nc_ref[...], 0, keepdims=True)),
                        jnp.max(nn_ref[...], 0, keepdims=True))
    kmax = jnp.broadcast_to(jnp.sqrt(kmax2) * NA_BOUND_SLACK, (2 * GRID_W, LANES))
    row_lo = lax.broadcasted_iota(jnp.int32, (2 * GRID_W, LANES), 0) < GRID_W
    lane = lax.broadcasted_iota(jnp.int32, (2 * GRID_W, LANES), 1)
    kcols = [jnp.sum(jnp.where(lane == jnp.where(row_lo, 2 * hp, 2 * hp + 1), kmax, 0.0), -1, keepdims=True)
             for hp in range(NA_HEADS // 2)]

    def attend(exact):
        bad = jnp.zeros((2 * GRID_W, 1), F32)
        for i in range(NA_ROW_BLOCK):
            r = r0 + i
            start = jnp.clip(r - NA_WIN_H // 2, 0, rows - NA_WIN_H)
            pat = start - r + (NA_WIN_H - 1)
            koff = pl.multiple_of((start - r0 + NA_ROW_BLOCK) * GRID_W, GRID_W)
            qs = slice(i * GRID_W, (i + 1) * GRID_W)
            for hp in range(NA_HEADS // 2):
                cs = slice(hp * LANES, (hp + 1) * LANES)
                qp = q_ref[qs, cs]
                zero = jnp.zeros_like(qp)
                q2 = jnp.concatenate([jnp.where(lane_lo, qp, zero), jnp.where(lane_lo, zero, qp)], axis=0)
                kk = kcat[pl.ds(koff, NA_WIN_H * GRID_W), cs]
                vv = vcat[pl.ds(koff, NA_WIN_H * GRID_W), cs]
                s = _dot_nt(q2, kk) + bias_ref[pat, hp]
                if exact:
                    m = jnp.max(s, -1, keepdims=True)
                else:
                    qf = q2.astype(F32)
                    m = jnp.sqrt(jnp.sum(qf * qf, -1, keepdims=True)) * kcols[hp] + bmax_ref[hp][:, 0:1]
                e = jnp.exp(s - m)
                l = jnp.sum(e, -1, keepdims=True)
                if not exact:
                    bad = bad + jnp.where((l >= NA_L_MIN) & (l <= NA_L_MAX), 0.0, 1.0)
                o2 = _dot(e.astype(BF16), vv) / l
                o_ref[qs, cs] = jnp.where(lane_lo, o2[:GRID_W], o2[GRID_W:]).astype(BF16)
        return bad

    n_bad = jnp.sum(attend(exact=False))

    @pl.when(n_bad > 0.0)
    def _():
        attend(exact=True)


def _natten_call(q, k, v, knorm, bias, bmax, bsz, seq):
    rows = seq // GRID_W
    nrb = rows // NA_ROW_BLOCK
    nt = NA_TOK_BLOCK
    cur = lambda width: pl.BlockSpec((nt, width), lambda b, j: (b * nrb + j, 0))
    prev = lambda width: pl.BlockSpec((nt, width), lambda b, j: (b * nrb + jnp.maximum(j - 1, 0), 0))
    nxt = lambda width: pl.BlockSpec((nt, width), lambda b, j: (b * nrb + jnp.minimum(j + 1, nrb - 1), 0))
    w = NA_WIDTH
    return pl.pallas_call(
        functools.partial(_natten_kernel, rows=rows),
        grid=(bsz, nrb),
        in_specs=[cur(w), prev(w), cur(w), nxt(w), prev(w), cur(w), nxt(w),
                  prev(LANES), cur(LANES), nxt(LANES), _const_spec(bias.shape), _const_spec(bmax.shape)],
        out_specs=cur(w),
        out_shape=jax.ShapeDtypeStruct((bsz * seq, NA_WIDTH), BF16),
        scratch_shapes=[pltpu.VMEM((3 * nt, NA_WIDTH), BF16), pltpu.VMEM((3 * nt, NA_WIDTH), BF16)],
        compiler_params=_params("parallel", "parallel"),
        name="natten",
    )(q, k, k, k, v, v, v, knorm, knorm, knorm, bias, bmax)


def _natten_bias_max(rpb):
    m = jnp.max(rpb.astype(F32), axis=(1, 2)).reshape(NA_HEADS // 2, 2, 1, 1)
    return jnp.broadcast_to(m, (NA_HEADS // 2, 2, GRID_W, LANES)).reshape(NA_HEADS // 2, 2 * GRID_W, LANES)


def _natten_bias(rpb):
    c = jnp.arange(GRID_W)
    col_start = jnp.clip(c - NA_WIN_W // 2, 0, GRID_W - NA_WIN_W)
    in_win = (c[None, :] >= col_start[:, None]) & (c[None, :] < col_start[:, None] + NA_WIN_W)
    dc = jnp.clip(c[None, :] - c[:, None] + (NA_WIN_W - 1), 0, 2 * NA_WIN_W - 2)
    onehot = (dc[:, :, None] == jnp.arange(2 * NA_WIN_W - 1)).astype(F32)
    t = jnp.einsum('hrd,qkd->hrqk', rpb.astype(F32), onehot, precision=lax.Precision.HIGHEST)
    t = jnp.where(in_win[None, None], t, NEG_INF)
    b = jnp.stack([t[:, p:p + NA_WIN_H] for p in range(NA_WIN_H)])
    b = b.transpose(0, 1, 3, 2, 4)
    return b.reshape(NA_WIN_H, NA_HEADS // 2, 2 * GRID_W, NA_WIN_H * GRID_W)


def _mla_online_softmax(q, k_ref, v_ref, tk):
    tq = q.shape[0]

    def body(j, carry):
        m, l, acc = carry
        ks = pl.ds(pl.multiple_of(j * tk, tk), tk)
        s = _dot_nt(q, k_ref[ks, :])
        m_new = jnp.maximum(m, jnp.max(s, -1, keepdims=True))
        alpha = jnp.exp2(m - m_new)
        p = jnp.exp2(s - m_new)
        l = alpha * l + jnp.sum(p, -1, keepdims=True)
        acc = alpha * acc + _dot(p.astype(BF16), v_ref[ks, :])
        return m_new, l, acc

    init = (jnp.full((tq, 1), NEG_INF, F32), jnp.zeros((tq, 1), F32), jnp.zeros((tq, V_DIM), F32))
    _, l, acc = lax.fori_loop(0, k_ref.shape[0] // tk, body, init)
    return acc, l


def _mla_kernel(q_ref, k_ref, v_ref, o_ref, kmax_ref, *, tk, unroll):
    q = q_ref[...]
    tq = q.shape[0]
    seq = k_ref.shape[0]

    @pl.when(pl.program_id(2) == 0)
    def _():
        def body(c, mx):
            kc = k_ref[pl.ds(pl.multiple_of(c * tk, tk), tk), :].astype(F32)
            n2 = jnp.sum(kc * kc, -1, keepdims=True)
            return jnp.maximum(mx, jnp.max(n2, 0, keepdims=True))

        mx = lax.fori_loop(0, seq // tk, body, jnp.zeros((1, 1), F32))
        kmax = jnp.broadcast_to(jnp.sqrt(mx) * MLA_BOUND_SLACK, kmax_ref.shape)
        lane = lax.broadcasted_iota(jnp.int32, kmax_ref.shape, 1)
        kmax_ref[...] = jnp.where(lane == MLA_SHIFT_LANE, kmax, 1.0).astype(BF16)

    q_shift = q * kmax_ref[0:1, :]
    ones = jnp.ones((tk, LANES), BF16)

    def body(j, acc):
        ks = pl.ds(pl.multiple_of(j * tk, tk), tk)
        p = jnp.exp2(_dot_nt(q_shift, k_ref[ks, :])).astype(BF16)
        return acc + _dot(p, jnp.concatenate([v_ref[ks, :], ones], axis=1))

    nk = seq // tk
    acc = lax.fori_loop(0, nk - 1, body, jnp.zeros((tq, V_DIM + LANES), F32), unroll=unroll)
    l_part = acc[:, V_DIM:]
    usable = (l_part >= MLA_L_MIN) & (l_part <= MLA_L_MAX)
    n_bad = jnp.sum(jnp.where(usable, 0.0, 1.0))
    acc = body(nk - 1, acc)
    l = acc[:, V_DIM:]
    o_ref[...] = (acc[:, :V_DIM] / l).astype(BF16)

    @pl.when(n_bad > 0.0)
    def _():
        acc, l = _mla_online_softmax(q, k_ref, v_ref, tk)
        o_ref[...] = (acc / l).astype(BF16)


def _mla_call(qm, km, vm, bsz, seq):
    tq = min(MLA_TQ, seq)
    nq = seq // tq
    tk = min(MLA_TK, seq // 2)
    looped = seq // tk - 1
    unroll = max(u for u in range(1, MLA_MAX_UNROLL + 1) if looped % u == 0)
    return pl.pallas_call(
        functools.partial(_mla_kernel, tk=tk, unroll=unroll),
        grid=(bsz, MLA_HEADS, nq),
        in_specs=[pl.BlockSpec((tq, MLA_HEAD_PAD), lambda b, h, i: (b * nq + i, h)),
                  pl.BlockSpec((seq, MLA_HEAD_PAD), lambda b, h, i: (b, h)),
                  pl.BlockSpec((seq, V_DIM), lambda b, h, i: (b, h))],
        out_specs=pl.BlockSpec((tq, V_DIM), lambda b, h, i: (b * nq + i, h)),
        out_shape=jax.ShapeDtypeStruct((bsz * seq, MLA_WIDTH), BF16),
        scratch_shapes=[pltpu.VMEM((16, MLA_HEAD_PAD), BF16)],
        compiler_params=_params("parallel", "parallel", "arbitrary"),
        name="mla",
    )(qm, km, vm)


def _first_argmax(vals):
    best, idx = vals[0], jnp.zeros(vals[0].shape, jnp.int32)
    for i in range(1, len(vals)):
        better = vals[i] > best
        idx = jnp.where(better, i, idx)
        best = jnp.where(better, vals[i], best)
    return best, idx


def _select(idx, vals):
    out = vals[0]
    for i in range(1, len(vals)):
        out = jnp.where(idx == i, vals[i], out)
    return out


def _merge_kernel(x_ref, oa_ref, ob_ref, oc_ref, wg_ref, wbr_ref, wo_ref, lng_ref, lnb_ref,
                  wrh_ref, wrl_ref, rb_ref, x1_ref, xpk_ref, cls_ref):
    x = x_ref[...]
    xb = x.astype(BF16)
    merged = None
    for i, o_ref in enumerate((oa_ref, ob_ref, oc_ref)):
        g = _sigmoid(_dot(xb, wg_ref[:, i * D_MODEL:(i + 1) * D_MODEL]))
        term = g * _dot(o_ref[...], wbr_ref[i])
        merged = term if merged is None else merged + term
    m = _dot(merged.astype(BF16), wo_ref[...])
    x1 = _layer_norm(DEEPNORM_ALPHA * x + m, lng_ref[...], lnb_ref[...])
    x1_ref[...] = x1

    x1h = x1.astype(BF16)
    x1l = (x1 - x1h.astype(F32)).astype(BF16)
    logits = _dot_nt(wrh_ref[...], x1h) + (_dot_nt(wrl_ref[...], x1h) + _dot_nt(wrh_ref[...], x1l))
    scores = _sigmoid(logits)
    biased = scores + rb_ref[...]
    sc = [scores[e:e + 1, :] for e in range(N_EXPERTS)]
    bi = [biased[e:e + 1, :] for e in range(N_EXPERTS)]
    grp_scores = []
    for g in range(N_GROUPS):
        v = bi[g * EXPERTS_PER_GROUP:(g + 1) * EXPERTS_PER_GROUP]
        top2 = None
        for i in range(EXPERTS_PER_GROUP):
            for k in range(i + 1, EXPERTS_PER_GROUP):
                top2 = v[i] + v[k] if top2 is None else jnp.maximum(top2, v[i] + v[k])
        grp_scores.append(top2)
    _, grp = _first_argmax(grp_scores)
    cand = [_select(grp, [bi[g * EXPERTS_PER_GROUP + i] for g in range(N_GROUPS)])
            for i in range(EXPERTS_PER_GROUP)]
    cand_sc = [_select(grp, [sc[g * EXPERTS_PER_GROUP + i] for g in range(N_GROUPS)])
               for i in range(EXPERTS_PER_GROUP)]
    _, i1 = _first_argmax(cand)
    _, i2 = _first_argmax([jnp.where(i1 == i, -jnp.inf, cand[i]) for i in range(EXPERTS_PER_GROUP)])
    s1 = _select(i1, cand_sc)
    s2 = _select(i2, cand_sc)
    tot = s1 + s2
    g1 = s1 / tot
    g2 = s2 / tot
    first_lower = i1 < i2
    a = jnp.where(first_lower, i1, i2)
    b = jnp.where(first_lower, i2, i1)
    cls_ref[...] = grp * PAIRS_PER_GROUP + (((a * (7 - a)) >> 1) + (b - a - 1))
    xpk_ref[:, :D_MODEL // 2] = _pack_bf16_pairs(x1)
    tm = x.shape[0]
    sub = lax.broadcasted_iota(jnp.int32, (LANES, tm), 0)
    ga = jnp.broadcast_to(jnp.where(first_lower, g1, g2), (LANES, tm))
    gb = jnp.broadcast_to(jnp.where(first_lower, g2, g1), (LANES, tm))
    gmat = jnp.where(sub == 0, ga, jnp.where(sub == 1, gb, 0.0))
    xpk_ref[:, D_MODEL // 2:] = lax.bitcast_convert_type(gmat.T, jnp.int32)


def _merge_call(x2d, oa, ob, oc, w, shared):
    n = x2d.shape[0]
    tm = MERGE_TM
    weights = [w['wg'], w['wbr'], w['wo'], w['ln1_g'], w['ln1_b'], shared['wrh'], shared['wrl'], shared['rb']]
    tok = lambda width: pl.BlockSpec((tm, width), lambda i: (i, 0))
    return pl.pallas_call(
        _merge_kernel,
        grid=(n // tm,),
        in_specs=[tok(D_MODEL), tok(A_WIDTH), tok(NA_WIDTH), tok(MLA_WIDTH)]
                 + [_const_spec(t.shape) for t in weights],
        out_specs=[tok(D_MODEL), tok(ROW_W), pl.BlockSpec((1, tm), lambda i: (0, i))],
        out_shape=[jax.ShapeDtypeStruct((n, D_MODEL), F32),
                   jax.ShapeDtypeStruct((n, ROW_W), jnp.int32),
                   jax.ShapeDtypeStruct((1, n), jnp.int32)],
        compiler_params=_params("parallel"),
        name="merge",
    )(x2d, oa, ob, oc, *weights)


def _row_copy(src, src_row, dst, dst_row, sem):
    return pltpu.make_async_copy(src.at[pl.ds(src_row, 1)], dst.at[pl.ds(dst_row, 1)], sem)


def _dispatch_kernel(pstart_ref, pend_ref, nused_ref, dest_ref, x_ref, spare_hbm, xs_hbm, st_ref, zbuf, zsem, sem):
    i = pl.program_id(0)
    tm = dest_ref.shape[2]
    n_slots = xs_hbm.shape[0]
    n_blocks = n_slots // MOE_BLOCK

    @pl.when(i == 0)
    def _():
        no_owner = pltpu.make_async_copy(spare_hbm, st_ref, zsem)
        no_owner.start()
        no_owner.wait()
        zbuf[...] = jnp.zeros_like(zbuf)

        def zero_block(first_slot):
            dst = xs_hbm.at[pl.ds(pl.multiple_of(first_slot, MOE_BLOCK), MOE_BLOCK)]
            return pltpu.make_async_copy(zbuf, dst, zsem)

        def for_each_zero_block(fn):
            for g in range(N_CLASSES):
                @pl.when(pend_ref[g] > pstart_ref[g])
                def _():
                    fn(zero_block(pend_ref[g] - MOE_BLOCK))

                @pl.when(nused_ref[0] + g < n_blocks)
                def _():
                    fn(zero_block((nused_ref[0] + g) * MOE_BLOCK))

        for_each_zero_block(lambda cp: cp.start())
        for_each_zero_block(lambda cp: cp.wait())

    def wait(j, c):
        _row_copy(x_ref, 0, xs_hbm, 0, sem).wait()
        return c

    for j in range(tm):
        d = dest_ref[0, 0, j]
        _row_copy(x_ref, j, xs_hbm, d, sem).start(priority=j % 2)
        st_ref[d] = i * tm + j
    lax.fori_loop(0, tm, wait, 0, unroll=8)


def _dispatch_call(xpk, dest3, pad_start, pad_end, n_used, n_slots):
    n = xpk.shape[0]
    tm = dest3.shape[2]
    grid_spec = pltpu.PrefetchScalarGridSpec(
        num_scalar_prefetch=3,
        grid=(n // tm,),
        in_specs=[pl.BlockSpec((1, 1, tm), lambda i, ps, pe, nu: (i, 0, 0), memory_space=pltpu.SMEM),
                  pl.BlockSpec((tm, ROW_W), lambda i, ps, pe, nu: (i, 0)),
                  pl.BlockSpec(memory_space=pl.ANY)],
        out_specs=[pl.BlockSpec(memory_space=pl.ANY), pl.BlockSpec(memory_space=pltpu.SMEM)],
        scratch_shapes=[pltpu.VMEM((MOE_BLOCK, ROW_W), jnp.int32), pltpu.SemaphoreType.DMA(()),
                        pltpu.SemaphoreType.DMA(())],
    )
    return pl.pallas_call(
        _dispatch_kernel,
        grid_spec=grid_spec,
        out_shape=[jax.ShapeDtypeStruct((n_slots, ROW_W), jnp.int32),
                   jax.ShapeDtypeStruct((n_slots,), jnp.int32)],
        compiler_params=_params("arbitrary"),
        name="dispatch",
    )(pad_start, pad_end, n_used, dest3, xpk, n + jnp.arange(n_slots, dtype=jnp.int32) % MOE_BLOCK)


def _experts_kernel(ba_ref, bb_ref, nused_ref, x_ref, st_prev_ref, st_cur_ref, wgu_a_ref, wgu_b_ref,
                    wd_a_ref, wd_b_ref, f_hbm, ybuf, sems, zsem):
    b = pl.program_id(0)
    last = pl.num_programs(0) - 1

    def scatter(st_ref, slot):
        for j in range(MOE_BLOCK):
            _row_copy(ybuf.at[slot], j, f_hbm, st_ref[0, 0, j], sems.at[slot]).start(priority=j % 2)

    def drain(slot):
        def wait(j, c):
            _row_copy(ybuf.at[slot], 0, f_hbm, 0, sems.at[slot]).wait()
            return c
        lax.fori_loop(0, MOE_BLOCK, wait, 0, unroll=8)

    def block_result():
        lo, hi = _unpack_bf16_pairs(x_ref[:, :D_MODEL // 2])
        lo, hi = lo.astype(BF16), hi.astype(BF16)
        gates = lax.bitcast_convert_type(x_ref[:, D_MODEL // 2:], F32)
        y = None
        for lane, (wgu_ref, wd_ref) in enumerate(((wgu_a_ref, wd_a_ref), (wgu_b_ref, wd_b_ref))):
            gu = _dot(lo, wgu_ref[0, :D_MODEL // 2, :]) + _dot(hi, wgu_ref[0, D_MODEL // 2:, :])
            gt = gu[:, :D_EXPERT]
            hid = gt * _sigmoid(gt) * gu[:, D_EXPERT:] * gates[:, lane:lane + 1]
            term = _dot(hid.astype(BF16), wd_ref[0])
            y = term if y is None else y + term
        return _pack_bf16_pairs(y)

    def step(slot):
        @pl.when(b == 0)
        def _():
            ybuf[1 - slot] = jnp.zeros(ybuf.shape[1:], ybuf.dtype)
            spare = f_hbm.at[pl.ds(f_hbm.shape[0] - MOE_BLOCK, MOE_BLOCK)]
            zero_spare = pltpu.make_async_copy(ybuf.at[1 - slot], spare, zsem)
            zero_spare.start()
            zero_spare.wait()
            ybuf[slot] = block_result()

        n_used = nused_ref[0]

        @pl.when((b >= 2) & (b - 2 < n_used))
        def _():
            drain(slot)

        @pl.when((b >= 1) & (b < n_used))
        def _():
            scatter(st_prev_ref, 1 - slot)
            ybuf[slot] = block_result()

        @pl.when((b >= 1) & (b == n_used))
        def _():
            scatter(st_prev_ref, 1 - slot)

        @pl.when(b == last)
        def _():
            @pl.when((b >= 1) & (b - 1 < n_used))
            def _():
                drain(1 - slot)

            @pl.when(b < n_used)
            def _():
                scatter(st_cur_ref, slot)
                drain(slot)

    for slot in range(2):
        @pl.when(b % 2 == slot)
        def _():
            step(slot)


def _experts_call(xs, slot_tok, block_a, block_b, n_used, n_tok, w):
    n_blocks = block_a.shape[0]
    wgu_block = (1, D_MODEL, 2 * D_EXPERT)
    wd_block = (1, D_EXPERT, D_MODEL)
    st3 = slot_tok.reshape(n_blocks, 1, MOE_BLOCK)
    grid_spec = pltpu.PrefetchScalarGridSpec(
        num_scalar_prefetch=3,
        grid=(n_blocks,),
        in_specs=[pl.BlockSpec((MOE_BLOCK, ROW_W), lambda b, ba, bb, bv: (b, 0)),
                  pl.BlockSpec((1, 1, MOE_BLOCK), lambda b, ba, bb, bv: (jnp.maximum(b - 1, 0), 0, 0),
                               memory_space=pltpu.SMEM),
                  pl.BlockSpec((1, 1, MOE_BLOCK), lambda b, ba, bb, bv: (b, 0, 0), memory_space=pltpu.SMEM),
                  pl.BlockSpec(wgu_block, lambda b, ba, bb, bv: (ba[b], 0, 0)),
                  pl.BlockSpec(wgu_block, lambda b, ba, bb, bv: (bb[b], 0, 0)),
                  pl.BlockSpec(wd_block, lambda b, ba, bb, bv: (ba[b], 0, 0)),
                  pl.BlockSpec(wd_block, lambda b, ba, bb, bv: (bb[b], 0, 0))],
        out_specs=pl.BlockSpec(memory_space=pl.ANY),
        scratch_shapes=[pltpu.VMEM((2, MOE_BLOCK, D_MODEL // 2), jnp.int32), pltpu.SemaphoreType.DMA((2,)),
                        pltpu.SemaphoreType.DMA(())],
    )
    return pl.pallas_call(
        _experts_kernel,
        grid_spec=grid_spec,
        out_shape=jax.ShapeDtypeStruct((n_tok + MOE_BLOCK, D_MODEL // 2), jnp.int32),
        compiler_params=_params("arbitrary"),
        name="experts",
    )(block_a, block_b, n_used, xs, st3, st3, w['wgu'], w['wgu'], w['wd'], w['wd'])


def _combine_kernel(x1_ref, f_ref, lng_ref, lnb_ref, o_ref):
    f = jnp.concatenate(_unpack_bf16_pairs(f_ref[...]), axis=1)
    o_ref[...] = _layer_norm(DEEPNORM_ALPHA * x1_ref[...] + f, lng_ref[...], lnb_ref[...])


def _combine_call(x1, f, w):
    n = x1.shape[0]
    tm = COMBINE_TM
    return pl.pallas_call(
        _combine_kernel,
        grid=(n // tm,),
        in_specs=[pl.BlockSpec((tm, D_MODEL), lambda i: (i, 0)),
                  pl.BlockSpec((tm, D_MODEL // 2), lambda i: (i, 0)),
                  _const_spec(w['ln2_g'].shape), _const_spec(w['ln2_b'].shape)],
        out_specs=pl.BlockSpec((tm, D_MODEL), lambda i: (i, 0)),
        out_shape=jax.ShapeDtypeStruct((n, D_MODEL), F32),
        compiler_params=_params("parallel"),
        name="combine",
    )(x1, f, w['ln2_g'], w['ln2_b'])


def _class_experts():
    pairs = [(a, b) for a in range(EXPERTS_PER_GROUP) for b in range(a + 1, EXPERTS_PER_GROUP)]
    ea = [g * EXPERTS_PER_GROUP + a for g in range(N_GROUPS) for a, _ in pairs]
    eb = [g * EXPERTS_PER_GROUP + b for g in range(N_GROUPS) for _, b in pairs]
    return jnp.asarray(ea, jnp.int32), jnp.asarray(eb, jnp.int32)


def _dispatch_plan(cls):
    n = cls.shape[1]
    n_blocks = -(-n // MOE_BLOCK) + N_CLASSES
    onehot = (cls[0][:, None] == jnp.arange(N_CLASSES, dtype=jnp.int32)).astype(jnp.int32)
    incl = jnp.cumsum(onehot, axis=0)
    counts = incl[-1]
    padded = (counts + MOE_BLOCK - 1) // MOE_BLOCK * MOE_BLOCK
    pad_end = jnp.cumsum(padded).astype(jnp.int32)
    pad_start = pad_end - padded
    slot = pad_start[None, :] + (incl - onehot)
    dest = jnp.sum(onehot * slot, axis=1).astype(jnp.int32)
    dest3 = dest.reshape(n // ROUTE_TM, 1, ROUTE_TM)
    first_slot = jnp.arange(n_blocks, dtype=jnp.int32) * MOE_BLOCK
    block_c = jnp.minimum(jnp.sum(pad_end[None, :] <= first_slot[:, None], axis=1), N_CLASSES - 1)
    ea, eb = _class_experts()
    n_used = pad_end[-1:] // MOE_BLOCK
    return dest3, pad_start, pad_end, ea[block_c], eb[block_c], n_used, n_blocks * MOE_BLOCK


def _rope_tables(seq):
    inv = 1.0 / (ROPE_BASE ** (jnp.arange(0, QK_ROPE, 2, dtype=F32) / QK_ROPE))
    ang = jnp.arange(seq, dtype=F32)[:, None] * inv[None, :]
    pad = jnp.zeros((seq, LANES - QK_ROPE), F32)
    cos, sin = jnp.cos(ang), jnp.sin(ang)
    return jnp.concatenate([cos, cos, pad], -1), jnp.concatenate([sin, sin, pad], -1)


def _pad_cols(t, width):
    return jnp.pad(t, ((0, 0), (0, width - t.shape[1])))


def _rope_weight_pair(w_rope):
    half = QK_ROPE // 2
    swapped = jnp.concatenate([-w_rope[:, half:], w_rope[:, :half]], axis=1)
    return _pad_cols(w_rope, LANES), _pad_cols(swapped, LANES)


def _layer_weights(l, w_in, a_ln_g, a_ln_b, a_ws, a_bs, na_rpb, mla_q_norm, mla_kv_norm, mla_w_uq,
                   mla_w_ukv, w_br_a, w_br_b, w_br_c, w_o, ln1_g, ln1_b, ln2_g, ln2_b, w_gate, w_up,
                   w_down):
    sizes = (2 * A_WIDTH, NA_WIDTH, NA_WIDTH, NA_WIDTH, Q_LORA, KV_LORA, QK_ROPE, N_BRANCH * D_MODEL)
    cuts = np.cumsum(sizes)[:-1].tolist()
    wa, wq, wk, wv, wcq, wckv, wkr, wg = jnp.split(w_in[l], cuts, axis=-1)
    row = lambda t: t.reshape(1, -1).astype(F32)
    uq = mla_w_uq[l].reshape(Q_LORA, MLA_HEADS, QK_NOPE + QK_ROPE)
    wqn = uq[:, :, :QK_NOPE].reshape(Q_LORA, MLA_HEADS * QK_NOPE)
    rope_pairs = [_rope_weight_pair(uq[:, h, QK_NOPE:]) for h in range(MLA_HEADS)]
    wqr = jnp.concatenate([p[0] for p in rope_pairs], axis=1)
    wqs = jnp.concatenate([p[1] for p in rope_pairs], axis=1)
    ukv = mla_w_ukv[l].reshape(KV_LORA, MLA_HEADS, QK_NOPE + V_DIM)
    wukv = jnp.concatenate([ukv[:, :, :QK_NOPE].reshape(KV_LORA, -1), ukv[:, :, QK_NOPE:].reshape(KV_LORA, -1)], 1)
    wkr_p, wkrs_p = _rope_weight_pair(wkr)
    return {
        'wa': wa.astype(BF16), 'a_ln_g': row(a_ln_g[l]), 'a_ln_b': row(a_ln_b[l]),
        'ws': a_ws[l].astype(BF16),
        'bs': jnp.broadcast_to(a_bs[l][:, :, None], (A_GROUPS, CHUNK, CHUNK)).astype(F32),
        'wqkv': jnp.concatenate([wq * NA_SCALE, wk, wv], axis=1).astype(BF16),
        'wlat': jnp.concatenate([wcq, wkr_p, wckv, wkrs_p], axis=1).astype(BF16),
        'q_norm': row(mla_q_norm[l]), 'wqn': wqn.astype(BF16), 'wqr': wqr.astype(BF16),
        'wqs': wqs.astype(BF16), 'kv_norm': row(mla_kv_norm[l]), 'wukv': wukv.astype(BF16),
        'na_bias': _natten_bias(na_rpb[l]), 'na_bmax': _natten_bias_max(na_rpb[l]),
        'wg': wg.astype(BF16),
        'wbr': jnp.stack([w_br_a[l], w_br_b[l], w_br_c[l]]).astype(BF16),
        'wo': w_o[l].astype(BF16), 'ln1_g': row(ln1_g[l]), 'ln1_b': row(ln1_b[l]),
        'ln2_g': row(ln2_g[l]), 'ln2_b': row(ln2_b[l]),
        'wgu': jnp.concatenate([w_gate[l], w_up[l]], axis=-1).astype(BF16),
        'wd': w_down[l].astype(BF16),
    }


def _shared_weights(w_router, router_bias):
    wr_t = w_router.T.astype(F32)
    wrh = wr_t.astype(BF16)
    wrl = (wr_t - wrh.astype(F32)).astype(BF16)
    rb = jnp.broadcast_to(router_bias.astype(F32)[:, None], (N_EXPERTS, MERGE_TM))
    return {'wrh': wrh, 'wrl': wrl, 'rb': rb}


def _trunk(x, layers, shared):
    bsz, seq, _ = x.shape
    cos2, sin2 = _rope_tables(seq)
    x2d = x.reshape(bsz * seq, D_MODEL)
    moe = None
    for w in layers:
        if moe is None:
            oa, q, k, v, qm, km, vm, kn2 = _pre_call(x2d, seq, cos2, sin2, w)
        else:
            oa, q, k, v, qm, km, vm, kn2, x2d = _pre_call(None, seq, cos2, sin2, w, moe)
        ob = _natten_call(q, k, v, kn2, w['na_bias'], w['na_bmax'], bsz, seq)
        oc = _mla_call(qm, km, vm, bsz, seq)
        x1, xpk, cls = _merge_call(x2d, oa, ob, oc, w, shared)
        dest3, pad_start, pad_end, block_a, block_b, n_used, n_slots = _dispatch_plan(cls)
        xs, slot_tok = _dispatch_call(xpk, dest3, pad_start, pad_end, n_used, n_slots)
        f = _experts_call(xs, slot_tok, block_a, block_b, n_used, x1.shape[0], w)
        moe = (x1, f, w)
    return _combine_call(x1, f, w).reshape(bsz, seq, D_MODEL)


def kernel(x_prompt, x_sample, w_in, a_ln_g, a_ln_b, a_ws, a_bs, na_rpb, mla_q_norm, mla_kv_norm, mla_w_uq, mla_w_ukv, w_br_a, w_br_b, w_br_c, w_o, ln1_g, ln1_b, ln2_g, ln2_b, w_router, router_bias, w_gate, w_up, w_down):
    layers = [_layer_weights(l, w_in, a_ln_g, a_ln_b, a_ws, a_bs, na_rpb, mla_q_norm, mla_kv_norm,
                             mla_w_uq, mla_w_ukv, w_br_a, w_br_b, w_br_c, w_o, ln1_g, ln1_b, ln2_g,
                             ln2_b, w_gate, w_up, w_down) for l in range(DEPTH)]
    shared = _shared_weights(w_router, router_bias)
    return (_trunk(x_prompt, layers, shared), _trunk(x_sample, layers, shared))
```

```python
import functools
import math

import jax
import jax.numpy as jnp
import numpy as np
from jax import lax
from jax.experimental import pallas as pl
from jax.experimental.pallas import tpu as pltpu

F32 = jnp.float32
BF16 = jnp.bfloat16

D_MODEL = 1024
DEPTH = 2
GRID_W = 64
CHUNK = 128
A_WIDTH = 512
A_GROUPS = 4
NA_HEADS = 8
NA_HEAD_DIM = 64
NA_WIN_H = 8
NA_WIN_W = 16
NA_WIDTH = NA_HEADS * NA_HEAD_DIM
NA_SCALE = NA_HEAD_DIM ** -0.5
MLA_HEADS = 4
Q_LORA = 384
KV_LORA = 256
QK_NOPE = 128
QK_ROPE = 64
V_DIM = 128
MLA_WIDTH = MLA_HEADS * V_DIM
MLA_SCALE = (QK_NOPE + QK_ROPE) ** -0.5
ROPE_BASE = 10000.0
N_BRANCH = 3
N_EXPERTS = 16
N_GROUPS = 4
EXPERTS_PER_GROUP = N_EXPERTS // N_GROUPS
D_EXPERT = 256
MOE_BLOCK = 256
DEEPNORM_ALPHA = (2 * DEPTH) ** 0.25
LN_EPS = 1e-5
RMS_EPS = 1e-6
NEG_INF = -1e30
LOG2E = math.log2(math.e)

VMEM_LIMIT_BYTES = 60 * 1024 * 1024
LANES = 128
MLA_HEAD_PAD = 2 * LANES
NA_ROW_BLOCK = 8
NA_TOK_BLOCK = NA_ROW_BLOCK * GRID_W

ROW_W = D_MODEL // 2 + LANES
PAIRS_PER_GROUP = EXPERTS_PER_GROUP * (EXPERTS_PER_GROUP - 1) // 2
N_CLASSES = N_GROUPS * PAIRS_PER_GROUP
HIGH_HALF = -65536
PRE_TM = 1024
MERGE_TM = 1024
ROUTE_TM = 256
COMBINE_TM = 512
MLA_TQ = 2048
MLA_TK = 1024
MLA_MAX_UNROLL = 5
MLA_SHIFT_LANE = QK_NOPE + QK_ROPE
MLA_L_MIN = 2.0 ** -60
MLA_L_MAX = 2.0 ** 100
MLA_BOUND_SLACK = 1.0 + 2.0 ** -7
NA_BOUND_SLACK = 1.0 + 2.0 ** -6
NA_L_MIN = 2.0 ** -60
NA_L_MAX = 2.0 ** 100


def _params(*sem):
    return pltpu.CompilerParams(dimension_semantics=sem, vmem_limit_bytes=VMEM_LIMIT_BYTES)


def _const_spec(shape):
    nd = len(shape)
    return pl.BlockSpec(shape, lambda *_: (0,) * nd, pipeline_mode=pl.Buffered(1))


def _layer_norm(y, g, b):
    mu = jnp.mean(y, -1, keepdims=True)
    yc = y - mu
    var = jnp.mean(yc * yc, -1, keepdims=True)
    return yc * lax.rsqrt(var + LN_EPS) * g + b


def _rms_norm(y, g):
    return y * lax.rsqrt(jnp.mean(y * y, -1, keepdims=True) + RMS_EPS) * g


def _gelu_tanh(x):
    return 0.5 * x * (1.0 + jnp.tanh(math.sqrt(2.0 / math.pi) * (x + 0.044715 * (x * x * x))))


def _sigmoid(x):
    return 1.0 / (1.0 + jnp.exp(-x))


def _pack_bf16_pairs(y):
    w = y.shape[1] // 2
    lo = lax.bitcast_convert_type(y[:, :w].astype(BF16).astype(F32), jnp.int32)
    hi = lax.bitcast_convert_type(y[:, w:].astype(BF16).astype(F32), jnp.int32)
    return lax.shift_right_logical(lo, 16) | (hi & HIGH_HALF)


def _unpack_bf16_pairs(p):
    lo = lax.bitcast_convert_type(lax.shift_left(p, 16), F32)
    hi = lax.bitcast_convert_type(p & HIGH_HALF, F32)
    return lo, hi


def _dot(a, b):
    return jnp.dot(a, b, preferred_element_type=F32)


def _dot_nt(a, b):
    return lax.dot_general(a, b, (((1,), (1,)), ((), ())), preferred_element_type=F32)


def _pre_kernel(*refs, after_moe):
    if after_moe:
        x1_ref, f_ref, ln2g_ref, ln2b_ref = refs[:4]
        refs, x2_ref = refs[4:-1], refs[-1]
        f = jnp.concatenate(_unpack_bf16_pairs(f_ref[...]), axis=1)
        x = _layer_norm(DEEPNORM_ALPHA * x1_ref[...] + f, ln2g_ref[...], ln2b_ref[...])
        x2_ref[...] = x
    else:
        x, refs = refs[0][...], refs[1:]
    (cos_ref, sin_ref, wa_ref, lng_ref, lnb_ref, ws_ref, bs_ref, wqkv_ref, wlat_ref, qnorm_ref, wqn_ref,
     wqr_ref, wqs_ref, kvnorm_ref, wukv_ref, hsum_ref,
     oa_ref, q_ref, k_ref, v_ref, qm_ref, km_ref, vm_ref, kn2_ref) = refs
    tm = x.shape[0]
    xb = x.astype(BF16)
    cos2 = cos_ref[...]
    sin2 = sin_ref[...]

    a = _gelu_tanh(_dot(xb, wa_ref[...]))
    u = a[:, :A_WIDTH]
    vn = _layer_norm(a[:, A_WIDTH:], lng_ref[...], lnb_ref[...]).astype(BF16)
    for c in range(tm // CHUNK):
        rs = slice(c * CHUNK, (c + 1) * CHUNK)
        for g in range(A_GROUPS):
            cs = slice(g * CHUNK, (g + 1) * CHUNK)
            sp = _dot(ws_ref[g], vn[rs, cs]) + bs_ref[g]
            oa_ref[rs, cs] = (u[rs, cs] * sp).astype(BF16)

    qkv = _dot(xb, wqkv_ref[...])
    q_ref[...] = qkv[:, :NA_WIDTH].astype(BF16)
    kb = qkv[:, NA_WIDTH:2 * NA_WIDTH].astype(BF16)
    k_ref[...] = kb
    v_ref[...] = qkv[:, 2 * NA_WIDTH:].astype(BF16)
    kf = kb.astype(F32)
    kn2_ref[...] = _dot((kf * kf).astype(BF16), hsum_ref[...])

    lat = _dot(xb, wlat_ref[...])
    c_q = lat[:, :Q_LORA]
    kr = lat[:, Q_LORA:Q_LORA + LANES]
    c_kv = lat[:, Q_LORA + LANES:Q_LORA + LANES + KV_LORA]
    krs = lat[:, Q_LORA + LANES + KV_LORA:]
    cqn = _rms_norm(c_q, qnorm_ref[...]).astype(BF16)
    qn = _dot(cqn, wqn_ref[...])
    qr = _dot(cqn, wqr_ref[...])
    qs = _dot(cqn, wqs_ref[...])
    qscale = MLA_SCALE * LOG2E
    shift_lane = lax.broadcasted_iota(jnp.int32, (tm, LANES), 1) == MLA_SHIFT_LANE - QK_NOPE
    for h in range(MLA_HEADS):
        hs = slice(h * LANES, (h + 1) * LANES)
        nope = (qn[:, hs] * qscale).astype(BF16)
        rot = ((qr[:, hs] * cos2 + qs[:, hs] * sin2) * qscale).astype(BF16)
        nf, rf = nope.astype(F32), rot.astype(F32)
        norm = jnp.sqrt(jnp.sum(nf * nf, -1, keepdims=True) + jnp.sum(rf * rf, -1, keepdims=True))
        norm = jnp.broadcast_to(norm * MLA_BOUND_SLACK, rot.shape).astype(BF16)
        qm_ref[:, h * MLA_HEAD_PAD:h * MLA_HEAD_PAD + LANES] = nope
        qm_ref[:, h * MLA_HEAD_PAD + LANES:(h + 1) * MLA_HEAD_PAD] = jnp.where(shift_lane, norm, rot)

    ckvn = _rms_norm(c_kv, kvnorm_ref[...]).astype(BF16)
    kv = _dot(ckvn, wukv_ref[...])
    krope = kr * cos2 + krs * sin2
    lane = lax.broadcasted_iota(jnp.int32, krope.shape, 1)
    krope = jnp.where(lane == MLA_SHIFT_LANE - QK_NOPE, -1.0, krope).astype(BF16)
    for h in range(MLA_HEADS):
        km_ref[:, h * MLA_HEAD_PAD:h * MLA_HEAD_PAD + LANES] = kv[:, h * LANES:(h + 1) * LANES].astype(BF16)
        km_ref[:, h * MLA_HEAD_PAD + LANES:(h + 1) * MLA_HEAD_PAD] = krope
    vm_ref[...] = kv[:, MLA_HEADS * QK_NOPE:].astype(BF16)


def _head_sum_matrix():
    feat = np.arange(NA_WIDTH)[:, None] // NA_HEAD_DIM
    return jnp.asarray(feat == np.arange(LANES)[None, :], BF16)


def _pre_call(x2d, seq, cos2, sin2, w, moe=None):
    n = x2d.shape[0] if moe is None else moe[0].shape[0]
    tm = PRE_TM
    per_seq = seq // tm
    weights = [w['wa'], w['a_ln_g'], w['a_ln_b'], w['ws'], w['bs'], w['wqkv'], w['wlat'], w['q_norm'],
               w['wqn'], w['wqr'], w['wqs'], w['kv_norm'], w['wukv'], _head_sum_matrix()]
    tok = lambda width: pl.BlockSpec((tm, width), lambda i: (i, 0))
    pos = pl.BlockSpec((tm, LANES), lambda i: (i % per_seq, 0))
    out_widths = [A_WIDTH, NA_WIDTH, NA_WIDTH, NA_WIDTH, MLA_HEADS * MLA_HEAD_PAD,
                  MLA_HEADS * MLA_HEAD_PAD, MLA_WIDTH]
    out_specs = [tok(wd) for wd in out_widths] + [tok(LANES)]
    out_shape = ([jax.ShapeDtypeStruct((n, wd), BF16) for wd in out_widths]
                 + [jax.ShapeDtypeStruct((n, LANES), F32)])
    if moe is None:
        stream, stream_specs = [x2d], [tok(D_MODEL)]
    else:
        x1, f, prev = moe
        stream = [x1, f, prev['ln2_g'], prev['ln2_b']]
        stream_specs = [tok(D_MODEL), tok(D_MODEL // 2), _const_spec(prev['ln2_g'].shape),
                        _const_spec(prev['ln2_b'].shape)]
        out_specs.append(tok(D_MODEL))
        out_shape.append(jax.ShapeDtypeStruct((n, D_MODEL), F32))
    return pl.pallas_call(
        functools.partial(_pre_kernel, after_moe=moe is not None),
        grid=(n // tm,),
        in_specs=stream_specs + [pos, pos] + [_const_spec(t.shape) for t in weights],
        out_specs=out_specs,
        out_shape=out_shape,
        compiler_params=_params("parallel"),
        name="pre",
    )(*stream, cos2, sin2, *weights)


def _natten_kernel(q_ref, kp_ref, kc_ref, kn_ref, vp_ref, vc_ref, vn_ref, np_ref, nc_ref, nn_ref,
                   bias_ref, bmax_ref, o_ref, kcat, vcat, *, rows):
    j = pl.program_id(1)
    nt = NA_TOK_BLOCK
    kcat[0:nt] = kp_ref[...]
    kcat[nt:2 * nt] = kc_ref[...]
    kcat[2 * nt:3 * nt] = kn_ref[...]
    vcat[0:nt] = vp_ref[...]
    vcat[nt:2 * nt] = vc_ref[...]
    vcat[2 * nt:3 * nt] = vn_ref[...]
    r0 = j * NA_ROW_BLOCK
    lane_lo = lax.broadcasted_iota(jnp.int32, (GRID_W, LANES), 1) < NA_HEAD_DIM

    kmax2 = jnp.maximum(jnp.maximum(jnp.max(np_ref[...], 0, keepdims=True),
                                    jnp.max(nc_ref[...], 0, keepdims=True)),
                        jnp.max(nn_ref[...], 0, keepdims=True))
    kmax = jnp.broadcast_to(jnp.sqrt(kmax2) * NA_BOUND_SLACK, (2 * GRID_W, LANES))
    row_lo = lax.broadcasted_iota(jnp.int32, (2 * GRID_W, LANES), 0) < GRID_W
    lane = lax.broadcasted_iota(jnp.int32, (2 * GRID_W, LANES), 1)
    kcols = [jnp.sum(jnp.where(lane == jnp.where(row_lo, 2 * hp, 2 * hp + 1), kmax, 0.0), -1, keepdims=True)
             for hp in range(NA_HEADS // 2)]

    def attend(exact):
        bad = jnp.zeros((2 * GRID_W, 1), F32)
        for i in range(NA_ROW_BLOCK):
            r = r0 + i
            start = jnp.clip(r - NA_WIN_H // 2, 0, rows - NA_WIN_H)
            pat = start - r + (NA_WIN_H - 1)
            koff = pl.multiple_of((start - r0 + NA_ROW_BLOCK) * GRID_W, GRID_W)
            qs = slice(i * GRID_W, (i + 1) * GRID_W)
            for hp in range(NA_HEADS // 2):
                cs = slice(hp * LANES, (hp + 1) * LANES)
                qp = q_ref[qs, cs]
                zero = jnp.zeros_like(qp)
                q2 = jnp.concatenate([jnp.where(lane_lo, qp, zero), jnp.where(lane_lo, zero, qp)], axis=0)
                kk = kcat[pl.ds(koff, NA_WIN_H * GRID_W), cs]
                vv = vcat[pl.ds(koff, NA_WIN_H * GRID_W), cs]
                s = _dot_nt(q2, kk) + bias_ref[pat, hp]
                if exact:
                    m = jnp.max(s, -1, keepdims=True)
                else:
                    qf = q2.astype(F32)
                    m = jnp.sqrt(jnp.sum(qf * qf, -1, keepdims=True)) * kcols[hp] + bmax_ref[hp][:, 0:1]
                e = jnp.exp(s - m)
                l = jnp.sum(e, -1, keepdims=True)
                if not exact:
                    bad = bad + jnp.where((l >= NA_L_MIN) & (l <= NA_L_MAX), 0.0, 1.0)
                o2 = _dot(e.astype(BF16), vv) / l
                o_ref[qs, cs] = jnp.where(lane_lo, o2[:GRID_W], o2[GRID_W:]).astype(BF16)
        return bad

    n_bad = jnp.sum(attend(exact=False))

    @pl.when(n_bad > 0.0)
    def _():
        attend(exact=True)


def _natten_call(q, k, v, knorm, bias, bmax, bsz, seq):
    rows = seq // GRID_W
    nrb = rows // NA_ROW_BLOCK
    nt = NA_TOK_BLOCK
    cur = lambda width: pl.BlockSpec((nt, width), lambda b, j: (b * nrb + j, 0))
    prev = lambda width: pl.BlockSpec((nt, width), lambda b, j: (b * nrb + jnp.maximum(j - 1, 0), 0))
    nxt = lambda width: pl.BlockSpec((nt, width), lambda b, j: (b * nrb + jnp.minimum(j + 1, nrb - 1), 0))
    w = NA_WIDTH
    return pl.pallas_call(
        functools.partial(_natten_kernel, rows=rows),
        grid=(bsz, nrb),
        in_specs=[cur(w), prev(w), cur(w), nxt(w), prev(w), cur(w), nxt(w),
                  prev(LANES), cur(LANES), nxt(LANES), _const_spec(bias.shape), _const_spec(bmax.shape)],
        out_specs=cur(w),
        out_shape=jax.ShapeDtypeStruct((bsz * seq, NA_WIDTH), BF16),
        scratch_shapes=[pltpu.VMEM((3 * nt, NA_WIDTH), BF16), pltpu.VMEM((3 * nt, NA_WIDTH), BF16)],
        compiler_params=_params("parallel", "parallel"),
        name="natten",
    )(q, k, k, k, v, v, v, knorm, knorm, knorm, bias, bmax)


def _natten_bias_max(rpb):
    m = jnp.max(rpb.astype(F32), axis=(1, 2)).reshape(NA_HEADS // 2, 2, 1, 1)
    return jnp.broadcast_to(m, (NA_HEADS // 2, 2, GRID_W, LANES)).reshape(NA_HEADS // 2, 2 * GRID_W, LANES)


def _natten_bias(rpb):
    c = jnp.arange(GRID_W)
    col_start = jnp.clip(c - NA_WIN_W // 2, 0, GRID_W - NA_WIN_W)
    in_win = (c[None, :] >= col_start[:, None]) & (c[None, :] < col_start[:, None] + NA_WIN_W)
    dc = jnp.clip(c[None, :] - c[:, None] + (NA_WIN_W - 1), 0, 2 * NA_WIN_W - 2)
    onehot = (dc[:, :, None] == jnp.arange(2 * NA_WIN_W - 1)).astype(F32)
    t = jnp.einsum('hrd,qkd->hrqk', rpb.astype(F32), onehot, precision=lax.Precision.HIGHEST)
    t = jnp.where(in_win[None, None], t, NEG_INF)
    b = jnp.stack([t[:, p:p + NA_WIN_H] for p in range(NA_WIN_H)])
    b = b.transpose(0, 1, 3, 2, 4)
    return b.reshape(NA_WIN_H, NA_HEADS // 2, 2 * GRID_W, NA_WIN_H * GRID_W)


def _mla_online_softmax(q, k_ref, v_ref, tk):
    tq = q.shape[0]

    def body(j, carry):
        m, l, acc = carry
        ks = pl.ds(pl.multiple_of(j * tk, tk), tk)
        s = _dot_nt(q, k_ref[ks, :])
        m_new = jnp.maximum(m, jnp.max(s, -1, keepdims=True))
        alpha = jnp.exp2(m - m_new)
        p = jnp.exp2(s - m_new)
        l = alpha * l + jnp.sum(p, -1, keepdims=True)
        acc = alpha * acc + _dot(p.astype(BF16), v_ref[ks, :])
        return m_new, l, acc

    init = (jnp.full((tq, 1), NEG_INF, F32), jnp.zeros((tq, 1), F32), jnp.zeros((tq, V_DIM), F32))
    _, l, acc = lax.fori_loop(0, k_ref.shape[0] // tk, body, init)
    return acc, l


def _mla_kernel(q_ref, k_ref, v_ref, o_ref, kmax_ref, *, tk, unroll):
    q = q_ref[...]
    tq = q.shape[0]
    seq = k_ref.shape[0]

    @pl.when(pl.program_id(2) == 0)
    def _():
        def body(c, mx):
            kc = k_ref[pl.ds(pl.multiple_of(c * tk, tk), tk), :].astype(F32)
            n2 = jnp.sum(kc * kc, -1, keepdims=True)
            return jnp.maximum(mx, jnp.max(n2, 0, keepdims=True))

        mx = lax.fori_loop(0, seq // tk, body, jnp.zeros((1, 1), F32))
        kmax = jnp.broadcast_to(jnp.sqrt(mx) * MLA_BOUND_SLACK, kmax_ref.shape)
        lane = lax.broadcasted_iota(jnp.int32, kmax_ref.shape, 1)
        kmax_ref[...] = jnp.where(lane == MLA_SHIFT_LANE, kmax, 1.0).astype(BF16)

    q_shift = q * kmax_ref[0:1, :]
    ones = jnp.ones((tk, LANES), BF16)

    def body(j, acc):
        ks = pl.ds(pl.multiple_of(j * tk, tk), tk)
        p = jnp.exp2(_dot_nt(q_shift, k_ref[ks, :])).astype(BF16)
        return acc + _dot(p, jnp.concatenate([v_ref[ks, :], ones], axis=1))

    nk = seq // tk
    acc = lax.fori_loop(0, nk - 1, body, jnp.zeros((tq, V_DIM + LANES), F32), unroll=unroll)
    l_part = acc[:, V_DIM:]
    usable = (l_part >= MLA_L_MIN) & (l_part <= MLA_L_MAX)
    n_bad = jnp.sum(jnp.where(usable, 0.0, 1.0))
    acc = body(nk - 1, acc)
    l = acc[:, V_DIM:]
    o_ref[...] = (acc[:, :V_DIM] / l).astype(BF16)

    @pl.when(n_bad > 0.0)
    def _():
        acc, l = _mla_online_softmax(q, k_ref, v_ref, tk)
        o_ref[...] = (acc / l).astype(BF16)


def _mla_call(qm, km, vm, bsz, seq):
    tq = min(MLA_TQ, seq)
    nq = seq // tq
    tk = min(MLA_TK, seq // 2)
    looped = seq // tk - 1
    unroll = max(u for u in range(1, MLA_MAX_UNROLL + 1) if looped % u == 0)
    return pl.pallas_call(
        functools.partial(_mla_kernel, tk=tk, unroll=unroll),
        grid=(bsz, MLA_HEADS, nq),
        in_specs=[pl.BlockSpec((tq, MLA_HEAD_PAD), lambda b, h, i: (b * nq + i, h)),
                  pl.BlockSpec((seq, MLA_HEAD_PAD), lambda b, h, i: (b, h)),
                  pl.BlockSpec((seq, V_DIM), lambda b, h, i: (b, h))],
        out_specs=pl.BlockSpec((tq, V_DIM), lambda b, h, i: (b * nq + i, h)),
        out_shape=jax.ShapeDtypeStruct((bsz * seq, MLA_WIDTH), BF16),
        scratch_shapes=[pltpu.VMEM((16, MLA_HEAD_PAD), BF16)],
        compiler_params=_params("parallel", "parallel", "arbitrary"),
        name="mla",
    )(qm, km, vm)


def _first_argmax(vals):
    best, idx = vals[0], jnp.zeros(vals[0].shape, jnp.int32)
    for i in range(1, len(vals)):
        better = vals[i] > best
        idx = jnp.where(better, i, idx)
        best = jnp.where(better, vals[i], best)
    return best, idx


def _select(idx, vals):
    out = vals[0]
    for i in range(1, len(vals)):
        out = jnp.where(idx == i, vals[i], out)
    return out


def _merge_kernel(x_ref, oa_ref, ob_ref, oc_ref, wg_ref, wbr_ref, wo_ref, lng_ref, lnb_ref,
                  wrh_ref, wrl_ref, rb_ref, x1_ref, xpk_ref, cls_ref):
    x = x_ref[...]
    xb = x.astype(BF16)
    merged = None
    for i, o_ref in enumerate((oa_ref, ob_ref, oc_ref)):
        g = _sigmoid(_dot(xb, wg_ref[:, i * D_MODEL:(i + 1) * D_MODEL]))
        term = g * _dot(o_ref[...], wbr_ref[i])
        merged = term if merged is None else merged + term
    m = _dot(merged.astype(BF16), wo_ref[...])
    x1 = _layer_norm(DEEPNORM_ALPHA * x + m, lng_ref[...], lnb_ref[...])
    x1_ref[...] = x1

    x1h = x1.astype(BF16)
    x1l = (x1 - x1h.astype(F32)).astype(BF16)
    logits = _dot_nt(wrh_ref[...], x1h) + (_dot_nt(wrl_ref[...], x1h) + _dot_nt(wrh_ref[...], x1l))
    scores = _sigmoid(logits)
    biased = scores + rb_ref[...]
    sc = [scores[e:e + 1, :] for e in range(N_EXPERTS)]
    bi = [biased[e:e + 1, :] for e in range(N_EXPERTS)]
    grp_scores = []
    for g in range(N_GROUPS):
        v = bi[g * EXPERTS_PER_GROUP:(g + 1) * EXPERTS_PER_GROUP]
        top2 = None
        for i in range(EXPERTS_PER_GROUP):
            for k in range(i + 1, EXPERTS_PER_GROUP):
                top2 = v[i] + v[k] if top2 is None else jnp.maximum(top2, v[i] + v[k])
        grp_scores.append(top2)
    _, grp = _first_argmax(grp_scores)
    cand = [_select(grp, [bi[g * EXPERTS_PER_GROUP + i] for g in range(N_GROUPS)])
            for i in range(EXPERTS_PER_GROUP)]
    cand_sc = [_select(grp, [sc[g * EXPERTS_PER_GROUP + i] for g in range(N_GROUPS)])
               for i in range(EXPERTS_PER_GROUP)]
    _, i1 = _first_argmax(cand)
    _, i2 = _first_argmax([jnp.where(i1 == i, -jnp.inf, cand[i]) for i in range(EXPERTS_PER_GROUP)])
    s1 = _select(i1, cand_sc)
    s2 = _select(i2, cand_sc)
    tot = s1 + s2
    g1 = s1 / tot
    g2 = s2 / tot
    first_lower = i1 < i2
    a = jnp.where(first_lower, i1, i2)
    b = jnp.where(first_lower, i2, i1)
    cls_ref[...] = grp * PAIRS_PER_GROUP + (((a * (7 - a)) >> 1) + (b - a - 1))
    xpk_ref[:, :D_MODEL // 2] = _pack_bf16_pairs(x1)
    tm = x.shape[0]
    sub = lax.broadcasted_iota(jnp.int32, (LANES, tm), 0)
    ga = jnp.broadcast_to(jnp.where(first_lower, g1, g2), (LANES, tm))
    gb = jnp.broadcast_to(jnp.where(first_lower, g2, g1), (LANES, tm))
    gmat = jnp.where(sub == 0, ga, jnp.where(sub == 1, gb, 0.0))
    xpk_ref[:, D_MODEL // 2:] = lax.bitcast_convert_type(gmat.T, jnp.int32)


def _merge_call(x2d, oa, ob, oc, w, shared):
    n = x2d.shape[0]
    tm = MERGE_TM
    weights = [w['wg'], w['wbr'], w['wo'], w['ln1_g'], w['ln1_b'], shared['wrh'], shared['wrl'], shared['rb']]
    tok = lambda width: pl.BlockSpec((tm, width), lambda i: (i, 0))
    return pl.pallas_call(
        _merge_kernel,
        grid=(n // tm,),
        in_specs=[tok(D_MODEL), tok(A_WIDTH), tok(NA_WIDTH), tok(MLA_WIDTH)]
                 + [_const_spec(t.shape) for t in weights],
        out_specs=[tok(D_MODEL), tok(ROW_W), pl.BlockSpec((1, tm), lambda i: (0, i))],
        out_shape=[jax.ShapeDtypeStruct((n, D_MODEL), F32),
                   jax.ShapeDtypeStruct((n, ROW_W), jnp.int32),
                   jax.ShapeDtypeStruct((1, n), jnp.int32)],
        compiler_params=_params("parallel"),
        name="merge",
    )(x2d, oa, ob, oc, *weights)


def _row_copy(src, src_row, dst, dst_row, sem):
    return pltpu.make_async_copy(src.at[pl.ds(src_row, 1)], dst.at[pl.ds(dst_row, 1)], sem)


def _dispatch_kernel(pstart_ref, pend_ref, nused_ref, dest_ref, x_ref, spare_hbm, xs_hbm, st_ref, zbuf, zsem, sem):
    i = pl.program_id(0)
    tm = dest_ref.shape[2]
    n_slots = xs_hbm.shape[0]
    n_blocks = n_slots // MOE_BLOCK

    @pl.when(i == 0)
    def _():
        no_owner = pltpu.make_async_copy(spare_hbm, st_ref, zsem)
        no_owner.start()
        no_owner.wait()
        zbuf[...] = jnp.zeros_like(zbuf)

        def zero_block(first_slot):
            dst = xs_hbm.at[pl.ds(pl.multiple_of(first_slot, MOE_BLOCK), MOE_BLOCK)]
            return pltpu.make_async_copy(zbuf, dst, zsem)

        def for_each_zero_block(fn):
            for g in range(N_CLASSES):
                @pl.when(pend_ref[g] > pstart_ref[g])
                def _():
                    fn(zero_block(pend_ref[g] - MOE_BLOCK))

                @pl.when(nused_ref[0] + g < n_blocks)
                def _():
                    fn(zero_block((nused_ref[0] + g) * MOE_BLOCK))

        for_each_zero_block(lambda cp: cp.start())
        for_each_zero_block(lambda cp: cp.wait())

    def wait(j, c):
        _row_copy(x_ref, 0, xs_hbm, 0, sem).wait()
        return c

    for j in range(tm):
        d = dest_ref[0, 0, j]
        _row_copy(x_ref, j, xs_hbm, d, sem).start(priority=j % 2)
        st_ref[d] = i * tm + j
    lax.fori_loop(0, tm, wait, 0, unroll=8)


def _dispatch_call(xpk, dest3, pad_start, pad_end, n_used, n_slots):
    n = xpk.shape[0]
    tm = dest3.shape[2]
    grid_spec = pltpu.PrefetchScalarGridSpec(
        num_scalar_prefetch=3,
        grid=(n // tm,),
        in_specs=[pl.BlockSpec((1, 1, tm), lambda i, ps, pe, nu: (i, 0, 0), memory_space=pltpu.SMEM),
                  pl.BlockSpec((tm, ROW_W), lambda i, ps, pe, nu: (i, 0)),
                  pl.BlockSpec(memory_space=pl.ANY)],
        out_specs=[pl.BlockSpec(memory_space=pl.ANY), pl.BlockSpec(memory_space=pltpu.SMEM)],
        scratch_shapes=[pltpu.VMEM((MOE_BLOCK, ROW_W), jnp.int32), pltpu.SemaphoreType.DMA(()),
                        pltpu.SemaphoreType.DMA(())],
    )
    return pl.pallas_call(
        _dispatch_kernel,
        grid_spec=grid_spec,
        out_shape=[jax.ShapeDtypeStruct((n_slots, ROW_W), jnp.int32),
                   jax.ShapeDtypeStruct((n_slots,), jnp.int32)],
        compiler_params=_params("arbitrary"),
        name="dispatch",
    )(pad_start, pad_end, n_used, dest3, xpk, n + jnp.arange(n_slots, dtype=jnp.int32) % MOE_BLOCK)


def _experts_kernel(ba_ref, bb_ref, nused_ref, x_ref, st_prev_ref, st_cur_ref, wgu_a_ref, wgu_b_ref,
                    wd_a_ref, wd_b_ref, f_hbm, ybuf, sems, zsem):
    b = pl.program_id(0)
    last = pl.num_programs(0) - 1

    def scatter(st_ref, slot):
        for j in range(MOE_BLOCK):
            _row_copy(ybuf.at[slot], j, f_hbm, st_ref[0, 0, j], sems.at[slot]).start(priority=j % 2)

    def drain(slot):
        def wait(j, c):
            _row_copy(ybuf.at[slot], 0, f_hbm, 0, sems.at[slot]).wait()
            return c
        lax.fori_loop(0, MOE_BLOCK, wait, 0, unroll=8)

    def block_result():
        lo, hi = _unpack_bf16_pairs(x_ref[:, :D_MODEL // 2])
        lo, hi = lo.astype(BF16), hi.astype(BF16)
        gates = lax.bitcast_convert_type(x_ref[:, D_MODEL // 2:], F32)
        y = None
        for lane, (wgu_ref, wd_ref) in enumerate(((wgu_a_ref, wd_a_ref), (wgu_b_ref, wd_b_ref))):
            gu = _dot(lo, wgu_ref[0, :D_MODEL // 2, :]) + _dot(hi, wgu_ref[0, D_MODEL // 2:, :])
            gt = gu[:, :D_EXPERT]
            hid = gt * _sigmoid(gt) * gu[:, D_EXPERT:] * gates[:, lane:lane + 1]
            term = _dot(hid.astype(BF16), wd_ref[0])
            y = term if y is None else y + term
        return _pack_bf16_pairs(y)

    def step(slot):
        @pl.when(b == 0)
        def _():
            ybuf[1 - slot] = jnp.zeros(ybuf.shape[1:], ybuf.dtype)
            spare = f_hbm.at[pl.ds(f_hbm.shape[0] - MOE_BLOCK, MOE_BLOCK)]
            zero_spare = pltpu.make_async_copy(ybuf.at[1 - slot], spare, zsem)
            zero_spare.start()
            zero_spare.wait()
            ybuf[slot] = block_result()

        n_used = nused_ref[0]

        @pl.when((b >= 2) & (b - 2 < n_used))
        def _():
            drain(slot)

        @pl.when((b >= 1) & (b < n_used))
        def _():
            scatter(st_prev_ref, 1 - slot)
            ybuf[slot] = block_result()

        @pl.when((b >= 1) & (b == n_used))
        def _():
            scatter(st_prev_ref, 1 - slot)

        @pl.when(b == last)
        def _():
            @pl.when((b >= 1) & (b - 1 < n_used))
            def _():
                drain(1 - slot)

            @pl.when(b < n_used)
            def _():
                scatter(st_cur_ref, slot)
                drain(slot)

    for slot in range(2):
        @pl.when(b % 2 == slot)
        def _():
            step(slot)


def _experts_call(xs, slot_tok, block_a, block_b, n_used, n_tok, w):
    n_blocks = block_a.shape[0]
    wgu_block = (1, D_MODEL, 2 * D_EXPERT)
    wd_block = (1, D_EXPERT, D_MODEL)
    st3 = slot_tok.reshape(n_blocks, 1, MOE_BLOCK)
    grid_spec = pltpu.PrefetchScalarGridSpec(
        num_scalar_prefetch=3,
        grid=(n_blocks,),
        in_specs=[pl.BlockSpec((MOE_BLOCK, ROW_W), lambda b, ba, bb, bv: (b, 0)),
                  pl.BlockSpec((1, 1, MOE_BLOCK), lambda b, ba, bb, bv: (jnp.maximum(b - 1, 0), 0, 0),
                               memory_space=pltpu.SMEM),
                  pl.BlockSpec((1, 1, MOE_BLOCK), lambda b, ba, bb, bv: (b, 0, 0), memory_space=pltpu.SMEM),
                  pl.BlockSpec(wgu_block, lambda b, ba, bb, bv: (ba[b], 0, 0)),
                  pl.BlockSpec(wgu_block, lambda b, ba, bb, bv: (bb[b], 0, 0)),
                  pl.BlockSpec(wd_block, lambda b, ba, bb, bv: (ba[b], 0, 0)),
                  pl.BlockSpec(wd_block, lambda b, ba, bb, bv: (bb[b], 0, 0))],
        out_specs=pl.BlockSpec(memory_space=pl.ANY),
        scratch_shapes=[pltpu.VMEM((2, MOE_BLOCK, D_MODEL // 2), jnp.int32), pltpu.SemaphoreType.DMA((2,)),
                        pltpu.SemaphoreType.DMA(())],
    )
    return pl.pallas_call(
        _experts_kernel,
        grid_spec=grid_spec,
        out_shape=jax.ShapeDtypeStruct((n_tok + MOE_BLOCK, D_MODEL // 2), jnp.int32),
        compiler_params=_params("arbitrary"),
        name="experts",
    )(block_a, block_b, n_used, xs, st3, st3, w['wgu'], w['wgu'], w['wd'], w['wd'])


def _combine_kernel(x1_ref, f_ref, lng_ref, lnb_ref, o_ref):
    f = jnp.concatenate(_unpack_bf16_pairs(f_ref[...]), axis=1)
    o_ref[...] = _layer_norm(DEEPNORM_ALPHA * x1_ref[...] + f, lng_ref[...], lnb_ref[...])


def _combine_call(x1, f, w):
    n = x1.shape[0]
    tm = COMBINE_TM
    return pl.pallas_call(
        _combine_kernel,
        grid=(n // tm,),
        in_specs=[pl.BlockSpec((tm, D_MODEL), lambda i: (i, 0)),
                  pl.BlockSpec((tm, D_MODEL // 2), lambda i: (i, 0)),
                  _const_spec(w['ln2_g'].shape), _const_spec(w['ln2_b'].shape)],
        out_specs=pl.BlockSpec((tm, D_MODEL), lambda i: (i, 0)),
        out_shape=jax.ShapeDtypeStruct((n, D_MODEL), F32),
        compiler_params=_params("parallel"),
        name="combine",
    )(x1, f, w['ln2_g'], w['ln2_b'])


def _class_experts():
    pairs = [(a, b) for a in range(EXPERTS_PER_GROUP) for b in range(a + 1, EXPERTS_PER_GROUP)]
    ea = [g * EXPERTS_PER_GROUP + a for g in range(N_GROUPS) for a, _ in pairs]
    eb = [g * EXPERTS_PER_GROUP + b for g in range(N_GROUPS) for _, b in pairs]
    return jnp.asarray(ea, jnp.int32), jnp.asarray(eb, jnp.int32)


def _dispatch_plan(cls):
    n = cls.shape[1]
    n_blocks = -(-n // MOE_BLOCK) + N_CLASSES
    onehot = (cls[0][:, None] == jnp.arange(N_CLASSES, dtype=jnp.int32)).astype(jnp.int32)
    incl = jnp.cumsum(onehot, axis=0)
    counts = incl[-1]
    padded = (counts + MOE_BLOCK - 1) // MOE_BLOCK * MOE_BLOCK
    pad_end = jnp.cumsum(padded).astype(jnp.int32)
    pad_start = pad_end - padded
    slot = pad_start[None, :] + (incl - onehot)
    dest = jnp.sum(onehot * slot, axis=1).astype(jnp.int32)
    dest3 = dest.reshape(n // ROUTE_TM, 1, ROUTE_TM)
    first_slot = jnp.arange(n_blocks, dtype=jnp.int32) * MOE_BLOCK
    block_c = jnp.minimum(jnp.sum(pad_end[None, :] <= first_slot[:, None], axis=1), N_CLASSES - 1)
    ea, eb = _class_experts()
    n_used = pad_end[-1:] // MOE_BLOCK
    return dest3, pad_start, pad_end, ea[block_c], eb[block_c], n_used, n_blocks * MOE_BLOCK


def _rope_tables(seq):
    inv = 1.0 / (ROPE_BASE ** (jnp.arange(0, QK_ROPE, 2, dtype=F32) / QK_ROPE))
    ang = jnp.arange(seq, dtype=F32)[:, None] * inv[None, :]
    pad = jnp.zeros((seq, LANES - QK_ROPE), F32)
    cos, sin = jnp.cos(ang), jnp.sin(ang)
    return jnp.concatenate([cos, cos, pad], -1), jnp.concatenate([sin, sin, pad], -1)


def _pad_cols(t, width):
    return jnp.pad(t, ((0, 0), (0, width - t.shape[1])))


def _rope_weight_pair(w_rope):
    half = QK_ROPE // 2
    swapped = jnp.concatenate([-w_rope[:, half:], w_rope[:, :half]], axis=1)
    return _pad_cols(w_rope, LANES), _pad_cols(swapped, LANES)


def _layer_weights(l, w_in, a_ln_g, a_ln_b, a_ws, a_bs, na_rpb, mla_q_norm, mla_kv_norm, mla_w_uq,
                   mla_w_ukv, w_br_a, w_br_b, w_br_c, w_o, ln1_g, ln1_b, ln2_g, ln2_b, w_gate, w_up,
                   w_down):
    sizes = (2 * A_WIDTH, NA_WIDTH, NA_WIDTH, NA_WIDTH, Q_LORA, KV_LORA, QK_ROPE, N_BRANCH * D_MODEL)
    cuts = np.cumsum(sizes)[:-1].tolist()
    wa, wq, wk, wv, wcq, wckv, wkr, wg = jnp.split(w_in[l], cuts, axis=-1)
    row = lambda t: t.reshape(1, -1).astype(F32)
    uq = mla_w_uq[l].reshape(Q_LORA, MLA_HEADS, QK_NOPE + QK_ROPE)
    wqn = uq[:, :, :QK_NOPE].reshape(Q_LORA, MLA_HEADS * QK_NOPE)
    rope_pairs = [_rope_weight_pair(uq[:, h, QK_NOPE:]) for h in range(MLA_HEADS)]
    wqr = jnp.concatenate([p[0] for p in rope_pairs], axis=1)
    wqs = jnp.concatenate([p[1] for p in rope_pairs], axis=1)
    ukv = mla_w_ukv[l].reshape(KV_LORA, MLA_HEADS, QK_NOPE + V_DIM)
    wukv = jnp.concatenate([ukv[:, :, :QK_NOPE].reshape(KV_LORA, -1), ukv[:, :, QK_NOPE:].reshape(KV_LORA, -1)], 1)
    wkr_p, wkrs_p = _rope_weight_pair(wkr)
    return {
        'wa': wa.astype(BF16), 'a_ln_g': row(a_ln_g[l]), 'a_ln_b': row(a_ln_b[l]),
        'ws': a_ws[l].astype(BF16),
        'bs': jnp.broadcast_to(a_bs[l][:, :, None], (A_GROUPS, CHUNK, CHUNK)).astype(F32),
        'wqkv': jnp.concatenate([wq * NA_SCALE, wk, wv], axis=1).astype(BF16),
        'wlat': jnp.concatenate([wcq, wkr_p, wckv, wkrs_p], axis=1).astype(BF16),
        'q_norm': row(mla_q_norm[l]), 'wqn': wqn.astype(BF16), 'wqr': wqr.astype(BF16),
        'wqs': wqs.astype(BF16), 'kv_norm': row(mla_kv_norm[l]), 'wukv': wukv.astype(BF16),
        'na_bias': _natten_bias(na_rpb[l]), 'na_bmax': _natten_bias_max(na_rpb[l]),
        'wg': wg.astype(BF16),
        'wbr': jnp.stack([w_br_a[l], w_br_b[l], w_br_c[l]]).astype(BF16),
        'wo': w_o[l].astype(BF16), 'ln1_g': row(ln1_g[l]), 'ln1_b': row(ln1_b[l]),
        'ln2_g': row(ln2_g[l]), 'ln2_b': row(ln2_b[l]),
        'wgu': jnp.concatenate([w_gate[l], w_up[l]], axis=-1).astype(BF16),
        'wd': w_down[l].astype(BF16),
    }


def _shared_weights(w_router, router_bias):
    wr_t = w_router.T.astype(F32)
    wrh = wr_t.astype(BF16)
    wrl = (wr_t - wrh.astype(F32)).astype(BF16)
    rb = jnp.broadcast_to(router_bias.astype(F32)[:, None], (N_EXPERTS, MERGE_TM))
    return {'wrh': wrh, 'wrl': wrl, 'rb': rb}


def _trunk(x, layers, shared):
    bsz, seq, _ = x.shape
    cos2, sin2 = _rope_tables(seq)
    x2d = x.reshape(bsz * seq, D_MODEL)
    moe = None
    for w in layers:
        if moe is None:
            oa, q, k, v, qm, km, vm, kn2 = _pre_call(x2d, seq, cos2, sin2, w)
        else:
            oa, q, k, v, qm, km, vm, kn2, x2d = _pre_call(None, seq, cos2, sin2, w, moe)
        ob = _natten_call(q, k, v, kn2, w['na_bias'], w['na_bmax'], bsz, seq)
        oc = _mla_call(qm, km, vm, bsz, seq)
        x1, xpk, cls = _merge_call(x2d, oa, ob, oc, w, shared)
        dest3, pad_start, pad_end, block_a, block_b, n_used, n_slots = _dispatch_plan(cls)
        xs, slot_tok = _dispatch_call(xpk, dest3, pad_start, pad_end, n_used, n_slots)
        f = _experts_call(xs, slot_tok, block_a, block_b, n_used, x1.shape[0], w)
        moe = (x1, f, w)
    return _combine_call(x1, f, w).reshape(bsz, seq, D_MODEL)


def kernel(x_prompt, x_sample, w_in, a_ln_g, a_ln_b, a_ws, a_bs, na_rpb, mla_q_norm, mla_kv_norm, mla_w_uq, mla_w_ukv, w_br_a, w_br_b, w_br_c, w_o, ln1_g, ln1_b, ln2_g, ln2_b, w_router, router_bias, w_gate, w_up, w_down):
    layers = [_layer_weights(l, w_in, a_ln_g, a_ln_b, a_ws, a_bs, na_rpb, mla_q_norm, mla_kv_norm,
                             mla_w_uq, mla_w_ukv, w_br_a, w_br_b, w_br_c, w_o, ln1_g, ln1_b, ln2_g,
                             ln2_b, w_gate, w_up, w_down) for l in range(DEPTH)]
    shared = _shared_weights(w_router, router_bias)
    return (_trunk(x_prompt, layers, shared), _trunk(x_sample, layers, shared))
```
